```python
import jax
import jax.numpy as jnp
from jax import lax
import numpy as np

D_MODEL = 1024
BATCH = 2
SEQ = 16384
DEPTH = 2
DEC_BATCH = 16
DEC_SEQ = 16
PAST_LEN = 1024

CHUNK = 64
N_MIXERS = 2
N_DELTA_LAYERS = (DEPTH + 1) // 2
N_LRU_LAYERS = DEPTH // 2
CONV_W = 4
NORM_EPS = 1e-6
GDN_HEADS = 8
GDN_DK = 128
GDN_DV = 128
GDN_QK_W = GDN_HEADS * GDN_DK
GDN_V_W = GDN_HEADS * GDN_DV
GDN_CONV_CH = 2 * GDN_QK_W + GDN_V_W
GDN_IN_W = GDN_CONV_CH + GDN_V_W + 2 * GDN_HEADS
LRU_WIDTH = D_MODEL
LRU_BLOCKS = 8
LRU_BLOCK_W = LRU_WIDTH // LRU_BLOCKS
LRU_C = 8.0
N_GROUPS = 4
EXPERTS_PER_GROUP = 8
N_EXPERTS = N_GROUPS * EXPERTS_PER_GROUP
EXPERT_TOPK = 2
D_FF_EXPERT = 512
MOE_BLOCK = 128

kernel_name = 'hybrid_gdn_rglru_hmoe_stream_step'


def rmsnorm(x, g):
    xf = x.astype(jnp.float32)
    y = xf * lax.rsqrt(jnp.mean(xf * xf, axis=-1, keepdims=True) + NORM_EPS)
    return (y * g.astype(jnp.float32)).astype(x.dtype)


def l2norm(x):
    return x * lax.rsqrt(jnp.sum(x * x, axis=-1, keepdims=True) + 1e-6)


def causal_conv(x, hist, w, b=None):
    L = x.shape[1]
    xp = jnp.concatenate([hist.astype(x.dtype), x], axis=1)
    y = xp[:, 0:L] * w[0]
    for j in range(1, CONV_W):
        y = y + xp[:, j:j + L] * w[j]
    if b is not None:
        y = y + b
    return y, xp[:, xp.shape[1] - (CONV_W - 1):]


def gated_delta_chunked(q, k, v, g, beta, S0):
    B, L, H, DK = q.shape
    DV = v.shape[-1]
    Lp = -(-L // CHUNK) * CHUNK
    pad = Lp - L
    padf = lambda t: jnp.pad(t, [(0, 0), (0, pad)] + [(0, 0)] * (t.ndim - 2))
    q, k, v, g, beta = (padf(t) for t in (q, k, v, g, beta))
    N = Lp // CHUNK

    def blk(t):
        t = t.reshape((B, N, CHUNK) + t.shape[2:])
        return jnp.moveaxis(t, 3, 1)

    qc, kc, vc, gc, bc = (blk(t) for t in (q, k, v, g, beta))
    gcum = jnp.cumsum(gc, axis=-1)
    tri_incl = jnp.tril(jnp.ones((CHUNK, CHUNK), bool))
    tri_strict = jnp.tril(jnp.ones((CHUNK, CHUNK), bool), -1)
    diff = gcum[..., :, None] - gcum[..., None, :]
    decay = jnp.where(tri_incl, jnp.exp(jnp.where(tri_incl, diff, 0.0)), 0.0)
    kb = kc * bc[..., None]
    M = jnp.where(tri_strict, jnp.einsum('bhncd,bhnsd->bhncs', kb, kc) * decay, 0.0)
    eye = jnp.eye(CHUNK, dtype=q.dtype)
    T = lax.linalg.triangular_solve(eye + M, jnp.broadcast_to(eye, M.shape),
                                    left_side=True, lower=True, unit_diagonal=True)
    eg = jnp.exp(gcum)
    w = jnp.einsum('bhncs,bhnsd->bhncd', T, kb * eg[..., None])
    u = jnp.einsum('bhncs,bhnsd->bhncd', T, vc * bc[..., None])
    attn = jnp.einsum('bhncd,bhnsd->bhncs', qc, kc) * decay
    q_dec = qc * eg[..., None]
    k_dec = kc * jnp.exp(gcum[..., -1:] - gcum)[..., None]
    g_last = jnp.exp(gcum[..., -1])

    def step(S, xs):
        w_n, u_n, a_n, qd_n, kd_n, gl_n = xs
        v_new = u_n - jnp.einsum('bhcd,bhde->bhce', w_n, S)
        o = jnp.einsum('bhcd,bhde->bhce', qd_n, S) + jnp.einsum('bhcs,bhse->bhce', a_n, v_new)
        S = S * gl_n[..., None, None] + jnp.einsum('bhcd,bhce->bhde', kd_n, v_new)
        return S, o

    xs = tuple(jnp.moveaxis(t, 2, 0) for t in (w, u, attn, q_dec, k_dec, g_last))
    S, o = lax.scan(step, S0, xs)
    o = jnp.transpose(o, (1, 0, 3, 2, 4)).reshape(B, Lp, H, DV)[:, :L]
    return o, S


def gdn_mixer(h, conv_hist, S0, w_in, conv_w, a_log, dt_bias, norm_g, w_out):
    B, L, _ = h.shape
    f32 = jnp.float32
    proj = h @ w_in
    qkv, z, a, b = jnp.split(proj, [GDN_CONV_CH, GDN_CONV_CH + GDN_V_W, GDN_CONV_CH + GDN_V_W + GDN_HEADS], axis=-1)
    qkv, new_hist = causal_conv(qkv, conv_hist, conv_w)
    qkv = jax.nn.silu(qkv)
    q, k, v = jnp.split(qkv, [GDN_QK_W, 2 * GDN_QK_W], axis=-1)
    q = l2norm(q.reshape(B, L, GDN_HEADS, GDN_DK).astype(f32)) * (GDN_DK ** -0.5)
    k = l2norm(k.reshape(B, L, GDN_HEADS, GDN_DK).astype(f32))
    v = v.reshape(B, L, GDN_HEADS, GDN_DV).astype(f32)
    beta = jax.nn.sigmoid(b.astype(f32))
    g = -jnp.exp(a_log.astype(f32)) * jax.nn.softplus(a.astype(f32) + dt_bias.astype(f32))
    o, S = gated_delta_chunked(q, k, v, g, beta, S0.astype(f32))
    o = rmsnorm(o, norm_g) * jax.nn.silu(z.reshape(B, L, GDN_HEADS, GDN_DV).astype(f32))
    y = o.reshape(B, L, GDN_V_W).astype(h.dtype) @ w_out
    return y, new_hist, S.astype(S0.dtype)


def rglru_mixer(h, conv_hist, h0, w_in, conv_w, conv_b, w_a, b_a, w_x, b_x, lam, w_out):
    B, L, _ = h.shape
    f32 = jnp.float32
    proj = h @ w_in
    gate_br, x_br = jnp.split(proj, 2, axis=-1)
    gate_br = jax.nn.gelu(gate_br)
    xc, new_hist = causal_conv(x_br, conv_hist, conv_w, conv_b)
    xb = xc.reshape(B, L, LRU_BLOCKS, LRU_BLOCK_W)
    r = jax.nn.sigmoid(jnp.einsum('blhi,hij->blhj', xb, w_a).astype(f32) + b_a.reshape(LRU_BLOCKS, LRU_BLOCK_W).astype(f32))
    i = jax.nn.sigmoid(jnp.einsum('blhi,hij->blhj', xb, w_x).astype(f32) + b_x.reshape(LRU_BLOCKS, LRU_BLOCK_W).astype(f32))
    r = r.reshape(B, L, LRU_WIDTH)
    i = i.reshape(B, L, LRU_WIDTH)
    log_a = -LRU_C * r * jax.nn.softplus(-lam.astype(f32))
    a = jnp.exp(log_a)
    mult = jnp.sqrt(-jnp.expm1(2.0 * log_a))
    bx = mult * (i * xc.astype(f32))
    bx = bx.at[:, 0].add(a[:, 0] * h0.astype(f32))

    def comb(left, right):
        a1, b1 = left
        a2, b2 = right
        return a1 * a2, a2 * b1 + b2

    _, hs = lax.associative_scan(comb, (a, bx), axis=1)
    y = (hs.astype(h.dtype) * gate_br) @ w_out
    return y, new_hist, hs[:, -1].astype(h0.dtype)


def expert_dispatch(x, eidx, wts, w1, w3, w2):
    N, D = x.shape
    A = N * EXPERT_TOPK
    flat_e = eidx.reshape(-1)
    order = jnp.argsort(flat_e)
    e_sorted = flat_e[order]
    tok_sorted = order // EXPERT_TOPK
    counts = jnp.zeros((N_EXPERTS,), jnp.int32).at[flat_e].add(1)
    padded = (counts + MOE_BLOCK - 1) // MOE_BLOCK * MOE_BLOCK
    pad_end = jnp.cumsum(padded)
    pad_start = pad_end - padded
    start = jnp.cumsum(counts) - counts
    dest = pad_start[e_sorted] + jnp.arange(A, dtype=jnp.int32) - start[e_sorted]
    n_blocks = -(-A // MOE_BLOCK) + N_EXPERTS
    P = n_blocks * MOE_BLOCK
    slot_tok = jnp.zeros((P,), jnp.int32).at[dest].set(tok_sorted.astype(jnp.int32))
    blk_expert = jnp.minimum(jnp.searchsorted(pad_end, jnp.arange(n_blocks, dtype=jnp.int32) * MOE_BLOCK, side='right'), N_EXPERTS - 1)

    def run_block(args):
        toks, e = args
        xb = x[toks]
        hid = jax.nn.silu(xb @ w1[e]) * (xb @ w3[e])
        return hid @ w2[e]

    yb = lax.map(run_block, (slot_tok.reshape(n_blocks, MOE_BLOCK), blk_expert))
    y_assign = yb.reshape(P, D)[dest] * wts.reshape(-1)[order][:, None]
    return jnp.zeros_like(x).at[tok_sorted].add(y_assign)


def hier_moe(h, w_rg, b_rg, w_re, b_re, w1, w3, w2):
    B, L, D = h.shape
    x = h.reshape(-1, D)
    N = x.shape[0]
    pg = jax.nn.softmax((x @ w_rg + b_rg).astype(jnp.float32), axis=-1)
    gate_g, grp = lax.top_k(pg, 1)
    le = (x @ w_re + b_re).astype(jnp.float32).reshape(N, N_GROUPS, EXPERTS_PER_GROUP)
    le = jnp.take_along_axis(le, jnp.broadcast_to(grp[:, :, None], (N, 1, EXPERTS_PER_GROUP)), axis=1)[:, 0]
    pe = jax.nn.softmax(le, axis=-1)
    we, loc = lax.top_k(pe, EXPERT_TOPK)
    we = we / jnp.sum(we, axis=-1, keepdims=True)
    wts = (gate_g * we).astype(x.dtype)
    eidx = grp * EXPERTS_PER_GROUP + loc
    return expert_dispatch(x, eidx, wts, w1, w3, w2).reshape(B, L, D)


def run_trunk(x, c, gdn_S, gdn_conv, lru_h, lru_conv, p):
    out_S, out_gc, out_h, out_lc = [], [], [], []
    for i in range(DEPTH):
        j = i // N_MIXERS
        mod = jax.nn.silu(c) @ p['w_ada'][i] + p['b_ada'][i]
        sh1, sc1, g1, sh2, sc2, g2 = jnp.split(mod[:, None, :], 6, axis=-1)
        hn = rmsnorm(x, p['norm_mix'][i]) * (1 + sc1) + sh1
        if i % N_MIXERS == 0:
            y, conv_new, S_new = gdn_mixer(hn, gdn_conv[j], gdn_S[j], p['gdn_w_in'][j], p['gdn_conv_w'][j],
                                           p['gdn_a_log'][j], p['gdn_dt_bias'][j], p['gdn_norm'][j], p['gdn_w_out'][j])
            out_S.append(S_new)
            out_gc.append(conv_new)
        else:
            y, conv_new, h_new = rglru_mixer(hn, lru_conv[j], lru_h[j], p['lru_w_in'][j], p['lru_conv_w'][j],
                                             p['lru_conv_b'][j], p['lru_w_a'][j], p['lru_b_a'][j], p['lru_w_x'][j],
                                             p['lru_b_x'][j], p['lru_lambda'][j], p['lru_w_out'][j])
            out_h.append(h_new)
            out_lc.append(conv_new)
        x = x + g1 * y
        hn = rmsnorm(x, p['norm_ffn'][i]) * (1 + sc2) + sh2
        x = x + g2 * hier_moe(hn, p['moe_w_rg'][i], p['moe_b_rg'][i], p['moe_w_re'][i], p['moe_b_re'][i],
                              p['moe_w1'][i], p['moe_w3'][i], p['moe_w2'][i])
    return (rmsnorm(x, p['norm_out']), jnp.stack(out_S), jnp.stack(out_gc), jnp.stack(out_h), jnp.stack(out_lc))


def setup_inputs(seed: int = 0) -> dict:
    key = jax.random.key(seed)
    ks = iter(jax.random.split(key, 40))
    f32 = jnp.float32
    nrm = lambda shape, s: jax.random.normal(next(ks), shape, f32) * s
    D = D_MODEL
    u_dt = jax.random.uniform(next(ks), (N_DELTA_LAYERS, GDN_HEADS), f32, np.log(1e-3), np.log(1e-1))
    dt = jnp.exp(u_dt)
    u_lam = jax.random.uniform(next(ks), (N_LRU_LAYERS, LRU_WIDTH), f32, 0.9, 0.999)
    return {
        'x_prompt': nrm((BATCH, SEQ, D), 1.0),
        'x_sample': nrm((DEC_BATCH, DEC_SEQ, D), 1.0),
        'state_gdn_S': nrm((N_DELTA_LAYERS, DEC_BATCH, GDN_HEADS, GDN_DK, GDN_DV), GDN_DK ** -0.5),
        'state_gdn_conv': nrm((N_DELTA_LAYERS, DEC_BATCH, CONV_W - 1, GDN_CONV_CH), 1.0),
        'state_lru_h': nrm((N_LRU_LAYERS, DEC_BATCH, LRU_WIDTH), 0.5),
        'state_lru_conv': nrm((N_LRU_LAYERS, DEC_BATCH, CONV_W - 1, LRU_WIDTH), 1.0),
        'c_prompt': nrm((BATCH, D), 1.0),
        'c_sample': nrm((DEC_BATCH, D), 1.0),
        'w_ada': nrm((DEPTH, D, 6 * D), 0.5 * D ** -0.5),
        'b_ada': nrm((DEPTH, 6 * D), 0.01),
        'norm_mix': 1.0 + nrm((DEPTH, D), 0.01),
        'norm_ffn': 1.0 + nrm((DEPTH, D), 0.01),
        'norm_out': 1.0 + nrm((D,), 0.01),
        'gdn_w_in': nrm((N_DELTA_LAYERS, D, GDN_IN_W), D ** -0.5),
        'gdn_conv_w': nrm((N_DELTA_LAYERS, CONV_W, GDN_CONV_CH), CONV_W ** -0.5),
        'gdn_a_log': jnp.log(jax.random.uniform(next(ks), (N_DELTA_LAYERS, GDN_HEADS), f32, 1.0, 16.0)),
        'gdn_dt_bias': dt + jnp.log(-jnp.expm1(-dt)),
        'gdn_norm': 1.0 + nrm((N_DELTA_LAYERS, GDN_DV), 0.01),
        'gdn_w_out': nrm((N_DELTA_LAYERS, GDN_V_W, D), GDN_V_W ** -0.5),
        'lru_w_in': nrm((N_LRU_LAYERS, D, 2 * LRU_WIDTH), D ** -0.5),
        'lru_conv_w': nrm((N_LRU_LAYERS, CONV_W, LRU_WIDTH), CONV_W ** -0.5),
        'lru_conv_b': nrm((N_LRU_LAYERS, LRU_WIDTH), 0.01),
        'lru_w_a': nrm((N_LRU_LAYERS, LRU_BLOCKS, LRU_BLOCK_W, LRU_BLOCK_W), LRU_BLOCK_W ** -0.5),
        'lru_b_a': nrm((N_LRU_LAYERS, LRU_WIDTH), 0.01),
        'lru_w_x': nrm((N_LRU_LAYERS, LRU_BLOCKS, LRU_BLOCK_W, LRU_BLOCK_W), LRU_BLOCK_W ** -0.5),
        'lru_b_x': nrm((N_LRU_LAYERS, LRU_WIDTH), 0.01),
        'lru_lambda': jnp.log(u_lam) - jnp.log1p(-u_lam),
        'lru_w_out': nrm((N_LRU_LAYERS, LRU_WIDTH, D), LRU_WIDTH ** -0.5),
        'moe_w_rg': nrm((DEPTH, D, N_GROUPS), D ** -0.5),
        'moe_b_rg': nrm((DEPTH, N_GROUPS), 0.01),
        'moe_w_re': nrm((DEPTH, D, N_EXPERTS), D ** -0.5),
        'moe_b_re': nrm((DEPTH, N_EXPERTS), 0.01),
        'moe_w1': nrm((DEPTH, N_EXPERTS, D, D_FF_EXPERT), D ** -0.5),
        'moe_w3': nrm((DEPTH, N_EXPERTS, D, D_FF_EXPERT), D ** -0.5),
        'moe_w2': nrm((DEPTH, N_EXPERTS, D_FF_EXPERT, D), D_FF_EXPERT ** -0.5),
    }


def reference(x_prompt, x_sample, state_gdn_S, state_gdn_conv, state_lru_h, state_lru_conv, c_prompt, c_sample,
              w_ada, b_ada, norm_mix, norm_ffn, norm_out, gdn_w_in, gdn_conv_w, gdn_a_log, gdn_dt_bias, gdn_norm,
              gdn_w_out, lru_w_in, lru_conv_w, lru_conv_b, lru_w_a, lru_b_a, lru_w_x, lru_b_x, lru_lambda, lru_w_out,
              moe_w_rg, moe_b_rg, moe_w_re, moe_b_re, moe_w1, moe_w3, moe_w2):
    p = dict(w_ada=w_ada, b_ada=b_ada, norm_mix=norm_mix, norm_ffn=norm_ffn, norm_out=norm_out,
             gdn_w_in=gdn_w_in, gdn_conv_w=gdn_conv_w, gdn_a_log=gdn_a_log, gdn_dt_bias=gdn_dt_bias,
             gdn_norm=gdn_norm, gdn_w_out=gdn_w_out, lru_w_in=lru_w_in, lru_conv_w=lru_conv_w,
             lru_conv_b=lru_conv_b, lru_w_a=lru_w_a, lru_b_a=lru_b_a, lru_w_x=lru_w_x, lru_b_x=lru_b_x,
             lru_lambda=lru_lambda, lru_w_out=lru_w_out, moe_w_rg=moe_w_rg, moe_b_rg=moe_b_rg,
             moe_w_re=moe_w_re, moe_b_re=moe_b_re, moe_w1=moe_w1, moe_w3=moe_w3, moe_w2=moe_w2)
    dt = x_prompt.dtype
    z_S = jnp.zeros((N_DELTA_LAYERS, BATCH, GDN_HEADS, GDN_DK, GDN_DV), dt)
    z_gc = jnp.zeros((N_DELTA_LAYERS, BATCH, CONV_W - 1, GDN_CONV_CH), dt)
    z_h = jnp.zeros((N_LRU_LAYERS, BATCH, LRU_WIDTH), dt)
    z_lc = jnp.zeros((N_LRU_LAYERS, BATCH, CONV_W - 1, LRU_WIDTH), dt)
    y_p, gS_p, gc_p, lh_p, lc_p = run_trunk(x_prompt, c_prompt, z_S, z_gc, z_h, z_lc, p)
    y_s, gS_s, gc_s, lh_s, lc_s = run_trunk(x_sample, c_sample, state_gdn_S, state_gdn_conv,
                                            state_lru_h, state_lru_conv, p)
    return (y_p, y_s, gS_p, gc_p, lh_p, lc_p, gS_s, gc_s, lh_s, lc_s)
```

```python
import functools
import math

import jax
import jax.numpy as jnp
from jax import lax
from jax.experimental import pallas as pl
from jax.experimental.pallas import tpu as pltpu

F32 = jnp.float32
BF16 = jnp.bfloat16

D_MODEL = 1024
DEPTH = 2
CONV_W = 4
NORM_EPS = 1e-6
GDN_HEADS = 8
GDN_DK = 128
GDN_DV = 128
GDN_QK_W = GDN_HEADS * GDN_DK
GDN_V_W = GDN_HEADS * GDN_DV
GDN_CONV_CH = 2 * GDN_QK_W + GDN_V_W
GDN_CHUNK = 64
GDN_INV_BLOCK = 16
LRU_WIDTH = D_MODEL
LRU_BLOCKS = 8
LRU_BLOCK_W = LRU_WIDTH // LRU_BLOCKS
LRU_C = 8.0
N_GROUPS = 4
EXPERTS_PER_GROUP = 8
N_EXPERTS = N_GROUPS * EXPERTS_PER_GROUP
EXPERT_TOPK = 2
D_FF_EXPERT = 512
MOE_BLOCK = 128
LANES = 128
ROW_TILE = 256
VMEM_LIMIT = 56 * 1024 * 1024


def _dot(a, b):
    return jnp.dot(a, b, preferred_element_type=F32)


def _dot_nt(a, b):
    return lax.dot_general(a, b, (((1,), (1,)), ((), ())), preferred_element_type=F32)


def _dot_tn(a, b):
    return lax.dot_general(a, b, (((0,), (0,)), ((), ())), preferred_element_type=F32)


def _split3(x):
    a = x.astype(BF16)
    r = x - a.astype(F32)
    b = r.astype(BF16)
    c = (r - b.astype(F32)).astype(BF16)
    return a, b, c


def _rms(x):
    return x * lax.rsqrt(jnp.mean(x * x, axis=-1, keepdims=True) + NORM_EPS)


def _sigmoid(x):
    return 1.0 / (1.0 + jnp.exp(-x))


def _silu(x):
    return x * _sigmoid(x)


def _softplus(x):
    return jnp.maximum(x, 0.0) + jnp.log1p(jnp.exp(-jnp.abs(x)))


def _expm1(x):
    u = jnp.exp(x)
    near = (u > 0.5) & (u < 2.0) & (u != 1.0)
    corrected = (u - 1.0) * x / jnp.where(near, jnp.log(u), 1.0)
    return jnp.where(u == 1.0, x, jnp.where(near, corrected, u - 1.0))


def _params(*sem):
    return pltpu.CompilerParams(dimension_semantics=sem, vmem_limit_bytes=VMEM_LIMIT)


def _const_spec(shape):
    nd = len(shape)
    return pl.BlockSpec(shape, lambda *_: (0,) * nd)


def _ada_kernel(c_ref, w_ref, b_ref, o_ref):
    s = _silu(c_ref[...]).astype(BF16)
    o_ref[0] = _dot(s, w_ref[0].astype(BF16)) + b_ref[0]


def _ada_mod(c_all, w_ada, b_ada):
    bt = c_all.shape[0]
    tn = 1536
    n6 = 6 * D_MODEL
    return pl.pallas_call(
        _ada_kernel,
        grid=(DEPTH, n6 // tn),
        in_specs=[
            pl.BlockSpec((bt, D_MODEL), lambda i, j: (0, 0)),
            pl.BlockSpec((1, D_MODEL, tn), lambda i, j: (i, 0, j)),
            pl.BlockSpec((1, 1, tn), lambda i, j: (i, 0, j)),
        ],
        out_specs=pl.BlockSpec((1, bt, tn), lambda i, j: (i, 0, j)),
        out_shape=jax.ShapeDtypeStruct((DEPTH, bt, n6), F32),
        compiler_params=_params("arbitrary", "arbitrary"),
        name="ada_mod",
    )(c_all, w_ada, b_ada.reshape(DEPTH, 1, n6))


def _route(hn2, wr_hi, wr_lo, br):
    h_hi = hn2.astype(BF16)
    h_lo = (hn2 - h_hi.astype(F32)).astype(BF16)
    logits = _dot(h_hi, wr_hi) + (_dot(h_hi, wr_lo) + _dot(h_lo, wr_hi)) + br
    lane = lax.broadcasted_iota(jnp.int32, logits.shape, 1)
    neg = jnp.float32(-jnp.inf)
    big = jnp.int32(1 << 20)
    is_g = lane < N_GROUPS
    lg = jnp.where(is_g, logits, neg)
    eg = jnp.exp(lg - jnp.max(lg, axis=-1, keepdims=True))
    pg = eg / jnp.sum(eg, axis=-1, keepdims=True)
    pg = jnp.where(is_g, pg, -1.0)
    gate_g = jnp.max(pg, axis=-1, keepdims=True)
    grp = jnp.min(jnp.where(pg == gate_g, lane, big), axis=-1, keepdims=True)
    lo = N_GROUPS + grp * EXPERTS_PER_GROUP
    is_e = (lane >= lo) & (lane < lo + EXPERTS_PER_GROUP)
    le = jnp.where(is_e, logits, neg)
    ee = jnp.exp(le - jnp.max(le, axis=-1, keepdims=True))
    pe = ee / jnp.sum(ee, axis=-1, keepdims=True)
    pe = jnp.where(is_e, pe, -1.0)
    p1 = jnp.max(pe, axis=-1, keepdims=True)
    i1 = jnp.min(jnp.where(pe == p1, lane, big), axis=-1, keepdims=True)
    pe2 = jnp.where(lane == i1, -1.0, pe)
    p2 = jnp.max(pe2, axis=-1, keepdims=True)
    i2 = jnp.min(jnp.where(pe2 == p2, lane, big), axis=-1, keepdims=True)
    tot = p1 + p2
    w1 = gate_g * (p1 / tot)
    w2 = gate_g * (p2 / tot)
    e1 = (i1 - N_GROUPS).astype(F32)
    e2 = (i2 - N_GROUPS).astype(F32)
    return jnp.where(lane == 0, e1, jnp.where(lane == 1, e2, jnp.where(lane == 2, w1, jnp.where(lane == 3, w2, 0.0))))


def _post_mixer(x, y, mod, nf, wr_hi, wr_lo, br):
    g1 = mod[2:3]
    sh2 = mod[3:4]
    sc2 = mod[4:5]
    x1 = x + g1 * y
    hn2 = (_rms(x1) * nf) * (1.0 + sc2) + sh2
    return x1, hn2, _route(hn2, wr_hi, wr_lo, br)


def _gdn_in_kernel(x_ref, mod_ref, ng_ref, wqkv_ref, wz_ref, wab_ref, cw_ref, hist0_ref, alog_ref, dtb_ref,
                   q_ref, k_ref, v_ref, z_ref, gbc_ref, gbr_ref, hist_ref, xp_ref, *, tm, chunk):
    l = pl.program_id(1)

    @pl.when(l == 0)
    def _():
        xp_ref[0:8, :] = jnp.zeros((8, GDN_CONV_CH), F32)
        xp_ref[5:8, :] = hist0_ref[0]

    x = x_ref[0]
    mod = mod_ref[0]
    hn = (_rms(x) * ng_ref[...]) * (1.0 + mod[1:2]) + mod[0:1]
    hb = hn.astype(BF16)
    qkv = _dot(hb, wqkv_ref[...])
    xp_ref[8:8 + tm, :] = qkv
    cw = cw_ref[...]
    y = cw[3:4] * qkv
    for j in range(CONV_W - 1):
        y = y + cw[j:j + 1] * xp_ref[5 + j:5 + j + tm, :]
    new_hist = xp_ref[tm + 5:tm + 8, :]
    xp_ref[5:8, :] = new_hist
    hist_ref[0] = new_hist
    y = _silu(y)
    for h in range(GDN_HEADS):
        s = slice(h * GDN_DK, (h + 1) * GDN_DK)
        qh = y[:, s]
        q_ref[0, :, s] = qh * lax.rsqrt(jnp.sum(qh * qh, axis=-1, keepdims=True) + 1e-6) * (GDN_DK ** -0.5)
        kh = y[:, GDN_QK_W + h * GDN_DK:GDN_QK_W + (h + 1) * GDN_DK]
        k_ref[0, :, s] = kh * lax.rsqrt(jnp.sum(kh * kh, axis=-1, keepdims=True) + 1e-6)
    v_ref[0] = y[:, 2 * GDN_QK_W:]
    z_ref[0] = _dot(hb, wz_ref[...])
    ab = _dot(hb, wab_ref[...])
    g = -jnp.exp(alog_ref[...]) * _softplus(ab + dtb_ref[...])
    beta = _sigmoid(ab)
    ri = lax.broadcasted_iota(jnp.int32, (tm, tm), 0)
    ci = lax.broadcasted_iota(jnp.int32, (tm, tm), 1)
    tri = jnp.where((ri // chunk == ci // chunk) & (ci <= ri), 1.0, 0.0).astype(BF16)
    g1, g2, g3 = _split3(g)
    gcum = (_dot(tri, g1) + _dot(tri, g2)) + _dot(tri, g3)
    lane = lax.broadcasted_iota(jnp.int32, (tm, LANES), 1)
    gb = jnp.where(lane < GDN_HEADS, gcum, beta)
    gbc_ref[0] = gb
    er = lax.broadcasted_iota(jnp.int32, (2 * GDN_HEADS, LANES), 0)
    ec = lax.broadcasted_iota(jnp.int32, (2 * GDN_HEADS, LANES), 1)
    sel = jnp.where(er == ec, 1.0, 0.0).astype(BF16)
    b1, b2, b3 = _split3(gb)
    for n in range(tm // chunk):
        r = slice(n * chunk, (n + 1) * chunk)
        gbr_ref[0, n] = (_dot_nt(sel, b1[r]) + _dot_nt(sel, b2[r])) + _dot_nt(sel, b3[r])


def _gdn_in(x, mod, ng, wqkv, wz, wab, cw, hist0, alog, dtb, tm, chunk):
    b, l, d = x.shape
    grid = (b, l // tm)
    row = lambda shape: pl.BlockSpec(shape, lambda i, j: (i, j, 0))
    outs = pl.pallas_call(
        functools.partial(_gdn_in_kernel, tm=tm, chunk=chunk),
        grid=grid,
        in_specs=[
            row((1, tm, d)),
            pl.BlockSpec((1, 6, d), lambda i, j: (i, 0, 0)),
            _const_spec((1, d)),
            _const_spec(wqkv.shape),
            _const_spec(wz.shape),
            _const_spec(wab.shape),
            _const_spec(cw.shape),
            pl.BlockSpec((1, CONV_W - 1, GDN_CONV_CH), lambda i, j: (i, 0, 0)),
            _const_spec((1, LANES)),
            _const_spec((1, LANES)),
        ],
        out_specs=[
            row((1, tm, GDN_QK_W)), row((1, tm, GDN_QK_W)), row((1, tm, GDN_V_W)), row((1, tm, GDN_V_W)),
            row((1, tm, LANES)),
            pl.BlockSpec((1, tm // chunk, 2 * GDN_HEADS, chunk), lambda i, j: (i, j, 0, 0)),
            pl.BlockSpec((1, CONV_W - 1, GDN_CONV_CH), lambda i, j: (i, 0, 0)),
        ],
        out_shape=[
            jax.ShapeDtypeStruct((b, l, GDN_QK_W), F32), jax.ShapeDtypeStruct((b, l, GDN_QK_W), F32),
            jax.ShapeDtypeStruct((b, l, GDN_V_W), F32), jax.ShapeDtypeStruct((b, l, GDN_V_W), F32),
            jax.ShapeDtypeStruct((b, l, LANES), F32),
            jax.ShapeDtypeStruct((b, l // chunk, 2 * GDN_HEADS, chunk), F32),
            jax.ShapeDtypeStruct((b, CONV_W - 1, GDN_CONV_CH), F32),
        ],
        scratch_shapes=[pltpu.VMEM((tm + 8, GDN_CONV_CH), F32)],
        compiler_params=_params("arbitrary", "arbitrary"),
        name="gdn_in",
    )(x, mod, ng, wqkv, wz, wab, cw, hist0, alog, dtb)
    return outs


def _gdn_chunk_kernel(q_ref, k_ref, v_ref, gbc_ref, gbr_ref, s0_ref, o_ref, sout_ref, s_ref, *, tm, chunk):
    l = pl.program_id(1)

    @pl.when(l == 0)
    def _():
        s_ref[...] = s0_ref[0]

    ri = lax.broadcasted_iota(jnp.int32, (chunk, chunk), 0)
    ci = lax.broadcasted_iota(jnp.int32, (chunk, chunk), 1)
    incl = ri >= ci
    strict = ri > ci
    eye = jnp.where(ri == ci, 1.0, 0.0).astype(F32)
    base = min(GDN_INV_BLOCK, chunk)
    levels = int(math.log2(base))
    diag_blk = (ri // base) == (ci // base)
    merge_masks = []
    blk = base
    while blk < chunk:
        merge_masks.append(((ri // (2 * blk)) == (ci // (2 * blk))) & ((ri // blk) % 2 == 1) & ((ci // blk) % 2 == 0))
        blk *= 2

    def one_chunk(n, carry):
        r0 = pl.multiple_of(n * chunk, chunk)
        gbc = gbc_ref[0, pl.ds(r0, chunk), :]
        gbr = gbr_ref[0, n]
        for h in range(GDN_HEADS):
            s = slice(h * GDN_DK, (h + 1) * GDN_DK)
            q = q_ref[0, pl.ds(r0, chunk), s]
            k = k_ref[0, pl.ds(r0, chunk), s]
            v = v_ref[0, pl.ds(r0, chunk), s]
            gc = gbc[:, h:h + 1]
            bc = gbc[:, GDN_HEADS + h:GDN_HEADS + h + 1]
            gr = gbr[h:h + 1, :]
            decay = jnp.where(incl, jnp.exp(jnp.where(incl, gc - gr, 0.0)), 0.0)
            kb = k * bc
            kbf = k.astype(BF16)
            a2 = _dot_nt(jnp.concatenate([q, kb], axis=0).astype(BF16), kbf)
            attn = a2[:chunk] * decay
            m = jnp.where(strict, a2[chunk:] * decay, 0.0)
            nq = jnp.where(diag_blk, -m, 0.0)
            p = eye + nq
            nq = _dot(nq.astype(BF16), nq.astype(BF16))
            for j in range(1, levels):
                if j < levels - 1:
                    r = _dot(nq.astype(BF16), jnp.concatenate([nq, p], axis=1).astype(BF16))
                    nq, inc = r[:, :chunk], r[:, chunk:]
                else:
                    inc = _dot(nq.astype(BF16), p.astype(BF16))
                p = p + inc
            for low in merge_masks:
                pb = p.astype(BF16)
                p = p - _dot(pb, _dot(jnp.where(low, m, 0.0).astype(BF16), pb).astype(BF16))
            eg = jnp.exp(gc)
            wu = _dot(p.astype(BF16), jnp.concatenate([kb * eg, v * bc], axis=1).astype(BF16))
            w = wu[:, :GDN_DK]
            u = wu[:, GDN_DK:]
            st = s_ref[h]
            g_last = gc[chunk - 1:chunk, :]
            ws = _dot(jnp.concatenate([w, q * eg], axis=0).astype(BF16), st.astype(BF16))
            v_new = u - ws[:chunk]
            vb = v_new.astype(BF16)
            o_ref[0, pl.ds(r0, chunk), s] = ws[chunk:] + _dot(attn.astype(BF16), vb)
            kd = k * jnp.exp(g_last - gc)
            s_ref[h] = st * jnp.exp(g_last) + _dot_tn(kd.astype(BF16), vb)
        return carry

    lax.fori_loop(0, tm // chunk, one_chunk, 0)
    sout_ref[0] = s_ref[...]


def _gdn_chunks(q, k, v, gbc, gbr, s0, tm, chunk):
    b, l, _ = q.shape
    row = lambda shape: pl.BlockSpec(shape, lambda i, j: (i, j, 0))
    sspec = pl.BlockSpec((1, GDN_HEADS, GDN_DK, GDN_DV), lambda i, j: (i, 0, 0, 0))
    return pl.pallas_call(
        functools.partial(_gdn_chunk_kernel, tm=tm, chunk=chunk),
        grid=(b, l // tm),
        in_specs=[
            row((1, tm, GDN_QK_W)), row((1, tm, GDN_QK_W)), row((1, tm, GDN_V_W)), row((1, tm, LANES)),
            pl.BlockSpec((1, tm // chunk, 2 * GDN_HEADS, chunk), lambda i, j: (i, j, 0, 0)),
            sspec,
        ],
        out_specs=[row((1, tm, GDN_V_W)), sspec],
        out_shape=[jax.ShapeDtypeStruct((b, l, GDN_V_W), F32),
                   jax.ShapeDtypeStruct((b, GDN_HEADS, GDN_DK, GDN_DV), F32)],
        scratch_shapes=[pltpu.VMEM((GDN_HEADS, GDN_DK, GDN_DV), F32)],
        compiler_params=_params("arbitrary", "arbitrary"),
        name="gdn_chunks",
    )(q, k, v, gbc, gbr, s0)


def _gdn_out_kernel(o_ref, z_ref, x_ref, mod_ref, gn_ref, wout_ref, nf_ref, wrh_ref, wrl_ref, br_ref,
                    x1_ref, hn2_ref, route_ref):
    o = o_ref[0]
    z = z_ref[0]
    gn = gn_ref[...]
    parts = []
    for h in range(GDN_HEADS):
        s = slice(h * GDN_DV, (h + 1) * GDN_DV)
        parts.append((_rms(o[:, s]) * gn) * _silu(z[:, s]))
    on = jnp.concatenate(parts, axis=1).astype(BF16)
    y = _dot(on, wout_ref[...])
    x1, hn2, route = _post_mixer(x_ref[0], y, mod_ref[0], nf_ref[...], wrh_ref[...], wrl_ref[...], br_ref[...])
    x1_ref[0] = x1
    hn2_ref[0] = hn2
    route_ref[0] = route


def _gdn_out(o, z, x, mod, gn, wout, nf, wrh, wrl, br, tm):
    b, l, d = x.shape
    row = lambda shape: pl.BlockSpec(shape, lambda i, j: (i, j, 0))
    return pl.pallas_call(
        _gdn_out_kernel,
        grid=(b, l // tm),
        in_specs=[
            row((1, tm, GDN_V_W)), row((1, tm, GDN_V_W)), row((1, tm, d)),
            pl.BlockSpec((1, 6, d), lambda i, j: (i, 0, 0)),
            _const_spec((1, GDN_DV)), _const_spec(wout.shape), _const_spec((1, d)),
            _const_spec(wrh.shape), _const_spec(wrl.shape), _const_spec((1, LANES)),
        ],
        out_specs=[row((1, tm, d)), row((1, tm, d)), row((1, tm, LANES))],
        out_shape=[jax.ShapeDtypeStruct((b, l, d), F32), jax.ShapeDtypeStruct((b, l, d), F32),
                   jax.ShapeDtypeStruct((b, l, LANES), F32)],
        compiler_params=_params("arbitrary", "arbitrary"),
        name="gdn_out",
    )(o, z, x, mod, gn, wout, nf, wrh, wrl, br)


def _gelu_tanh(x):
    return 0.5 * x * (1.0 + jnp.tanh(math.sqrt(2.0 / math.pi) * (x + 0.044715 * (x * x * x))))


def _lru_kernel(x_ref, mod_ref, ng_ref, win_ref, cw_ref, cb_ref, wax_ref, ba_ref, bx_ref, lam_ref, wout_ref,
                hist0_ref, h0_ref, nf_ref, wrh_ref, wrl_ref, br_ref,
                x1_ref, hn2_ref, route_ref, hist_ref, hlast_ref, xp_ref, h_ref, *, tm):
    l = pl.program_id(1)

    @pl.when(l == 0)
    def _():
        xp_ref[0:8, :] = jnp.zeros((8, LRU_WIDTH), F32)
        xp_ref[5:8, :] = hist0_ref[0]
        h_ref[...] = h0_ref[0]

    x = x_ref[0]
    mod = mod_ref[0]
    hn = (_rms(x) * ng_ref[...]) * (1.0 + mod[1:2]) + mod[0:1]
    proj = _dot(hn.astype(BF16), win_ref[...])
    gate_br = _gelu_tanh(proj[:, :LRU_WIDTH])
    xb = proj[:, LRU_WIDTH:]
    xp_ref[8:8 + tm, :] = xb
    cw = cw_ref[...]
    xc = cw[3:4] * xb
    for j in range(CONV_W - 1):
        xc = xc + cw[j:j + 1] * xp_ref[5 + j:5 + j + tm, :]
    xc = xc + cb_ref[...]
    new_hist = xp_ref[tm + 5:tm + 8, :]
    xp_ref[5:8, :] = new_hist
    hist_ref[0] = new_hist
    xcb = xc.astype(BF16)
    ra, ia = [], []
    for h in range(LRU_BLOCKS):
        s = slice(h * LRU_BLOCK_W, (h + 1) * LRU_BLOCK_W)
        r2 = _dot(xcb[:, s], wax_ref[h])
        ra.append(r2[:, :LRU_BLOCK_W])
        ia.append(r2[:, LRU_BLOCK_W:])
    r = _sigmoid(jnp.concatenate(ra, axis=1) + ba_ref[...])
    i = _sigmoid(jnp.concatenate(ia, axis=1) + bx_ref[...])
    log_a = (-LRU_C * r) * _softplus(-lam_ref[...])
    a = jnp.exp(log_a)
    mult = jnp.sqrt(-_expm1(2.0 * log_a))
    b = mult * (i * xc)
    rowi = lax.broadcasted_iota(jnp.int32, (tm, LRU_WIDTH), 0)
    b = b + jnp.where(rowi == 0, a * h_ref[...], 0.0)
    sft = 1
    while sft < tm:
        keep = rowi >= sft
        a_prev = jnp.where(keep, pltpu.roll(a, sft, 0), 1.0)
        b_prev = jnp.where(keep, pltpu.roll(b, sft, 0), 0.0)
        b = a * b_prev + b
        a = a * a_prev
        sft *= 2
    hs = b
    h_last = hs[tm - 1:tm, :]
    h_ref[...] = h_last
    hlast_ref[0] = h_last
    y = _dot((hs * gate_br).astype(BF16), wout_ref[...])
    x1, hn2, route = _post_mixer(x, y, mod, nf_ref[...], wrh_ref[...], wrl_ref[...], br_ref[...])
    x1_ref[0] = x1
    hn2_ref[0] = hn2
    route_ref[0] = route


def _lru_layer(x, mod, ng, win, cw, cb, wax, ba, bx, lam, wout, hist0, h0, nf, wrh, wrl, br, tm):
    b, l, d = x.shape
    row = lambda shape: pl.BlockSpec(shape, lambda i, j: (i, j, 0))
    per_b = lambda shape: pl.BlockSpec(shape, lambda i, j: (i, 0, 0))
    vec = _const_spec((1, d))
    return pl.pallas_call(
        functools.partial(_lru_kernel, tm=tm),
        grid=(b, l // tm),
        in_specs=[
            row((1, tm, d)), per_b((1, 6, d)), vec, _const_spec(win.shape), _const_spec(cw.shape), vec,
            _const_spec(wax.shape), vec, vec, vec, _const_spec(wout.shape),
            per_b((1, CONV_W - 1, LRU_WIDTH)), per_b((1, 1, LRU_WIDTH)), vec,
            _const_spec(wrh.shape), _const_spec(wrl.shape), _const_spec((1, LANES)),
        ],
        out_specs=[row((1, tm, d)), row((1, tm, d)), row((1, tm, LANES)),
                   per_b((1, CONV_W - 1, LRU_WIDTH)), per_b((1, 1, LRU_WIDTH))],
        out_shape=[jax.ShapeDtypeStruct((b, l, d), F32), jax.ShapeDtypeStruct((b, l, d), F32),
                   jax.ShapeDtypeStruct((b, l, LANES), F32),
                   jax.ShapeDtypeStruct((b, CONV_W - 1, LRU_WIDTH), F32),
                   jax.ShapeDtypeStruct((b, 1, LRU_WIDTH), F32)],
        scratch_shapes=[pltpu.VMEM((tm + 8, LRU_WIDTH), F32), pltpu.VMEM((1, LRU_WIDTH), F32)],
        compiler_params=_params("arbitrary", "arbitrary"),
        name="lru_layer",
    )(x, mod, ng, win, cw, cb, wax, ba, bx, lam, wout, hist0, h0, nf, wrh, wrl, br)


def _moe_kernel(be_ref, xs_ref, w1_ref, w3_ref, w2_ref, y_ref, w1b_ref, w3b_ref, w2b_ref):
    i = pl.program_id(0)
    prev = be_ref[jnp.maximum(i - 1, 0)]

    @pl.when((i == 0) | (be_ref[i] != prev))
    def _():
        w1b_ref[...] = w1_ref[0].astype(BF16)
        w3b_ref[...] = w3_ref[0].astype(BF16)
        w2b_ref[...] = w2_ref[0].astype(BF16)

    xb = xs_ref[...].astype(BF16)
    hid = _silu(_dot(xb, w1b_ref[...])) * _dot(xb, w3b_ref[...])
    y_ref[...] = _dot(hid.astype(BF16), w2b_ref[...])


def _moe_blocks(blk_expert, xs, w1, w3, w2):
    p, d = xs.shape
    nb = p // MOE_BLOCK
    grid_spec = pltpu.PrefetchScalarGridSpec(
        num_scalar_prefetch=1,
        grid=(nb,),
        in_specs=[
            pl.BlockSpec((MOE_BLOCK, d), lambda i, be: (i, 0)),
            pl.BlockSpec((1, d, D_FF_EXPERT), lambda i, be: (be[i], 0, 0)),
            pl.BlockSpec((1, d, D_FF_EXPERT), lambda i, be: (be[i], 0, 0)),
            pl.BlockSpec((1, D_FF_EXPERT, d), lambda i, be: (be[i], 0, 0)),
        ],
        out_specs=pl.BlockSpec((MOE_BLOCK, d), lambda i, be: (i, 0)),
        scratch_shapes=[pltpu.VMEM((d, D_FF_EXPERT), BF16), pltpu.VMEM((d, D_FF_EXPERT), BF16),
                        pltpu.VMEM((D_FF_EXPERT, d), BF16)],
    )
    return pl.pallas_call(
        _moe_kernel,
        grid_spec=grid_spec,
        out_shape=jax.ShapeDtypeStruct((p, d), F32),
        compiler_params=_params("arbitrary"),
        name="moe_blocks",
    )(blk_expert, xs, w1, w3, w2)


def _combine_kernel(x1_ref, ya_ref, yb_ref, route_ref, mod_ref, no_ref, o_ref, *, final_norm):
    route = route_ref[0]
    moe = ya_ref[0] * route[:, 2:3] + yb_ref[0] * route[:, 3:4]
    x2 = x1_ref[0] + mod_ref[0][5:6] * moe
    if final_norm:
        x2 = _rms(x2) * no_ref[...]
    o_ref[0] = x2


def _combine(x1, ya, yb, route, mod, norm_out, tm, final_norm):
    b, l, d = x1.shape
    row = lambda shape: pl.BlockSpec(shape, lambda i, j: (i, j, 0))
    return pl.pallas_call(
        functools.partial(_combine_kernel, final_norm=final_norm),
        grid=(b, l // tm),
        in_specs=[row((1, tm, d)), row((1, tm, d)), row((1, tm, d)), row((1, tm, LANES)),
                  pl.BlockSpec((1, 6, d), lambda i, j: (i, 0, 0)), _const_spec((1, d))],
        out_specs=row((1, tm, d)),
        out_shape=jax.ShapeDtypeStruct((b, l, d), F32),
        compiler_params=_params("arbitrary", "arbitrary"),
        name="combine",
    )(x1, ya, yb, route, mod, norm_out)


def _moe(x1, hn2, route, mod, w1, w3, w2, norm_out, tm, final_norm):
    b, l, d = x1.shape
    n = b * l
    a = n * EXPERT_TOPK
    flat_e = route.reshape(n, LANES)[:, :EXPERT_TOPK].astype(jnp.int32).reshape(-1)
    onehot = (flat_e[:, None] == jnp.arange(N_EXPERTS, dtype=jnp.int32)[None, :]).astype(jnp.int32)
    rank = jnp.take_along_axis(jnp.cumsum(onehot, axis=0), flat_e[:, None], axis=1)[:, 0] - 1
    counts = jnp.sum(onehot, axis=0)
    padded = (counts + MOE_BLOCK - 1) // MOE_BLOCK * MOE_BLOCK
    pad_end = jnp.cumsum(padded)
    pad_start = pad_end - padded
    dest = pad_start[flat_e] + rank
    n_blocks = -(-a // MOE_BLOCK) + N_EXPERTS
    p = n_blocks * MOE_BLOCK
    slot_tok = jnp.zeros((p,), jnp.int32).at[dest].set(jnp.arange(a, dtype=jnp.int32) // EXPERT_TOPK)
    blk_expert = jnp.minimum(
        jnp.searchsorted(pad_end, jnp.arange(n_blocks, dtype=jnp.int32) * MOE_BLOCK, side='right'),
        N_EXPERTS - 1).astype(jnp.int32)
    xs = hn2.reshape(n, d)[slot_tok]
    ys = _moe_blocks(blk_expert, xs, w1, w3, w2)
    yab = ys[dest].reshape(n, EXPERT_TOPK, d)
    ya = yab[:, 0].reshape(b, l, d)
    yb = yab[:, 1].reshape(b, l, d)
    return _combine(x1, ya, yb, route, mod, norm_out, tm, final_norm)


def _trunk(x, mods, gdn_s, gdn_conv, lru_h, lru_conv, wp):
    b, l, d = x.shape
    tm = min(ROW_TILE, l)
    chunk = min(GDN_CHUNK, l)
    assert l % tm == 0 and tm % chunk == 0 and chunk & (chunk - 1) == 0
    mod = mods[0]
    q, k, v, z, gbc, gbr, gconv_new = _gdn_in(x, mod, wp['norm_mix0'], wp['gdn_wqkv'], wp['gdn_wz'], wp['gdn_wab'],
                                              wp['gdn_conv_w'], gdn_conv, wp['gdn_alog'], wp['gdn_dtb'], tm, chunk)
    o, s_new = _gdn_chunks(q, k, v, gbc, gbr, gdn_s, tm, chunk)
    x1, hn2, route = _gdn_out(o, z, x, mod, wp['gdn_norm'], wp['gdn_wout'], wp['norm_ffn0'],
                              wp['wr_hi0'], wp['wr_lo0'], wp['br0'], tm)
    x = _moe(x1, hn2, route, mod, wp['moe_w1'][0], wp['moe_w3'][0], wp['moe_w2'][0], wp['norm_out'], tm, False)
    mod = mods[1]
    x1, hn2, route, lconv_new, h_new = _lru_layer(
        x, mod, wp['norm_mix1'], wp['lru_win'], wp['lru_conv_w'], wp['lru_conv_b'], wp['lru_wax'], wp['lru_ba'],
        wp['lru_bx'], wp['lru_lam'], wp['lru_wout'], lru_conv, lru_h, wp['norm_ffn1'],
        wp['wr_hi1'], wp['wr_lo1'], wp['br1'], tm)
    y = _moe(x1, hn2, route, mod, wp['moe_w1'][1], wp['moe_w3'][1], wp['moe_w2'][1], wp['norm_out'], tm, True)
    return y, s_new[None], gconv_new[None], h_new.reshape(1, b, LRU_WIDTH), lconv_new[None]


def _pad_lanes(v, width=LANES):
    v = v.reshape(1, -1)
    return jnp.pad(v, ((0, 0), (0, width - v.shape[1])))


def _router_weights(w_rg, b_rg, w_re, b_re):
    w = jnp.pad(jnp.concatenate([w_rg, w_re], axis=1), ((0, 0), (0, LANES - N_GROUPS - N_EXPERTS)))
    hi = w.astype(BF16)
    lo = (w - hi.astype(F32)).astype(BF16)
    return hi, lo, _pad_lanes(jnp.concatenate([b_rg, b_re]))


def kernel(x_prompt, x_sample, state_gdn_S, state_gdn_conv, state_lru_h, state_lru_conv, c_prompt, c_sample, w_ada, b_ada, norm_mix, norm_ffn, norm_out, gdn_w_in, gdn_conv_w, gdn_a_log, gdn_dt_bias, gdn_norm, gdn_w_out, lru_w_in, lru_conv_w, lru_conv_b, lru_w_a, lru_b_a, lru_w_x, lru_b_x, lru_lambda, lru_w_out, moe_w_rg, moe_b_rg, moe_w_re, moe_b_re, moe_w1, moe_w3, moe_w2):
    d = D_MODEL
    bp = x_prompt.shape[0]
    bs = x_sample.shape[0]
    win = gdn_w_in[0]
    wab = jnp.pad(win[:, GDN_CONV_CH + GDN_V_W:], ((0, 0), (0, LANES - 2 * GDN_HEADS)))
    wp = dict(
        norm_mix0=norm_mix[0].reshape(1, d), norm_mix1=norm_mix[1].reshape(1, d),
        norm_ffn0=norm_ffn[0].reshape(1, d), norm_ffn1=norm_ffn[1].reshape(1, d),
        norm_out=norm_out.reshape(1, d),
        gdn_wqkv=win[:, :GDN_CONV_CH].astype(BF16),
        gdn_wz=win[:, GDN_CONV_CH:GDN_CONV_CH + GDN_V_W].astype(BF16),
        gdn_wab=wab.astype(BF16),
        gdn_conv_w=gdn_conv_w[0],
        gdn_alog=_pad_lanes(gdn_a_log[0]), gdn_dtb=_pad_lanes(gdn_dt_bias[0]),
        gdn_norm=gdn_norm[0].reshape(1, GDN_DV), gdn_wout=gdn_w_out[0].astype(BF16),
        lru_win=lru_w_in[0].astype(BF16), lru_conv_w=lru_conv_w[0], lru_conv_b=lru_conv_b[0].reshape(1, d),
        lru_wax=jnp.concatenate([lru_w_a[0], lru_w_x[0]], axis=-1).astype(BF16),
        lru_ba=lru_b_a[0].reshape(1, d), lru_bx=lru_b_x[0].reshape(1, d), lru_lam=lru_lambda[0].reshape(1, d),
        lru_wout=lru_w_out[0].astype(BF16),
        moe_w1=moe_w1, moe_w3=moe_w3, moe_w2=moe_w2,
    )
    for i in range(DEPTH):
        wp[f'wr_hi{i}'], wp[f'wr_lo{i}'], wp[f'br{i}'] = _router_weights(moe_w_rg[i], moe_b_rg[i], moe_w_re[i], moe_b_re[i])

    mods = _ada_mod(jnp.concatenate([c_prompt, c_sample], axis=0), w_ada, b_ada)
    mods = mods.reshape(DEPTH, bp + bs, 6, d)
    mods_p = [mods[i, :bp] for i in range(DEPTH)]
    mods_s = [mods[i, bp:] for i in range(DEPTH)]

    dt = x_prompt.dtype
    z_s = jnp.zeros((bp, GDN_HEADS, GDN_DK, GDN_DV), dt)
    z_gc = jnp.zeros((bp, CONV_W - 1, GDN_CONV_CH), dt)
    z_h = jnp.zeros((bp, 1, LRU_WIDTH), dt)
    z_lc = jnp.zeros((bp, CONV_W - 1, LRU_WIDTH), dt)
    y_p, gs_p, gc_p, lh_p, lc_p = _trunk(x_prompt, mods_p, z_s, z_gc, z_h, z_lc, wp)
    y_s, gs_s, gc_s, lh_s, lc_s = _trunk(x_sample, mods_s, state_gdn_S[0], state_gdn_conv[0],
                                         state_lru_h[0].reshape(bs, 1, LRU_WIDTH), state_lru_conv[0], wp)
    return (y_p, y_s, gs_p, gc_p, lh_p, lc_p, gs_s, gc_s, lh_s, lc_s)
```

```python
import functools
import math

import jax
import jax.numpy as jnp
from jax import lax
from jax.experimental import pallas as pl
from jax.experimental.pallas import tpu as pltpu

F32 = jnp.float32
BF16 = jnp.bfloat16

D_MODEL = 1024
DEPTH = 2
CONV_W = 4
NORM_EPS = 1e-6
GDN_HEADS = 8
GDN_DK = 128
GDN_DV = 128
GDN_QK_W = GDN_HEADS * GDN_DK
GDN_V_W = GDN_HEADS * GDN_DV
GDN_CONV_CH = 2 * GDN_QK_W + GDN_V_W
GDN_CHUNK = 64
GDN_INV_BLOCK = 16
LRU_WIDTH = D_MODEL
LRU_BLOCKS = 8
LRU_BLOCK_W = LRU_WIDTH // LRU_BLOCKS
LRU_C = 8.0
N_GROUPS = 4
EXPERTS_PER_GROUP = 8
N_EXPERTS = N_GROUPS * EXPERTS_PER_GROUP
EXPERT_TOPK = 2
D_FF_EXPERT = 512
MOE_BLOCK = 128
LANES = 128
ROW_TILE = 256
VMEM_LIMIT = 56 * 1024 * 1024


def _dot(a, b):
    return jnp.dot(a, b, preferred_element_type=F32)


def _dot_nt(a, b):
    return lax.dot_general(a, b, (((1,), (1,)), ((), ())), preferred_element_type=F32)


def _dot_tn(a, b):
    return lax.dot_general(a, b, (((0,), (0,)), ((), ())), preferred_element_type=F32)


def _split3(x):
    a = x.astype(BF16)
    r = x - a.astype(F32)
    b = r.astype(BF16)
    c = (r - b.astype(F32)).astype(BF16)
    return a, b, c


def _rms(x):
    return x * lax.rsqrt(jnp.mean(x * x, axis=-1, keepdims=True) + NORM_EPS)


def _sigmoid(x):
    return 1.0 / (1.0 + jnp.exp(-x))


def _silu(x):
    return x * _sigmoid(x)


def _softplus(x):
    return jnp.maximum(x, 0.0) + jnp.log1p(jnp.exp(-jnp.abs(x)))


def _expm1(x):
    u = jnp.exp(x)
    near = (u > 0.5) & (u < 2.0) & (u != 1.0)
    corrected = (u - 1.0) * x / jnp.where(near, jnp.log(u), 1.0)
    return jnp.where(u == 1.0, x, jnp.where(near, corrected, u - 1.0))


def _params(*sem):
    return pltpu.CompilerParams(dimension_semantics=sem, vmem_limit_bytes=VMEM_LIMIT)


def _const_spec(shape):
    nd = len(shape)
    return pl.BlockSpec(shape, lambda *_: (0,) * nd)


def _ada_kernel(c_ref, w_ref, b_ref, o_ref):
    s = _silu(c_ref[...]).astype(BF16)
    o_ref[0] = _dot(s, w_ref[0].astype(BF16)) + b_ref[0]


def _ada_mod(c_all, w_ada, b_ada):
    bt = c_all.shape[0]
    tn = 1536
    n6 = 6 * D_MODEL
    return pl.pallas_call(
        _ada_kernel,
        grid=(DEPTH, n6 // tn),
        in_specs=[
            pl.BlockSpec((bt, D_MODEL), lambda i, j: (0, 0)),
            pl.BlockSpec((1, D_MODEL, tn), lambda i, j: (i, 0, j)),
            pl.BlockSpec((1, 1, tn), lambda i, j: (i, 0, j)),
        ],
        out_specs=pl.BlockSpec((1, bt, tn), lambda i, j: (i, 0, j)),
        out_shape=jax.ShapeDtypeStruct((DEPTH, bt, n6), F32),
        compiler_params=_params("arbitrary", "arbitrary"),
        name="ada_mod",
    )(c_all, w_ada, b_ada.reshape(DEPTH, 1, n6))


def _route(hn2, wr_hi, wr_lo, br, cnt_ref):
    h_hi = hn2.astype(BF16)
    h_lo = (hn2 - h_hi.astype(F32)).astype(BF16)
    logits = _dot(h_hi, wr_hi) + (_dot(h_hi, wr_lo) + _dot(h_lo, wr_hi)) + br
    lane = lax.broadcasted_iota(jnp.int32, logits.shape, 1)
    neg = jnp.float32(-jnp.inf)
    big = jnp.int32(1 << 20)
    is_g = lane < N_GROUPS
    lg = jnp.where(is_g, logits, neg)
    eg = jnp.exp(lg - jnp.max(lg, axis=-1, keepdims=True))
    pg = eg / jnp.sum(eg, axis=-1, keepdims=True)
    pg = jnp.where(is_g, pg, -1.0)
    gate_g = jnp.max(pg, axis=-1, keepdims=True)
    grp = jnp.min(jnp.where(pg == gate_g, lane, big), axis=-1, keepdims=True)
    lo = N_GROUPS + grp * EXPERTS_PER_GROUP
    is_e = (lane >= lo) & (lane < lo + EXPERTS_PER_GROUP)
    le = jnp.where(is_e, logits, neg)
    ee = jnp.exp(le - jnp.max(le, axis=-1, keepdims=True))
    pe = ee / jnp.sum(ee, axis=-1, keepdims=True)
    pe = jnp.where(is_e, pe, -1.0)
    p1 = jnp.max(pe, axis=-1, keepdims=True)
    i1 = jnp.min(jnp.where(pe == p1, lane, big), axis=-1, keepdims=True)
    pe2 = jnp.where(lane == i1, -1.0, pe)
    p2 = jnp.max(pe2, axis=-1, keepdims=True)
    i2 = jnp.min(jnp.where(pe2 == p2, lane, big), axis=-1, keepdims=True)
    tot = p1 + p2
    w1 = gate_g * (p1 / tot)
    w2 = gate_g * (p2 / tot)
    e1 = i1 - N_GROUPS
    e2 = i2 - N_GROUPS
    tm = logits.shape[0]
    oh1 = lane == e1
    oh2 = lane == e2
    cnt = jnp.where(oh1 | oh2, 1.0, 0.0)
    ri = lax.broadcasted_iota(jnp.int32, (tm, tm), 0)
    ci = lax.broadcasted_iota(jnp.int32, (tm, tm), 1)
    before = jnp.where(ci < ri, 1.0, 0.0).astype(BF16)
    pos = _dot(before, cnt.astype(BF16)) + cnt_ref[...]
    r1 = jnp.sum(jnp.where(oh1, pos, 0.0), axis=-1, keepdims=True)
    r2 = jnp.sum(jnp.where(oh2, pos, 0.0), axis=-1, keepdims=True)
    cnt_ref[...] = cnt_ref[...] + jnp.sum(cnt, axis=0, keepdims=True)
    vals = (e1.astype(F32), e2.astype(F32), w1, w2, r1, r2)
    out = jnp.zeros(logits.shape, F32)
    for j, val in enumerate(vals):
        out = jnp.where(lane == j, val, out)
    return out


def _init_counts(cnt_ref):
    @pl.when((pl.program_id(0) == 0) & (pl.program_id(1) == 0))
    def _():
        cnt_ref[...] = jnp.zeros(cnt_ref.shape, F32)


def _post_mixer(x, y, mod, nf, wr_hi, wr_lo, br, cnt_ref):
    g1 = mod[2:3]
    sh2 = mod[3:4]
    sc2 = mod[4:5]
    x1 = x + g1 * y
    hn2 = (_rms(x1) * nf) * (1.0 + sc2) + sh2
    return x1, hn2, _route(hn2, wr_hi, wr_lo, br, cnt_ref)


def _gdn_in_kernel(x_ref, mod_ref, ng_ref, wqkv_ref, wz_ref, wab_ref, cw_ref, hist0_ref, alog_ref, dtb_ref,
                   q_ref, k_ref, v_ref, z_ref, gbc_ref, gbr_ref, hist_ref, xp_ref, *, tm, chunk):
    l = pl.program_id(1)

    @pl.when(l == 0)
    def _():
        xp_ref[0:8, :] = jnp.zeros((8, GDN_CONV_CH), F32)
        xp_ref[5:8, :] = hist0_ref[0]

    x = x_ref[0]
    mod = mod_ref[0]
    hn = (_rms(x) * ng_ref[...]) * (1.0 + mod[1:2]) + mod[0:1]
    hb = hn.astype(BF16)
    qkv = _dot(hb, wqkv_ref[...])
    xp_ref[8:8 + tm, :] = qkv
    cw = cw_ref[...]
    y = cw[3:4] * qkv
    for j in range(CONV_W - 1):
        y = y + cw[j:j + 1] * xp_ref[5 + j:5 + j + tm, :]
    new_hist = xp_ref[tm + 5:tm + 8, :]
    xp_ref[5:8, :] = new_hist
    hist_ref[0] = new_hist
    y = _silu(y)
    for h in range(GDN_HEADS):
        s = slice(h * GDN_DK, (h + 1) * GDN_DK)
        qh = y[:, s]
        q_ref[0, :, s] = qh * lax.rsqrt(jnp.sum(qh * qh, axis=-1, keepdims=True) + 1e-6) * (GDN_DK ** -0.5)
        kh = y[:, GDN_QK_W + h * GDN_DK:GDN_QK_W + (h + 1) * GDN_DK]
        k_ref[0, :, s] = kh * lax.rsqrt(jnp.sum(kh * kh, axis=-1, keepdims=True) + 1e-6)
    v_ref[0] = y[:, 2 * GDN_QK_W:]
    z_ref[0] = _dot(hb, wz_ref[...])
    ab = _dot(hb, wab_ref[...])
    g = -jnp.exp(alog_ref[...]) * _softplus(ab + dtb_ref[...])
    beta = _sigmoid(ab)
    ri = lax.broadcasted_iota(jnp.int32, (tm, tm), 0)
    ci = lax.broadcasted_iota(jnp.int32, (tm, tm), 1)
    tri = jnp.where((ri // chunk == ci // chunk) & (ci <= ri), 1.0, 0.0).astype(BF16)
    g1, g2, g3 = _split3(g)
    gcum = (_dot(tri, g1) + _dot(tri, g2)) + _dot(tri, g3)
    lane = lax.broadcasted_iota(jnp.int32, (tm, LANES), 1)
    gb = jnp.where(lane < GDN_HEADS, gcum, beta)
    gbc_ref[0] = gb
    er = lax.broadcasted_iota(jnp.int32, (2 * GDN_HEADS, LANES), 0)
    ec = lax.broadcasted_iota(jnp.int32, (2 * GDN_HEADS, LANES), 1)
    sel = jnp.where(er == ec, 1.0, 0.0).astype(BF16)
    b1, b2, b3 = _split3(gb)
    for n in range(tm // chunk):
        r = slice(n * chunk, (n + 1) * chunk)
        gbr_ref[0, n] = (_dot_nt(sel, b1[r]) + _dot_nt(sel, b2[r])) + _dot_nt(sel, b3[r])


def _gdn_in(x, mod, ng, wqkv, wz, wab, cw, hist0, alog, dtb, tm, chunk):
    b, l, d = x.shape
    grid = (b, l // tm)
    row = lambda shape: pl.BlockSpec(shape, lambda i, j: (i, j, 0))
    outs = pl.pallas_call(
        functools.partial(_gdn_in_kernel, tm=tm, chunk=chunk),
        grid=grid,
        in_specs=[
            row((1, tm, d)),
            pl.BlockSpec((1, 6, d), lambda i, j: (i, 0, 0)),
            _const_spec((1, d)),
            _const_spec(wqkv.shape),
            _const_spec(wz.shape),
            _const_spec(wab.shape),
            _const_spec(cw.shape),
            pl.BlockSpec((1, CONV_W - 1, GDN_CONV_CH), lambda i, j: (i, 0, 0)),
            _const_spec((1, LANES)),
            _const_spec((1, LANES)),
        ],
        out_specs=[
            row((1, tm, GDN_QK_W)), row((1, tm, GDN_QK_W)), row((1, tm, GDN_V_W)), row((1, tm, GDN_V_W)),
            row((1, tm, LANES)),
            pl.BlockSpec((1, tm // chunk, 2 * GDN_HEADS, chunk), lambda i, j: (i, j, 0, 0)),
            pl.BlockSpec((1, CONV_W - 1, GDN_CONV_CH), lambda i, j: (i, 0, 0)),
        ],
        out_shape=[
            jax.ShapeDtypeStruct((b, l, GDN_QK_W), F32), jax.ShapeDtypeStruct((b, l, GDN_QK_W), F32),
            jax.ShapeDtypeStruct((b, l, GDN_V_W), F32), jax.ShapeDtypeStruct((b, l, GDN_V_W), F32),
            jax.ShapeDtypeStruct((b, l, LANES), F32),
            jax.ShapeDtypeStruct((b, l // chunk, 2 * GDN_HEADS, chunk), F32),
            jax.ShapeDtypeStruct((b, CONV_W - 1, GDN_CONV_CH), F32),
        ],
        scratch_shapes=[pltpu.VMEM((tm + 8, GDN_CONV_CH), F32)],
        compiler_params=_params("arbitrary", "arbitrary"),
        name="gdn_in",
    )(x, mod, ng, wqkv, wz, wab, cw, hist0, alog, dtb)
    return outs


def _gdn_chunk_kernel(q_ref, k_ref, v_ref, gbc_ref, gbr_ref, s0_ref, o_ref, sout_ref, s_ref, *, tm, chunk):
    l = pl.program_id(1)

    @pl.when(l == 0)
    def _():
        s_ref[...] = s0_ref[0]

    c2 = 2 * chunk
    n_pairs = GDN_HEADS // 2
    ri = lax.broadcasted_iota(jnp.int32, (chunk, c2), 0)
    cl = lax.broadcasted_iota(jnp.int32, (chunk, c2), 1)
    ci = cl % chunk
    left = cl < chunk
    incl = ri >= ci
    strict = ri > ci
    eye = jnp.where(ri == ci, 1.0, 0.0).astype(F32)
    base = min(GDN_INV_BLOCK, chunk)
    levels = int(math.log2(base))
    diag_blk = (ri // base) == (ci // base)
    merge_masks = []
    blk = base
    while blk < chunk:
        merge_masks.append(((ri // (2 * blk)) == (ci // (2 * blk))) & ((ri // blk) % 2 == 1) & ((ci // blk) % 2 == 0))
        blk *= 2
    heads = range(GDN_HEADS)
    pairs = range(n_pairs)

    def bdiag(x):
        return jnp.concatenate([jnp.where(left, x, 0.0), jnp.where(left, 0.0, x)], axis=0).astype(BF16)

    def bdiag2(x0, x1):
        z = jnp.zeros_like(x0)
        return jnp.concatenate([jnp.concatenate([x0, z], axis=1), jnp.concatenate([z, x1], axis=1)], axis=0).astype(BF16)

    def one_chunk(n, carry):
        r0 = pl.multiple_of(n * chunk, chunk)
        rows = pl.ds(r0, chunk)
        gbc = gbc_ref[0, rows, :]
        gbr = gbr_ref[0, n]
        hs = [slice(h * GDN_DK, (h + 1) * GDN_DK) for h in heads]
        q = [q_ref[0, rows, hs[h]] for h in heads]
        k = [k_ref[0, rows, hs[h]] for h in heads]
        v = [v_ref[0, rows, hs[h]] for h in heads]
        gc = [gbc[:, h:h + 1] for h in heads]
        bc = [gbc[:, GDN_HEADS + h:GDN_HEADS + h + 1] for h in heads]
        kb = [k[h] * bc[h] for h in heads]
        a2 = [_dot_nt(jnp.concatenate([q[h], kb[h]], axis=0).astype(BF16), k[h].astype(BF16)) for h in heads]
        attn, m = [], []
        for hp in pairs:
            h0, h1 = 2 * hp, 2 * hp + 1
            gcp = jnp.where(left, gc[h0], gc[h1])
            grp = jnp.concatenate([gbr[h0:h0 + 1, :], gbr[h1:h1 + 1, :]], axis=1)
            decay = jnp.where(incl, jnp.exp(jnp.where(incl, gcp - grp, 0.0)), 0.0)
            ap = jnp.concatenate([a2[h0], a2[h1]], axis=1)
            attn.append(ap[:chunk] * decay)
            m.append(jnp.where(strict, ap[chunk:] * decay, 0.0))
        nq = [jnp.where(diag_blk, -m[hp], 0.0) for hp in pairs]
        p = [eye + nq[hp] for hp in pairs]
        nq = [_dot(nq[hp].astype(BF16), bdiag(nq[hp])) for hp in pairs]
        for j in range(1, levels):
            if j < levels - 1:
                r = [_dot(nq[hp].astype(BF16), jnp.concatenate([bdiag(nq[hp]), bdiag(p[hp])], axis=1)) for hp in pairs]
                nq = [r[hp][:, :c2] for hp in pairs]
                p = [p[hp] + r[hp][:, c2:] for hp in pairs]
            else:
                p = [p[hp] + _dot(nq[hp].astype(BF16), bdiag(p[hp])) for hp in pairs]
        for low in merge_masks:
            t = [_dot(jnp.where(low, m[hp], 0.0).astype(BF16), bdiag(p[hp])) for hp in pairs]
            p = [p[hp] - _dot(p[hp].astype(BF16), bdiag(t[hp])) for hp in pairs]
        eg = [jnp.exp(gc[h]) for h in heads]
        rhs = [jnp.concatenate([kb[h] * eg[h], v[h] * bc[h]], axis=1) for h in heads]
        wu = [_dot(p[hp].astype(BF16), bdiag2(rhs[2 * hp], rhs[2 * hp + 1])) for hp in pairs]
        wd = GDN_DK + GDN_DV
        w = [wu[h // 2][:, (h % 2) * wd:(h % 2) * wd + GDN_DK] for h in heads]
        u = [wu[h // 2][:, (h % 2) * wd + GDN_DK:(h % 2 + 1) * wd] for h in heads]
        st = [s_ref[h] for h in heads]
        ws = [_dot(jnp.concatenate([w[h], q[h] * eg[h]], axis=0).astype(BF16), st[h].astype(BF16)) for h in heads]
        v_new = [u[h] - ws[h][:chunk] for h in heads]
        av = [_dot(attn[hp].astype(BF16), bdiag2(v_new[2 * hp], v_new[2 * hp + 1])) for hp in pairs]
        for h in heads:
            o_ref[0, rows, hs[h]] = ws[h][chunk:] + av[h // 2][:, (h % 2) * GDN_DV:(h % 2 + 1) * GDN_DV]
        for h in heads:
            g_last = gc[h][chunk - 1:chunk, :]
            kd = k[h] * jnp.exp(g_last - gc[h])
            s_ref[h] = st[h] * jnp.exp(g_last) + _dot_tn(kd.astype(BF16), v_new[h].astype(BF16))
        return carry

    lax.fori_loop(0, tm // chunk, one_chunk, 0)
    sout_ref[0] = s_ref[...]


def _gdn_chunks(q, k, v, gbc, gbr, s0, tm, chunk):
    b, l, _ = q.shape
    row = lambda shape: pl.BlockSpec(shape, lambda i, j: (i, j, 0))
    sspec = pl.BlockSpec((1, GDN_HEADS, GDN_DK, GDN_DV), lambda i, j: (i, 0, 0, 0))
    return pl.pallas_call(
        functools.partial(_gdn_chunk_kernel, tm=tm, chunk=chunk),
        grid=(b, l // tm),
        in_specs=[
            row((1, tm, GDN_QK_W)), row((1, tm, GDN_QK_W)), row((1, tm, GDN_V_W)), row((1, tm, LANES)),
            pl.BlockSpec((1, tm // chunk, 2 * GDN_HEADS, chunk), lambda i, j: (i, j, 0, 0)),
            sspec,
        ],
        out_specs=[row((1, tm, GDN_V_W)), sspec],
        out_shape=[jax.ShapeDtypeStruct((b, l, GDN_V_W), F32),
                   jax.ShapeDtypeStruct((b, GDN_HEADS, GDN_DK, GDN_DV), F32)],
        scratch_shapes=[pltpu.VMEM((GDN_HEADS, GDN_DK, GDN_DV), F32)],
        compiler_params=_params("arbitrary", "arbitrary"),
        name="gdn_chunks",
    )(q, k, v, gbc, gbr, s0)


def _gdn_out_kernel(o_ref, z_ref, x_ref, mod_ref, gn_ref, wout_ref, nf_ref, wrh_ref, wrl_ref, br_ref,
                    x1_ref, hn2_ref, route_ref, cnt_ref):
    _init_counts(cnt_ref)
    o = o_ref[0]
    z = z_ref[0]
    gn = gn_ref[...]
    parts = []
    for h in range(GDN_HEADS):
        s = slice(h * GDN_DV, (h + 1) * GDN_DV)
        parts.append((_rms(o[:, s]) * gn) * _silu(z[:, s]))
    on = jnp.concatenate(parts, axis=1).astype(BF16)
    y = _dot(on, wout_ref[...])
    x1, hn2, route = _post_mixer(x_ref[0], y, mod_ref[0], nf_ref[...], wrh_ref[...], wrl_ref[...], br_ref[...],
                                 cnt_ref)
    x1_ref[0] = x1
    hn2_ref[0] = hn2
    route_ref[0] = route


def _gdn_out(o, z, x, mod, gn, wout, nf, wrh, wrl, br, tm):
    b, l, d = x.shape
    row = lambda shape: pl.BlockSpec(shape, lambda i, j: (i, j, 0))
    return pl.pallas_call(
        _gdn_out_kernel,
        grid=(b, l // tm),
        in_specs=[
            row((1, tm, GDN_V_W)), row((1, tm, GDN_V_W)), row((1, tm, d)),
            pl.BlockSpec((1, 6, d), lambda i, j: (i, 0, 0)),
            _const_spec((1, GDN_DV)), _const_spec(wout.shape), _const_spec((1, d)),
            _const_spec(wrh.shape), _const_spec(wrl.shape), _const_spec((1, LANES)),
        ],
        out_specs=[row((1, tm, d)), row((1, tm, d)), row((1, tm, LANES)), _const_spec((1, LANES))],
        out_shape=[jax.ShapeDtypeStruct((b, l, d), F32), jax.ShapeDtypeStruct((b, l, d), F32),
                   jax.ShapeDtypeStruct((b, l, LANES), F32), jax.ShapeDtypeStruct((1, LANES), F32)],
        compiler_params=_params("arbitrary", "arbitrary"),
        name="gdn_out",
    )(o, z, x, mod, gn, wout, nf, wrh, wrl, br)


def _gelu_tanh(x):
    return 0.5 * x * (1.0 + jnp.tanh(math.sqrt(2.0 / math.pi) * (x + 0.044715 * (x * x * x))))


def _lru_kernel(x_ref, mod_ref, ng_ref, win_ref, cw_ref, cb_ref, wax_ref, ba_ref, bx_ref, lam_ref, wout_ref,
                hist0_ref, h0_ref, nf_ref, wrh_ref, wrl_ref, br_ref,
                x1_ref, hn2_ref, route_ref, hist_ref, hlast_ref, cnt_ref, xp_ref, h_ref, *, tm):
    l = pl.program_id(1)
    _init_counts(cnt_ref)

    @pl.when(l == 0)
    def _():
        xp_ref[0:8, :] = jnp.zeros((8, LRU_WIDTH), F32)
        xp_ref[5:8, :] = hist0_ref[0]
        h_ref[...] = h0_ref[0]

    x = x_ref[0]
    mod = mod_ref[0]
    hn = (_rms(x) * ng_ref[...]) * (1.0 + mod[1:2]) + mod[0:1]
    proj = _dot(hn.astype(BF16), win_ref[...])
    gate_br = _gelu_tanh(proj[:, :LRU_WIDTH])
    xb = proj[:, LRU_WIDTH:]
    xp_ref[8:8 + tm, :] = xb
    cw = cw_ref[...]
    xc = cw[3:4] * xb
    for j in range(CONV_W - 1):
        xc = xc + cw[j:j + 1] * xp_ref[5 + j:5 + j + tm, :]
    xc = xc + cb_ref[...]
    new_hist = xp_ref[tm + 5:tm + 8, :]
    xp_ref[5:8, :] = new_hist
    hist_ref[0] = new_hist
    xcb = xc.astype(BF16)
    ra, ia = [], []
    for h in range(LRU_BLOCKS):
        s = slice(h * LRU_BLOCK_W, (h + 1) * LRU_BLOCK_W)
        r2 = _dot(xcb[:, s], wax_ref[h])
        ra.append(r2[:, :LRU_BLOCK_W])
        ia.append(r2[:, LRU_BLOCK_W:])
    r = _sigmoid(jnp.concatenate(ra, axis=1) + ba_ref[...])
    i = _sigmoid(jnp.concatenate(ia, axis=1) + bx_ref[...])
    log_a = (-LRU_C * r) * _softplus(-lam_ref[...])
    a = jnp.exp(log_a)
    mult = jnp.sqrt(-_expm1(2.0 * log_a))
    b = mult * (i * xc)
    rowi = lax.broadcasted_iota(jnp.int32, (tm, LRU_WIDTH), 0)
    b = b + jnp.where(rowi == 0, a * h_ref[...], 0.0)
    sft = 1
    while sft < tm:
        keep = rowi >= sft
        a_prev = jnp.where(keep, pltpu.roll(a, sft, 0), 1.0)
        b_prev = jnp.where(keep, pltpu.roll(b, sft, 0), 0.0)
        b = a * b_prev + b
        a = a * a_prev
        sft *= 2
    hs = b
    h_last = hs[tm - 1:tm, :]
    h_ref[...] = h_last
    hlast_ref[0] = h_last
    y = _dot((hs * gate_br).astype(BF16), wout_ref[...])
    x1, hn2, route = _post_mixer(x, y, mod, nf_ref[...], wrh_ref[...], wrl_ref[...], br_ref[...], cnt_ref)
    x1_ref[0] = x1
    hn2_ref[0] = hn2
    route_ref[0] = route


def _lru_layer(x, mod, ng, win, cw, cb, wax, ba, bx, lam, wout, hist0, h0, nf, wrh, wrl, br, tm):
    b, l, d = x.shape
    row = lambda shape: pl.BlockSpec(shape, lambda i, j: (i, j, 0))
    per_b = lambda shape: pl.BlockSpec(shape, lambda i, j: (i, 0, 0))
    vec = _const_spec((1, d))
    return pl.pallas_call(
        functools.partial(_lru_kernel, tm=tm),
        grid=(b, l // tm),
        in_specs=[
            row((1, tm, d)), per_b((1, 6, d)), vec, _const_spec(win.shape), _const_spec(cw.shape), vec,
            _const_spec(wax.shape), vec, vec, vec, _const_spec(wout.shape),
            per_b((1, CONV_W - 1, LRU_WIDTH)), per_b((1, 1, LRU_WIDTH)), vec,
            _const_spec(wrh.shape), _const_spec(wrl.shape), _const_spec((1, LANES)),
        ],
        out_specs=[row((1, tm, d)), row((1, tm, d)), row((1, tm, LANES)),
                   per_b((1, CONV_W - 1, LRU_WIDTH)), per_b((1, 1, LRU_WIDTH)), _const_spec((1, LANES))],
        out_shape=[jax.ShapeDtypeStruct((b, l, d), F32), jax.ShapeDtypeStruct((b, l, d), F32),
                   jax.ShapeDtypeStruct((b, l, LANES), F32),
                   jax.ShapeDtypeStruct((b, CONV_W - 1, LRU_WIDTH), F32),
                   jax.ShapeDtypeStruct((b, 1, LRU_WIDTH), F32), jax.ShapeDtypeStruct((1, LANES), F32)],
        scratch_shapes=[pltpu.VMEM((tm + 8, LRU_WIDTH), F32), pltpu.VMEM((1, LRU_WIDTH), F32)],
        compiler_params=_params("arbitrary", "arbitrary"),
        name="lru_layer",
    )(x, mod, ng, win, cw, cb, wax, ba, bx, lam, wout, hist0, h0, nf, wrh, wrl, br)


def _moe_kernel(be_ref, xs_ref, w1_ref, w3_ref, w2_ref, y_ref, w1b_ref, w3b_ref, w2b_ref):
    i = pl.program_id(0)
    prev = be_ref[jnp.maximum(i - 1, 0)]

    @pl.when((i == 0) | (be_ref[i] != prev))
    def _():
        w1b_ref[...] = w1_ref[0].astype(BF16)
        w3b_ref[...] = w3_ref[0].astype(BF16)
        w2b_ref[...] = w2_ref[0].astype(BF16)

    xb = xs_ref[...].astype(BF16)
    hid = _silu(_dot(xb, w1b_ref[...])) * _dot(xb, w3b_ref[...])
    y_ref[...] = _dot(hid.astype(BF16), w2b_ref[...])


def _moe_blocks(blk_expert, xs, w1, w3, w2):
    p, d = xs.shape
    nb = p // MOE_BLOCK
    grid_spec = pltpu.PrefetchScalarGridSpec(
        num_scalar_prefetch=1,
        grid=(nb,),
        in_specs=[
            pl.BlockSpec((MOE_BLOCK, d), lambda i, be: (i, 0)),
            pl.BlockSpec((1, d, D_FF_EXPERT), lambda i, be: (be[i], 0, 0)),
            pl.BlockSpec((1, d, D_FF_EXPERT), lambda i, be: (be[i], 0, 0)),
            pl.BlockSpec((1, D_FF_EXPERT, d), lambda i, be: (be[i], 0, 0)),
        ],
        out_specs=pl.BlockSpec((MOE_BLOCK, d), lambda i, be: (i, 0)),
        scratch_shapes=[pltpu.VMEM((d, D_FF_EXPERT), BF16), pltpu.VMEM((d, D_FF_EXPERT), BF16),
                        pltpu.VMEM((D_FF_EXPERT, d), BF16)],
    )
    return pl.pallas_call(
        _moe_kernel,
        grid_spec=grid_spec,
        out_shape=jax.ShapeDtypeStruct((p, d), F32),
        compiler_params=_params("arbitrary"),
        name="moe_blocks",
    )(blk_expert, xs, w1, w3, w2)


def _combine_kernel(x1_ref, ya_ref, yb_ref, route_ref, mod_ref, no_ref, o_ref, *, final_norm):
    route = route_ref[0]
    moe = ya_ref[0] * route[:, 2:3] + yb_ref[0] * route[:, 3:4]
    x2 = x1_ref[0] + mod_ref[0][5:6] * moe
    if final_norm:
        x2 = _rms(x2) * no_ref[...]
    o_ref[0] = x2


def _combine(x1, ya, yb, route, mod, norm_out, tm, final_norm):
    b, l, d = x1.shape
    row = lambda shape: pl.BlockSpec(shape, lambda i, j: (i, j, 0))
    return pl.pallas_call(
        functools.partial(_combine_kernel, final_norm=final_norm),
        grid=(b, l // tm),
        in_specs=[row((1, tm, d)), row((1, tm, d)), row((1, tm, d)), row((1, tm, LANES)),
                  pl.BlockSpec((1, 6, d), lambda i, j: (i, 0, 0)), _const_spec((1, d))],
        out_specs=row((1, tm, d)),
        out_shape=jax.ShapeDtypeStruct((b, l, d), F32),
        compiler_params=_params("arbitrary", "arbitrary"),
        name="combine",
    )(x1, ya, yb, route, mod, norm_out)


def _moe(x1, hn2, route, counts, mod, w1, w3, w2, norm_out, tm, final_norm):
    b, l, d = x1.shape
    n = b * l
    a = n * EXPERT_TOPK
    rt = route.reshape(n, LANES)
    e_ab = rt[:, 0:EXPERT_TOPK].astype(jnp.int32)
    r_ab = rt[:, 4:4 + EXPERT_TOPK].astype(jnp.int32)
    counts = counts[0, :N_EXPERTS].astype(jnp.int32)
    padded = (counts + MOE_BLOCK - 1) // MOE_BLOCK * MOE_BLOCK
    pad_end = jnp.cumsum(padded)
    pad_start = pad_end - padded
    dest = pad_start[e_ab] + r_ab
    n_blocks = -(-a // MOE_BLOCK) + N_EXPERTS
    p = n_blocks * MOE_BLOCK
    tok = jnp.broadcast_to(jnp.arange(n, dtype=jnp.int32)[:, None], (n, EXPERT_TOPK))
    slot_tok = jnp.zeros((p,), jnp.int32).at[dest.reshape(-1)].set(tok.reshape(-1))
    blk_expert = jnp.minimum(
        jnp.searchsorted(pad_end, jnp.arange(n_blocks, dtype=jnp.int32) * MOE_BLOCK, side='right'),
        N_EXPERTS - 1).astype(jnp.int32)
    xs = hn2.reshape(n, d)[slot_tok]
    ys = _moe_blocks(blk_expert, xs, w1, w3, w2)
    ya = ys[dest[:, 0]].reshape(b, l, d)
    yb = ys[dest[:, 1]].reshape(b, l, d)
    return _combine(x1, ya, yb, route, mod, norm_out, tm, final_norm)


def _trunk(x, mods, gdn_s, gdn_conv, lru_h, lru_conv, wp):
    b, l, d = x.shape
    tm = min(ROW_TILE, l)
    chunk = min(GDN_CHUNK, l)
    assert l % tm == 0 and tm % chunk == 0 and chunk & (chunk - 1) == 0
    mod = mods[0]
    q, k, v, z, gbc, gbr, gconv_new = _gdn_in(x, mod, wp['norm_mix0'], wp['gdn_wqkv'], wp['gdn_wz'], wp['gdn_wab'],
                                              wp['gdn_conv_w'], gdn_conv, wp['gdn_alog'], wp['gdn_dtb'], tm, chunk)
    o, s_new = _gdn_chunks(q, k, v, gbc, gbr, gdn_s, tm, chunk)
    x1, hn2, route, counts = _gdn_out(o, z, x, mod, wp['gdn_norm'], wp['gdn_wout'], wp['norm_ffn0'],
                                      wp['wr_hi0'], wp['wr_lo0'], wp['br0'], tm)
    x = _moe(x1, hn2, route, counts, mod, wp['moe_w1'][0], wp['moe_w3'][0], wp['moe_w2'][0], wp['norm_out'],
             tm, False)
    mod = mods[1]
    x1, hn2, route, lconv_new, h_new, counts = _lru_layer(
        x, mod, wp['norm_mix1'], wp['lru_win'], wp['lru_conv_w'], wp['lru_conv_b'], wp['lru_wax'], wp['lru_ba'],
        wp['lru_bx'], wp['lru_lam'], wp['lru_wout'], lru_conv, lru_h, wp['norm_ffn1'],
        wp['wr_hi1'], wp['wr_lo1'], wp['br1'], tm)
    y = _moe(x1, hn2, route, counts, mod, wp['moe_w1'][1], wp['moe_w3'][1], wp['moe_w2'][1], wp['norm_out'],
             tm, True)
    return y, s_new[None], gconv_new[None], h_new.reshape(1, b, LRU_WIDTH), lconv_new[None]


def _pad_lanes(v, width=LANES):
    v = v.reshape(1, -1)
    return jnp.pad(v, ((0, 0), (0, width - v.shape[1])))


def _router_weights(w_rg, b_rg, w_re, b_re):
    w = jnp.pad(jnp.concatenate([w_rg, w_re], axis=1), ((0, 0), (0, LANES - N_GROUPS - N_EXPERTS)))
    hi = w.astype(BF16)
    lo = (w - hi.astype(F32)).astype(BF16)
    return hi, lo, _pad_lanes(jnp.concatenate([b_rg, b_re]))


def kernel(x_prompt, x_sample, state_gdn_S, state_gdn_conv, state_lru_h, state_lru_conv, c_prompt, c_sample, w_ada, b_ada, norm_mix, norm_ffn, norm_out, gdn_w_in, gdn_conv_w, gdn_a_log, gdn_dt_bias, gdn_norm, gdn_w_out, lru_w_in, lru_conv_w, lru_conv_b, lru_w_a, lru_b_a, lru_w_x, lru_b_x, lru_lambda, lru_w_out, moe_w_rg, moe_b_rg, moe_w_re, moe_b_re, moe_w1, moe_w3, moe_w2):
    d = D_MODEL
    bp = x_prompt.shape[0]
    bs = x_sample.shape[0]
    win = gdn_w_in[0]
    wab = jnp.pad(win[:, GDN_CONV_CH + GDN_V_W:], ((0, 0), (0, LANES - 2 * GDN_HEADS)))
    wp = dict(
        norm_mix0=norm_mix[0].reshape(1, d), norm_mix1=norm_mix[1].reshape(1, d),
        norm_ffn0=norm_ffn[0].reshape(1, d), norm_ffn1=norm_ffn[1].reshape(1, d),
        norm_out=norm_out.reshape(1, d),
        gdn_wqkv=win[:, :GDN_CONV_CH].astype(BF16),
        gdn_wz=win[:, GDN_CONV_CH:GDN_CONV_CH + GDN_V_W].astype(BF16),
        gdn_wab=wab.astype(BF16),
        gdn_conv_w=gdn_conv_w[0],
        gdn_alog=_pad_lanes(gdn_a_log[0]), gdn_dtb=_pad_lanes(gdn_dt_bias[0]),
        gdn_norm=gdn_norm[0].reshape(1, GDN_DV), gdn_wout=gdn_w_out[0].astype(BF16),
        lru_win=lru_w_in[0].astype(BF16), lru_conv_w=lru_conv_w[0], lru_conv_b=lru_conv_b[0].reshape(1, d),
        lru_wax=jnp.concatenate([lru_w_a[0], lru_w_x[0]], axis=-1).astype(BF16),
        lru_ba=lru_b_a[0].reshape(1, d), lru_bx=lru_b_x[0].reshape(1, d), lru_lam=lru_lambda[0].reshape(1, d),
        lru_wout=lru_w_out[0].astype(BF16),
        moe_w1=moe_w1, moe_w3=moe_w3, moe_w2=moe_w2,
    )
    for i in range(DEPTH):
        wp[f'wr_hi{i}'], wp[f'wr_lo{i}'], wp[f'br{i}'] = _router_weights(moe_w_rg[i], moe_b_rg[i], moe_w_re[i], moe_b_re[i])

    mods = _ada_mod(jnp.concatenate([c_prompt, c_sample], axis=0), w_ada, b_ada)
    mods = mods.reshape(DEPTH, bp + bs, 6, d)
    mods_p = [mods[i, :bp] for i in range(DEPTH)]
    mods_s = [mods[i, bp:] for i in range(DEPTH)]

    dt = x_prompt.dtype
    z_s = jnp.zeros((bp, GDN_HEADS, GDN_DK, GDN_DV), dt)
    z_gc = jnp.zeros((bp, CONV_W - 1, GDN_CONV_CH), dt)
    z_h = jnp.zeros((bp, 1, LRU_WIDTH), dt)
    z_lc = jnp.zeros((bp, CONV_W - 1, LRU_WIDTH), dt)
    y_p, gs_p, gc_p, lh_p, lc_p = _trunk(x_prompt, mods_p, z_s, z_gc, z_h, z_lc, wp)
    y_s, gs_s, gc_s, lh_s, lc_s = _trunk(x_sample, mods_s, state_gdn_S[0], state_gdn_conv[0],
                                         state_lru_h[0].reshape(bs, 1, LRU_WIDTH), state_lru_conv[0], wp)
    return (y_p, y_s, gs_p, gc_p, lh_p, lc_p, gs_s, gc_s, lh_s, lc_s)
```

```python
import functools
import math

import jax
import jax.numpy as jnp
from jax import lax
from jax.experimental import pallas as pl
from jax.experimental.pallas import tpu as pltpu

F32 = jnp.float32
BF16 = jnp.bfloat16

D_MODEL = 1024
DEPTH = 2
CONV_W = 4
NORM_EPS = 1e-6
GDN_HEADS = 8
GDN_DK = 128
GDN_DV = 128
GDN_QK_W = GDN_HEADS * GDN_DK
GDN_V_W = GDN_HEADS * GDN_DV
GDN_CONV_CH = 2 * GDN_QK_W + GDN_V_W
GDN_CHUNK = 64
GDN_INV_BLOCK = 16
LRU_WIDTH = D_MODEL
LRU_BLOCKS = 8
LRU_BLOCK_W = LRU_WIDTH // LRU_BLOCKS
LRU_C = 8.0
N_GROUPS = 4
EXPERTS_PER_GROUP = 8
N_EXPERTS = N_GROUPS * EXPERTS_PER_GROUP
EXPERT_TOPK = 2
D_FF_EXPERT = 512
MOE_BLOCK = 256
LANES = 128
SUBLANES = 8
ROW_TILE = 256
VMEM_LIMIT = 56 * 1024 * 1024


def _dot(a, b):
    return jnp.dot(a, b, preferred_element_type=F32)


def _dot_nt(a, b):
    return lax.dot_general(a, b, (((1,), (1,)), ((), ())), preferred_element_type=F32)


def _dot_tn(a, b):
    return lax.dot_general(a, b, (((0,), (0,)), ((), ())), preferred_element_type=F32)


def _split3(x):
    a = x.astype(BF16)
    r = x - a.astype(F32)
    b = r.astype(BF16)
    c = (r - b.astype(F32)).astype(BF16)
    return a, b, c


def _rms(x):
    return x * lax.rsqrt(jnp.mean(x * x, axis=-1, keepdims=True) + NORM_EPS)


def _sigmoid(x):
    return 1.0 / (1.0 + jnp.exp(-x))


def _silu(x):
    return x * _sigmoid(x)


def _softplus(x):
    return jnp.maximum(x, 0.0) + jnp.log1p(jnp.exp(-jnp.abs(x)))


def _expm1(x):
    u = jnp.exp(x)
    near = (u > 0.5) & (u < 2.0) & (u != 1.0)
    corrected = (u - 1.0) * x / jnp.where(near, jnp.log(u), 1.0)
    return jnp.where(u == 1.0, x, jnp.where(near, corrected, u - 1.0))


def _params(*sem):
    return pltpu.CompilerParams(dimension_semantics=sem, vmem_limit_bytes=VMEM_LIMIT)


def _const_spec(shape):
    nd = len(shape)
    return pl.BlockSpec(shape, lambda *_: (0,) * nd)


def _ada_kernel(c_ref, w_ref, b_ref, o_ref):
    s = _silu(c_ref[...]).astype(BF16)
    o_ref[0] = _dot(s, w_ref[0].astype(BF16)) + b_ref[0]


def _ada_mod(c_all, w_ada, b_ada):
    bt = c_all.shape[0]
    tn = 1536
    n6 = 6 * D_MODEL
    return pl.pallas_call(
        _ada_kernel,
        grid=(DEPTH, n6 // tn),
        in_specs=[
            pl.BlockSpec((bt, D_MODEL), lambda i, j: (0, 0)),
            pl.BlockSpec((1, D_MODEL, tn), lambda i, j: (i, 0, j)),
            pl.BlockSpec((1, 1, tn), lambda i, j: (i, 0, j)),
        ],
        out_specs=pl.BlockSpec((1, bt, tn), lambda i, j: (i, 0, j)),
        out_shape=jax.ShapeDtypeStruct((DEPTH, bt, n6), F32),
        compiler_params=_params("arbitrary", "arbitrary"),
        name="ada_mod",
    )(c_all, w_ada, b_ada.reshape(DEPTH, 1, n6))


def _route(hn2, wr_hi, wr_lo, br, cnt_ref):
    h_hi = hn2.astype(BF16)
    h_lo = (hn2 - h_hi.astype(F32)).astype(BF16)
    logits = _dot(h_hi, wr_hi) + (_dot(h_hi, wr_lo) + _dot(h_lo, wr_hi)) + br
    lane = lax.broadcasted_iota(jnp.int32, logits.shape, 1)
    neg = jnp.float32(-jnp.inf)
    big = jnp.int32(1 << 20)
    is_g = lane < N_GROUPS
    lg = jnp.where(is_g, logits, neg)
    eg = jnp.exp(lg - jnp.max(lg, axis=-1, keepdims=True))
    pg = eg / jnp.sum(eg, axis=-1, keepdims=True)
    pg = jnp.where(is_g, pg, -1.0)
    gate_g = jnp.max(pg, axis=-1, keepdims=True)
    grp = jnp.min(jnp.where(pg == gate_g, lane, big), axis=-1, keepdims=True)
    lo = N_GROUPS + grp * EXPERTS_PER_GROUP
    is_e = (lane >= lo) & (lane < lo + EXPERTS_PER_GROUP)
    le = jnp.where(is_e, logits, neg)
    ee = jnp.exp(le - jnp.max(le, axis=-1, keepdims=True))
    pe = ee / jnp.sum(ee, axis=-1, keepdims=True)
    pe = jnp.where(is_e, pe, -1.0)
    p1 = jnp.max(pe, axis=-1, keepdims=True)
    i1 = jnp.min(jnp.where(pe == p1, lane, big), axis=-1, keepdims=True)
    pe2 = jnp.where(lane == i1, -1.0, pe)
    p2 = jnp.max(pe2, axis=-1, keepdims=True)
    i2 = jnp.min(jnp.where(pe2 == p2, lane, big), axis=-1, keepdims=True)
    tot = p1 + p2
    w1 = gate_g * (p1 / tot)
    w2 = gate_g * (p2 / tot)
    e1 = i1 - N_GROUPS
    e2 = i2 - N_GROUPS
    tm = logits.shape[0]
    oh1 = lane == e1
    oh2 = lane == e2
    cnt = jnp.where(oh1 | oh2, 1.0, 0.0)
    ri = lax.broadcasted_iota(jnp.int32, (tm, tm), 0)
    ci = lax.broadcasted_iota(jnp.int32, (tm, tm), 1)
    before = jnp.where(ci < ri, 1.0, 0.0).astype(BF16)
    pos = _dot(before, cnt.astype(BF16)) + cnt_ref[...]
    r1 = jnp.sum(jnp.where(oh1, pos, 0.0), axis=-1, keepdims=True)
    r2 = jnp.sum(jnp.where(oh2, pos, 0.0), axis=-1, keepdims=True)
    cnt_ref[...] = cnt_ref[...] + jnp.sum(cnt, axis=0, keepdims=True)
    vals = (e1.astype(F32), e2.astype(F32), w1, w2, r1, r2)
    out = jnp.zeros(logits.shape, F32)
    for j, val in enumerate(vals):
        out = jnp.where(lane == j, val, out)
    return out


def _init_counts(cnt_ref):
    @pl.when((pl.program_id(0) == 0) & (pl.program_id(1) == 0))
    def _():
        cnt_ref[...] = jnp.zeros(cnt_ref.shape, F32)


def _post_mixer(x, y, mod, nf, wr_hi, wr_lo, br, cnt_ref):
    g1 = mod[2:3]
    sh2 = mod[3:4]
    sc2 = mod[4:5]
    x1 = x + g1 * y
    hn2 = (_rms(x1) * nf) * (1.0 + sc2) + sh2
    return x1, hn2, _route(hn2, wr_hi, wr_lo, br, cnt_ref)


def _gdn_in_kernel(x_ref, mod_ref, ng_ref, wqkv_ref, wz_ref, wab_ref, cw_ref, hist0_ref, alog_ref, dtb_ref,
                   q_ref, k_ref, v_ref, z_ref, gbc_ref, gbr_ref, hist_ref, xp_ref, *, tm, chunk):
    l = pl.program_id(1)

    @pl.when(l == 0)
    def _():
        xp_ref[0:8, :] = jnp.zeros((8, GDN_CONV_CH), F32)
        xp_ref[5:8, :] = hist0_ref[0]

    x = x_ref[0]
    mod = mod_ref[0]
    hn = (_rms(x) * ng_ref[...]) * (1.0 + mod[1:2]) + mod[0:1]
    hb = hn.astype(BF16)
    qkv = _dot(hb, wqkv_ref[...])
    xp_ref[8:8 + tm, :] = qkv
    cw = cw_ref[...]
    y = cw[3:4] * qkv
    for j in range(CONV_W - 1):
        y = y + cw[j:j + 1] * xp_ref[5 + j:5 + j + tm, :]
    new_hist = xp_ref[tm + 5:tm + 8, :]
    xp_ref[5:8, :] = new_hist
    hist_ref[0] = new_hist
    y = _silu(y)
    for h in range(GDN_HEADS):
        s = slice(h * GDN_DK, (h + 1) * GDN_DK)
        qh = y[:, s]
        q_ref[0, :, s] = qh * lax.rsqrt(jnp.sum(qh * qh, axis=-1, keepdims=True) + 1e-6) * (GDN_DK ** -0.5)
        kh = y[:, GDN_QK_W + h * GDN_DK:GDN_QK_W + (h + 1) * GDN_DK]
        k_ref[0, :, s] = kh * lax.rsqrt(jnp.sum(kh * kh, axis=-1, keepdims=True) + 1e-6)
    v_ref[0] = y[:, 2 * GDN_QK_W:]
    z_ref[0] = _dot(hb, wz_ref[...])
    ab = _dot(hb, wab_ref[...])
    g = -jnp.exp(alog_ref[...]) * _softplus(ab + dtb_ref[...])
    beta = _sigmoid(ab)
    ri = lax.broadcasted_iota(jnp.int32, (tm, tm), 0)
    ci = lax.broadcasted_iota(jnp.int32, (tm, tm), 1)
    tri = jnp.where((ri // chunk == ci // chunk) & (ci <= ri), 1.0, 0.0).astype(BF16)
    g1, g2, g3 = _split3(g)
    gcum = (_dot(tri, g1) + _dot(tri, g2)) + _dot(tri, g3)
    lane = lax.broadcasted_iota(jnp.int32, (tm, LANES), 1)
    gb = jnp.where(lane < GDN_HEADS, gcum, beta)
    gbc_ref[0] = gb
    er = lax.broadcasted_iota(jnp.int32, (2 * GDN_HEADS, LANES), 0)
    ec = lax.broadcasted_iota(jnp.int32, (2 * GDN_HEADS, LANES), 1)
    sel = jnp.where(er == ec, 1.0, 0.0).astype(BF16)
    b1, b2, b3 = _split3(gb)
    for n in range(tm // chunk):
        r = slice(n * chunk, (n + 1) * chunk)
        gbr_ref[0, n] = (_dot_nt(sel, b1[r]) + _dot_nt(sel, b2[r])) + _dot_nt(sel, b3[r])


def _gdn_in(x, mod, ng, wqkv, wz, wab, cw, hist0, alog, dtb, tm, chunk):
    b, l, d = x.shape
    grid = (b, l // tm)
    row = lambda shape: pl.BlockSpec(shape, lambda i, j: (i, j, 0))
    outs = pl.pallas_call(
        functools.partial(_gdn_in_kernel, tm=tm, chunk=chunk),
        grid=grid,
        in_specs=[
            row((1, tm, d)),
            pl.BlockSpec((1, 6, d), lambda i, j: (i, 0, 0)),
            _const_spec((1, d)),
            _const_spec(wqkv.shape),
            _const_spec(wz.shape),
            _const_spec(wab.shape),
            _const_spec(cw.shape),
            pl.BlockSpec((1, CONV_W - 1, GDN_CONV_CH), lambda i, j: (i, 0, 0)),
            _const_spec((1, LANES)),
            _const_spec((1, LANES)),
        ],
        out_specs=[
            row((1, tm, GDN_QK_W)), row((1, tm, GDN_QK_W)), row((1, tm, GDN_V_W)), row((1, tm, GDN_V_W)),
            row((1, tm, LANES)),
            pl.BlockSpec((1, tm // chunk, 2 * GDN_HEADS, chunk), lambda i, j: (i, j, 0, 0)),
            pl.BlockSpec((1, CONV_W - 1, GDN_CONV_CH), lambda i, j: (i, 0, 0)),
        ],
        out_shape=[
            jax.ShapeDtypeStruct((b, l, GDN_QK_W), F32), jax.ShapeDtypeStruct((b, l, GDN_QK_W), F32),
            jax.ShapeDtypeStruct((b, l, GDN_V_W), F32), jax.ShapeDtypeStruct((b, l, GDN_V_W), F32),
            jax.ShapeDtypeStruct((b, l, LANES), F32),
            jax.ShapeDtypeStruct((b, l // chunk, 2 * GDN_HEADS, chunk), F32),
            jax.ShapeDtypeStruct((b, CONV_W - 1, GDN_CONV_CH), F32),
        ],
        scratch_shapes=[pltpu.VMEM((tm + 8, GDN_CONV_CH), F32)],
        compiler_params=_params("arbitrary", "arbitrary"),
        name="gdn_in",
    )(x, mod, ng, wqkv, wz, wab, cw, hist0, alog, dtb)
    return outs


def _gdn_chunk_kernel(q_ref, k_ref, v_ref, gbc_ref, gbr_ref, s0_ref, o_ref, sout_ref, s_ref, *, tm, chunk):
    l = pl.program_id(1)

    @pl.when(l == 0)
    def _():
        s_ref[...] = s0_ref[0]

    c2 = 2 * chunk
    n_pairs = GDN_HEADS // 2
    ri = lax.broadcasted_iota(jnp.int32, (chunk, c2), 0)
    cl = lax.broadcasted_iota(jnp.int32, (chunk, c2), 1)
    ci = cl % chunk
    left = cl < chunk
    incl = ri >= ci
    strict = ri > ci
    eye = jnp.where(ri == ci, 1.0, 0.0).astype(F32)
    base = min(GDN_INV_BLOCK, chunk)
    levels = int(math.log2(base))
    diag_blk = (ri // base) == (ci // base)
    merge_masks = []
    blk = base
    while blk < chunk:
        merge_masks.append(((ri // (2 * blk)) == (ci // (2 * blk))) & ((ri // blk) % 2 == 1) & ((ci // blk) % 2 == 0))
        blk *= 2
    heads = range(GDN_HEADS)
    pairs = range(n_pairs)

    def bdiag(x):
        return jnp.concatenate([jnp.where(left, x, 0.0), jnp.where(left, 0.0, x)], axis=0).astype(BF16)

    def bdiag2(x0, x1):
        z = jnp.zeros_like(x0)
        return jnp.concatenate([jnp.concatenate([x0, z], axis=1), jnp.concatenate([z, x1], axis=1)], axis=0).astype(BF16)

    def one_chunk(n, carry):
        r0 = pl.multiple_of(n * chunk, chunk)
        rows = pl.ds(r0, chunk)
        gbc = gbc_ref[0, rows, :]
        gbr = gbr_ref[0, n]
        hs = [slice(h * GDN_DK, (h + 1) * GDN_DK) for h in heads]
        q = [q_ref[0, rows, hs[h]] for h in heads]
        k = [k_ref[0, rows, hs[h]] for h in heads]
        v = [v_ref[0, rows, hs[h]] for h in heads]
        gc = [gbc[:, h:h + 1] for h in heads]
        bc = [gbc[:, GDN_HEADS + h:GDN_HEADS + h + 1] for h in heads]
        kb = [k[h] * bc[h] for h in heads]
        a2 = [_dot_nt(jnp.concatenate([q[h], kb[h]], axis=0).astype(BF16), k[h].astype(BF16)) for h in heads]
        attn, m = [], []
        for hp in pairs:
            h0, h1 = 2 * hp, 2 * hp + 1
            gcp = jnp.where(left, gc[h0], gc[h1])
            grp = jnp.concatenate([gbr[h0:h0 + 1, :], gbr[h1:h1 + 1, :]], axis=1)
            decay = jnp.where(incl, jnp.exp(jnp.where(incl, gcp - grp, 0.0)), 0.0)
            ap = jnp.concatenate([a2[h0], a2[h1]], axis=1)
            attn.append(ap[:chunk] * decay)
            m.append(jnp.where(strict, ap[chunk:] * decay, 0.0))
        nq = [jnp.where(diag_blk, -m[hp], 0.0) for hp in pairs]
        p = [eye + nq[hp] for hp in pairs]
        nq = [_dot(nq[hp].astype(BF16), bdiag(nq[hp])) for hp in pairs]
        for j in range(1, levels):
            if j < levels - 1:
                r = [_dot(nq[hp].astype(BF16), jnp.concatenate([bdiag(nq[hp]), bdiag(p[hp])], axis=1)) for hp in pairs]
                nq = [r[hp][:, :c2] for hp in pairs]
                p = [p[hp] + r[hp][:, c2:] for hp in pairs]
            else:
                p = [p[hp] + _dot(nq[hp].astype(BF16), bdiag(p[hp])) for hp in pairs]
        for low in merge_masks:
            t = [_dot(jnp.where(low, m[hp], 0.0).astype(BF16), bdiag(p[hp])) for hp in pairs]
            p = [p[hp] - _dot(p[hp].astype(BF16), bdiag(t[hp])) for hp in pairs]
        eg = [jnp.exp(gc[h]) for h in heads]
        rhs = [jnp.concatenate([kb[h] * eg[h], v[h] * bc[h]], axis=1) for h in heads]
        wu = [_dot(p[hp].astype(BF16), bdiag2(rhs[2 * hp], rhs[2 * hp + 1])) for hp in pairs]
        wd = GDN_DK + GDN_DV
        w = [wu[h // 2][:, (h % 2) * wd:(h % 2) * wd + GDN_DK] for h in heads]
        u = [wu[h // 2][:, (h % 2) * wd + GDN_DK:(h % 2 + 1) * wd] for h in heads]
        st = [s_ref[h] for h in heads]
        ws = [_dot(jnp.concatenate([w[h], q[h] * eg[h]], axis=0).astype(BF16), st[h].astype(BF16)) for h in heads]
        v_new = [u[h] - ws[h][:chunk] for h in heads]
        av = [_dot(attn[hp].astype(BF16), bdiag2(v_new[2 * hp], v_new[2 * hp + 1])) for hp in pairs]
        for h in heads:
            o_ref[0, rows, hs[h]] = ws[h][chunk:] + av[h // 2][:, (h % 2) * GDN_DV:(h % 2 + 1) * GDN_DV]
        for h in heads:
            g_last = gc[h][chunk - 1:chunk, :]
            kd = k[h] * jnp.exp(g_last - gc[h])
            s_ref[h] = st[h] * jnp.exp(g_last) + _dot_tn(kd.astype(BF16), v_new[h].astype(BF16))
        return carry

    lax.fori_loop(0, tm // chunk, one_chunk, 0)
    sout_ref[0] = s_ref[...]


def _gdn_chunks(q, k, v, gbc, gbr, s0, tm, chunk):
    b, l, _ = q.shape
    row = lambda shape: pl.BlockSpec(shape, lambda i, j: (i, j, 0))
    sspec = pl.BlockSpec((1, GDN_HEADS, GDN_DK, GDN_DV), lambda i, j: (i, 0, 0, 0))
    return pl.pallas_call(
        functools.partial(_gdn_chunk_kernel, tm=tm, chunk=chunk),
        grid=(b, l // tm),
        in_specs=[
            row((1, tm, GDN_QK_W)), row((1, tm, GDN_QK_W)), row((1, tm, GDN_V_W)), row((1, tm, LANES)),
            pl.BlockSpec((1, tm // chunk, 2 * GDN_HEADS, chunk), lambda i, j: (i, j, 0, 0)),
            sspec,
        ],
        out_specs=[row((1, tm, GDN_V_W)), sspec],
        out_shape=[jax.ShapeDtypeStruct((b, l, GDN_V_W), F32),
                   jax.ShapeDtypeStruct((b, GDN_HEADS, GDN_DK, GDN_DV), F32)],
        scratch_shapes=[pltpu.VMEM((GDN_HEADS, GDN_DK, GDN_DV), F32)],
        compiler_params=_params("arbitrary", "arbitrary"),
        name="gdn_chunks",
    )(q, k, v, gbc, gbr, s0)


def _gdn_out_kernel(o_ref, z_ref, x_ref, mod_ref, gn_ref, wout_ref, nf_ref, wrh_ref, wrl_ref, br_ref,
                    x1_ref, hn2_ref, route_ref, cnt_ref):
    _init_counts(cnt_ref)
    o = o_ref[0]
    z = z_ref[0]
    gn = gn_ref[...]
    parts = []
    for h in range(GDN_HEADS):
        s = slice(h * GDN_DV, (h + 1) * GDN_DV)
        parts.append((_rms(o[:, s]) * gn) * _silu(z[:, s]))
    on = jnp.concatenate(parts, axis=1).astype(BF16)
    y = _dot(on, wout_ref[...])
    x1, hn2, route = _post_mixer(x_ref[0], y, mod_ref[0], nf_ref[...], wrh_ref[...], wrl_ref[...], br_ref[...],
                                 cnt_ref)
    x1_ref[0] = x1
    hn2_ref[0] = hn2.astype(BF16)
    route_ref[0] = route


def _gdn_out(o, z, x, mod, gn, wout, nf, wrh, wrl, br, tm):
    b, l, d = x.shape
    row = lambda shape: pl.BlockSpec(shape, lambda i, j: (i, j, 0))
    return pl.pallas_call(
        _gdn_out_kernel,
        grid=(b, l // tm),
        in_specs=[
            row((1, tm, GDN_V_W)), row((1, tm, GDN_V_W)), row((1, tm, d)),
            pl.BlockSpec((1, 6, d), lambda i, j: (i, 0, 0)),
            _const_spec((1, GDN_DV)), _const_spec(wout.shape), _const_spec((1, d)),
            _const_spec(wrh.shape), _const_spec(wrl.shape), _const_spec((1, LANES)),
        ],
        out_specs=[row((1, tm, d)), row((1, tm, d)), row((1, tm, LANES)), _const_spec((1, LANES))],
        out_shape=[jax.ShapeDtypeStruct((b, l, d), F32), jax.ShapeDtypeStruct((b, l, d), BF16),
                   jax.ShapeDtypeStruct((b, l, LANES), F32), jax.ShapeDtypeStruct((1, LANES), F32)],
        compiler_params=_params("arbitrary", "arbitrary"),
        name="gdn_out",
    )(o, z, x, mod, gn, wout, nf, wrh, wrl, br)


def _gelu_tanh(x):
    return 0.5 * x * (1.0 + jnp.tanh(math.sqrt(2.0 / math.pi) * (x + 0.044715 * (x * x * x))))


def _lru_kernel(xa_ref, ya_ref, yb_ref, rprev_ref, mprev_ref,
                mod_ref, ng_ref, win_ref, cw_ref, cb_ref, wax_ref, ba_ref, bx_ref, lam_ref, wout_ref,
                hist0_ref, h0_ref, nf_ref, wrh_ref, wrl_ref, br_ref,
                x1_ref, hn2_ref, route_ref, hist_ref, hlast_ref, cnt_ref, xp_ref, h_ref, *, tm):
    l = pl.program_id(1)
    _init_counts(cnt_ref)

    @pl.when(l == 0)
    def _():
        xp_ref[0:8, :] = jnp.zeros((8, LRU_WIDTH), F32)
        xp_ref[5:8, :] = hist0_ref[0]
        h_ref[...] = h0_ref[0]

    x = _moe_residual(xa_ref[0], ya_ref[0], yb_ref[0], rprev_ref[0], mprev_ref[0])
    mod = mod_ref[0]
    hn = (_rms(x) * ng_ref[...]) * (1.0 + mod[1:2]) + mod[0:1]
    proj = _dot(hn.astype(BF16), win_ref[...])
    gate_br = _gelu_tanh(proj[:, :LRU_WIDTH])
    xb = proj[:, LRU_WIDTH:]
    xp_ref[8:8 + tm, :] = xb
    cw = cw_ref[...]
    xc = cw[3:4] * xb
    for j in range(CONV_W - 1):
        xc = xc + cw[j:j + 1] * xp_ref[5 + j:5 + j + tm, :]
    xc = xc + cb_ref[...]
    new_hist = xp_ref[tm + 5:tm + 8, :]
    xp_ref[5:8, :] = new_hist
    hist_ref[0] = new_hist
    xcb = xc.astype(BF16)
    ra, ia = [], []
    for h in range(LRU_BLOCKS):
        s = slice(h * LRU_BLOCK_W, (h + 1) * LRU_BLOCK_W)
        r2 = _dot(xcb[:, s], wax_ref[h])
        ra.append(r2[:, :LRU_BLOCK_W])
        ia.append(r2[:, LRU_BLOCK_W:])
    r = _sigmoid(jnp.concatenate(ra, axis=1) + ba_ref[...])
    i = _sigmoid(jnp.concatenate(ia, axis=1) + bx_ref[...])
    log_a = (-LRU_C * r) * _softplus(-lam_ref[...])
    a = jnp.exp(log_a)
    mult = jnp.sqrt(-_expm1(2.0 * log_a))
    b = mult * (i * xc)
    sub = lax.broadcasted_iota(jnp.int32, (tm, LRU_WIDTH), 0) % SUBLANES
    sft = 1
    while sft < SUBLANES:
        keep = sub >= sft
        a_prev = jnp.where(keep, pltpu.roll(a, sft, 0), 1.0)
        b_prev = jnp.where(keep, pltpu.roll(b, sft, 0), 0.0)
        b = a * b_prev + b
        a = a * a_prev
        sft *= 2
    h = h_ref[...]
    groups = []
    for g in range(tm // SUBLANES):
        rows = slice(g * SUBLANES, (g + 1) * SUBLANES)
        hg = b[rows] + a[rows] * h
        groups.append(hg)
        h = hg[SUBLANES - 1:SUBLANES, :]
    hs = jnp.concatenate(groups, axis=0)
    h_last = h
    h_ref[...] = h_last
    hlast_ref[0] = h_last
    y = _dot((hs * gate_br).astype(BF16), wout_ref[...])
    x1, hn2, route = _post_mixer(x, y, mod, nf_ref[...], wrh_ref[...], wrl_ref[...], br_ref[...], cnt_ref)
    x1_ref[0] = x1
    hn2_ref[0] = hn2.astype(BF16)
    route_ref[0] = route


def _lru_layer(xa, ya, yb, rprev, mprev, mod, ng, win, cw, cb, wax, ba, bx, lam, wout, hist0, h0, nf, wrh, wrl, br, tm):
    b, l, d = xa.shape
    row = lambda shape: pl.BlockSpec(shape, lambda i, j: (i, j, 0))
    per_b = lambda shape: pl.BlockSpec(shape, lambda i, j: (i, 0, 0))
    vec = _const_spec((1, d))
    return pl.pallas_call(
        functools.partial(_lru_kernel, tm=tm),
        grid=(b, l // tm),
        in_specs=[
            row((1, tm, d)), row((1, tm, d)), row((1, tm, d)), row((1, tm, LANES)), per_b((1, 6, d)),
            per_b((1, 6, d)), vec, _const_spec(win.shape), _const_spec(cw.shape), vec,
            _const_spec(wax.shape), vec, vec, vec, _const_spec(wout.shape),
            per_b((1, CONV_W - 1, LRU_WIDTH)), per_b((1, 1, LRU_WIDTH)), vec,
            _const_spec(wrh.shape), _const_spec(wrl.shape), _const_spec((1, LANES)),
        ],
        out_specs=[row((1, tm, d)), row((1, tm, d)), row((1, tm, LANES)),
                   per_b((1, CONV_W - 1, LRU_WIDTH)), per_b((1, 1, LRU_WIDTH)), _const_spec((1, LANES))],
        out_shape=[jax.ShapeDtypeStruct((b, l, d), F32), jax.ShapeDtypeStruct((b, l, d), BF16),
                   jax.ShapeDtypeStruct((b, l, LANES), F32),
                   jax.ShapeDtypeStruct((b, CONV_W - 1, LRU_WIDTH), F32),
                   jax.ShapeDtypeStruct((b, 1, LRU_WIDTH), F32), jax.ShapeDtypeStruct((1, LANES), F32)],
        scratch_shapes=[pltpu.VMEM((tm + 8, LRU_WIDTH), F32), pltpu.VMEM((1, LRU_WIDTH), F32)],
        compiler_params=_params("arbitrary", "arbitrary"),
        name="lru_layer",
    )(xa, ya, yb, rprev, mprev, mod, ng, win, cw, cb, wax, ba, bx, lam, wout, hist0, h0, nf, wrh, wrl, br)


def _moe_kernel(be_ref, xs_ref, w1_ref, w3_ref, w2_ref, y_ref, w1b_ref, w3b_ref, w2b_ref):
    i = pl.program_id(0)
    prev = be_ref[jnp.maximum(i - 1, 0)]

    @pl.when((i == 0) | (be_ref[i] != prev))
    def _():
        w1b_ref[...] = w1_ref[0].astype(BF16)
        w3b_ref[...] = w3_ref[0].astype(BF16)
        w2b_ref[...] = w2_ref[0].astype(BF16)

    xb = xs_ref[...].astype(BF16)
    hid = _silu(_dot(xb, w1b_ref[...])) * _dot(xb, w3b_ref[...])
    y_ref[...] = _dot(hid.astype(BF16), w2b_ref[...])


def _moe_blocks(blk_expert, xs, w1, w3, w2):
    p, d = xs.shape
    nb = p // MOE_BLOCK
    grid_spec = pltpu.PrefetchScalarGridSpec(
        num_scalar_prefetch=1,
        grid=(nb,),
        in_specs=[
            pl.BlockSpec((MOE_BLOCK, d), lambda i, be: (i, 0)),
            pl.BlockSpec((1, d, D_FF_EXPERT), lambda i, be: (be[i], 0, 0)),
            pl.BlockSpec((1, d, D_FF_EXPERT), lambda i, be: (be[i], 0, 0)),
            pl.BlockSpec((1, D_FF_EXPERT, d), lambda i, be: (be[i], 0, 0)),
        ],
        out_specs=pl.BlockSpec((MOE_BLOCK, d), lambda i, be: (i, 0)),
        scratch_shapes=[pltpu.VMEM((d, D_FF_EXPERT), BF16), pltpu.VMEM((d, D_FF_EXPERT), BF16),
                        pltpu.VMEM((D_FF_EXPERT, d), BF16)],
    )
    return pl.pallas_call(
        _moe_kernel,
        grid_spec=grid_spec,
        out_shape=jax.ShapeDtypeStruct((p, d), F32),
        compiler_params=_params("arbitrary"),
        name="moe_blocks",
    )(blk_expert, xs, w1, w3, w2)


def _moe_residual(x1, ya, yb, route, mod):
    return x1 + mod[5:6] * (ya * route[:, 2:3] + yb * route[:, 3:4])


def _final_kernel(x1_ref, ya_ref, yb_ref, route_ref, mod_ref, no_ref, o_ref):
    x2 = _moe_residual(x1_ref[0], ya_ref[0], yb_ref[0], route_ref[0], mod_ref[0])
    o_ref[0] = _rms(x2) * no_ref[...]


def _final(x1, ya, yb, route, mod, norm_out, tm):
    b, l, d = x1.shape
    row = lambda shape: pl.BlockSpec(shape, lambda i, j: (i, j, 0))
    return pl.pallas_call(
        _final_kernel,
        grid=(b, l // tm),
        in_specs=[row((1, tm, d)), row((1, tm, d)), row((1, tm, d)), row((1, tm, LANES)),
                  pl.BlockSpec((1, 6, d), lambda i, j: (i, 0, 0)), _const_spec((1, d))],
        out_specs=row((1, tm, d)),
        out_shape=jax.ShapeDtypeStruct((b, l, d), F32),
        compiler_params=_params("arbitrary", "arbitrary"),
        name="final",
    )(x1, ya, yb, route, mod, norm_out)


def _moe_experts(hn2, route, counts, w1, w3, w2):
    b, l, d = hn2.shape
    n = b * l
    a = n * EXPERT_TOPK
    rt = route.reshape(n, LANES)
    e_ab = rt[:, 0:EXPERT_TOPK].astype(jnp.int32)
    r_ab = rt[:, 4:4 + EXPERT_TOPK].astype(jnp.int32)
    counts = counts[0, :N_EXPERTS].astype(jnp.int32)
    padded = (counts + MOE_BLOCK - 1) // MOE_BLOCK * MOE_BLOCK
    pad_end = jnp.cumsum(padded)
    pad_start = pad_end - padded
    dest = pad_start[e_ab] + r_ab
    n_blocks = -(-a // MOE_BLOCK) + N_EXPERTS
    p = n_blocks * MOE_BLOCK
    tok = jnp.broadcast_to(jnp.arange(n, dtype=jnp.int32)[:, None], (n, EXPERT_TOPK))
    slot_tok = jnp.zeros((p,), jnp.int32).at[dest.reshape(-1)].set(tok.reshape(-1))
    blk_first = jnp.arange(n_blocks, dtype=jnp.int32) * MOE_BLOCK
    blk_expert = jnp.minimum(jnp.sum((pad_end[None, :] <= blk_first[:, None]).astype(jnp.int32), axis=1),
                             N_EXPERTS - 1)
    xs = hn2.reshape(n, d)[slot_tok]
    ys = _moe_blocks(blk_expert, xs, w1, w3, w2)
    return ys[dest[:, 0]].reshape(b, l, d), ys[dest[:, 1]].reshape(b, l, d)


def _trunk(x, mods, gdn_s, gdn_conv, lru_h, lru_conv, wp):
    b, l, d = x.shape
    tm = min(ROW_TILE, l)
    chunk = min(GDN_CHUNK, l)
    assert l % tm == 0 and tm % chunk == 0 and chunk & (chunk - 1) == 0
    mod = mods[0]
    q, k, v, z, gbc, gbr, gconv_new = _gdn_in(x, mod, wp['norm_mix0'], wp['gdn_wqkv'], wp['gdn_wz'], wp['gdn_wab'],
                                              wp['gdn_conv_w'], gdn_conv, wp['gdn_alog'], wp['gdn_dtb'], tm, chunk)
    o, s_new = _gdn_chunks(q, k, v, gbc, gbr, gdn_s, tm, chunk)
    x1, hn2, route, counts = _gdn_out(o, z, x, mod, wp['gdn_norm'], wp['gdn_wout'], wp['norm_ffn0'],
                                      wp['wr_hi0'], wp['wr_lo0'], wp['br0'], tm)
    ya, yb = _moe_experts(hn2, route, counts, wp['moe_w1'][0], wp['moe_w3'][0], wp['moe_w2'][0])
    x1, hn2, route, lconv_new, h_new, counts = _lru_layer(
        x1, ya, yb, route, mod, mods[1], wp['norm_mix1'], wp['lru_win'], wp['lru_conv_w'], wp['lru_conv_b'],
        wp['lru_wax'], wp['lru_ba'], wp['lru_bx'], wp['lru_lam'], wp['lru_wout'], lru_conv, lru_h, wp['norm_ffn1'],
        wp['wr_hi1'], wp['wr_lo1'], wp['br1'], tm)
    ya, yb = _moe_experts(hn2, route, counts, wp['moe_w1'][1], wp['moe_w3'][1], wp['moe_w2'][1])
    y = _final(x1, ya, yb, route, mods[1], wp['norm_out'], tm)
    return y, s_new[None], gconv_new[None], h_new.reshape(1, b, LRU_WIDTH), lconv_new[None]


def _pad_lanes(v, width=LANES):
    v = v.reshape(1, -1)
    return jnp.pad(v, ((0, 0), (0, width - v.shape[1])))


def _router_weights(w_rg, b_rg, w_re, b_re):
    w = jnp.pad(jnp.concatenate([w_rg, w_re], axis=1), ((0, 0), (0, LANES - N_GROUPS - N_EXPERTS)))
    hi = w.astype(BF16)
    lo = (w - hi.astype(F32)).astype(BF16)
    return hi, lo, _pad_lanes(jnp.concatenate([b_rg, b_re]))


def kernel(x_prompt, x_sample, state_gdn_S, state_gdn_conv, state_lru_h, state_lru_conv, c_prompt, c_sample, w_ada, b_ada, norm_mix, norm_ffn, norm_out, gdn_w_in, gdn_conv_w, gdn_a_log, gdn_dt_bias, gdn_norm, gdn_w_out, lru_w_in, lru_conv_w, lru_conv_b, lru_w_a, lru_b_a, lru_w_x, lru_b_x, lru_lambda, lru_w_out, moe_w_rg, moe_b_rg, moe_w_re, moe_b_re, moe_w1, moe_w3, moe_w2):
    d = D_MODEL
    bp = x_prompt.shape[0]
    bs = x_sample.shape[0]
    win = gdn_w_in[0]
    wab = jnp.pad(win[:, GDN_CONV_CH + GDN_V_W:], ((0, 0), (0, LANES - 2 * GDN_HEADS)))
    wp = dict(
        norm_mix0=norm_mix[0].reshape(1, d), norm_mix1=norm_mix[1].reshape(1, d),
        norm_ffn0=norm_ffn[0].reshape(1, d), norm_ffn1=norm_ffn[1].reshape(1, d),
        norm_out=norm_out.reshape(1, d),
        gdn_wqkv=win[:, :GDN_CONV_CH].astype(BF16),
        gdn_wz=win[:, GDN_CONV_CH:GDN_CONV_CH + GDN_V_W].astype(BF16),
        gdn_wab=wab.astype(BF16),
        gdn_conv_w=gdn_conv_w[0],
        gdn_alog=_pad_lanes(gdn_a_log[0]), gdn_dtb=_pad_lanes(gdn_dt_bias[0]),
        gdn_norm=gdn_norm[0].reshape(1, GDN_DV), gdn_wout=gdn_w_out[0].astype(BF16),
        lru_win=lru_w_in[0].astype(BF16), lru_conv_w=lru_conv_w[0], lru_conv_b=lru_conv_b[0].reshape(1, d),
        lru_wax=jnp.concatenate([lru_w_a[0], lru_w_x[0]], axis=-1).astype(BF16),
        lru_ba=lru_b_a[0].reshape(1, d), lru_bx=lru_b_x[0].reshape(1, d), lru_lam=lru_lambda[0].reshape(1, d),
        lru_wout=lru_w_out[0].astype(BF16),
        moe_w1=moe_w1, moe_w3=moe_w3, moe_w2=moe_w2,
    )
    for i in range(DEPTH):
        wp[f'wr_hi{i}'], wp[f'wr_lo{i}'], wp[f'br{i}'] = _router_weights(moe_w_rg[i], moe_b_rg[i], moe_w_re[i], moe_b_re[i])

    mods = _ada_mod(jnp.concatenate([c_prompt, c_sample], axis=0), w_ada, b_ada)
    mods = mods.reshape(DEPTH, bp + bs, 6, d)
    mods_p = [mods[i, :bp] for i in range(DEPTH)]
    mods_s = [mods[i, bp:] for i in range(DEPTH)]

    dt = x_prompt.dtype
    z_s = jnp.zeros((bp, GDN_HEADS, GDN_DK, GDN_DV), dt)
    z_gc = jnp.zeros((bp, CONV_W - 1, GDN_CONV_CH), dt)
    z_h = jnp.zeros((bp, 1, LRU_WIDTH), dt)
    z_lc = jnp.zeros((bp, CONV_W - 1, LRU_WIDTH), dt)
    y_p, gs_p, gc_p, lh_p, lc_p = _trunk(x_prompt, mods_p, z_s, z_gc, z_h, z_lc, wp)
    y_s, gs_s, gc_s, lh_s, lc_s = _trunk(x_sample, mods_s, state_gdn_S[0], state_gdn_conv[0],
                                         state_lru_h[0].reshape(bs, 1, LRU_WIDTH), state_lru_conv[0], wp)
    return (y_p, y_s, gs_p, gc_p, lh_p, lc_p, gs_s, gc_s, lh_s, lc_s)
```

```python
import functools
import math

import jax
import jax.numpy as jnp
from jax import lax
from jax.experimental import pallas as pl
from jax.experimental.pallas import tpu as pltpu

F32 = jnp.float32
BF16 = jnp.bfloat16

D_MODEL = 1024
DEPTH = 2
CONV_W = 4
NORM_EPS = 1e-6
GDN_HEADS = 8
GDN_DK = 128
GDN_DV = 128
GDN_QK_W = GDN_HEADS * GDN_DK
GDN_V_W = GDN_HEADS * GDN_DV
GDN_CONV_CH = 2 * GDN_QK_W + GDN_V_W
GDN_CHUNK = 64
GDN_INV_BLOCK = 16
LRU_WIDTH = D_MODEL
LRU_BLOCKS = 8
LRU_BLOCK_W = LRU_WIDTH // LRU_BLOCKS
LRU_C = 8.0
N_GROUPS = 4
EXPERTS_PER_GROUP = 8
N_EXPERTS = N_GROUPS * EXPERTS_PER_GROUP
EXPERT_TOPK = 2
D_FF_EXPERT = 512
MOE_BLOCK = 256
LANES = 128
SUBLANES = 8
ROW_TILE = 256
VMEM_LIMIT = 56 * 1024 * 1024


def _dot(a, b):
    return jnp.dot(a, b, preferred_element_type=F32)


def _dot_nt(a, b):
    return lax.dot_general(a, b, (((1,), (1,)), ((), ())), preferred_element_type=F32)


def _dot_tn(a, b):
    return lax.dot_general(a, b, (((0,), (0,)), ((), ())), preferred_element_type=F32)


def _split3(x):
    a = x.astype(BF16)
    r = x - a.astype(F32)
    b = r.astype(BF16)
    c = (r - b.astype(F32)).astype(BF16)
    return a, b, c


def _rms(x):
    return x * lax.rsqrt(jnp.mean(x * x, axis=-1, keepdims=True) + NORM_EPS)


def _sigmoid(x):
    return 1.0 / (1.0 + jnp.exp(-x))


def _silu(x):
    return x * _sigmoid(x)


def _softplus(x):
    return jnp.maximum(x, 0.0) + jnp.log1p(jnp.exp(-jnp.abs(x)))


def _expm1(x):
    u = jnp.exp(x)
    near = (u > 0.5) & (u < 2.0) & (u != 1.0)
    corrected = (u - 1.0) * x / jnp.where(near, jnp.log(u), 1.0)
    return jnp.where(u == 1.0, x, jnp.where(near, corrected, u - 1.0))


def _params(*sem):
    return pltpu.CompilerParams(dimension_semantics=sem, vmem_limit_bytes=VMEM_LIMIT)


def _const_spec(shape):
    nd = len(shape)
    return pl.BlockSpec(shape, lambda *_: (0,) * nd)


def _ada_kernel(c_ref, w_ref, b_ref, o_ref):
    s = _silu(c_ref[...]).astype(BF16)
    o_ref[0] = _dot(s, w_ref[0].astype(BF16)) + b_ref[0]


def _ada_mod(c_all, w_ada, b_ada):
    bt = c_all.shape[0]
    tn = 1536
    n6 = 6 * D_MODEL
    return pl.pallas_call(
        _ada_kernel,
        grid=(DEPTH, n6 // tn),
        in_specs=[
            pl.BlockSpec((bt, D_MODEL), lambda i, j: (0, 0)),
            pl.BlockSpec((1, D_MODEL, tn), lambda i, j: (i, 0, j)),
            pl.BlockSpec((1, 1, tn), lambda i, j: (i, 0, j)),
        ],
        out_specs=pl.BlockSpec((1, bt, tn), lambda i, j: (i, 0, j)),
        out_shape=jax.ShapeDtypeStruct((DEPTH, bt, n6), F32),
        compiler_params=_params("arbitrary", "arbitrary"),
        name="ada_mod",
    )(c_all, w_ada, b_ada.reshape(DEPTH, 1, n6))


def _route(hn2, wr_hi, wr_lo, br, cnt_ref):
    h_hi = hn2.astype(BF16)
    h_lo = (hn2 - h_hi.astype(F32)).astype(BF16)
    logits = _dot(h_hi, wr_hi) + (_dot(h_hi, wr_lo) + _dot(h_lo, wr_hi)) + br
    lane = lax.broadcasted_iota(jnp.int32, logits.shape, 1)
    neg = jnp.float32(-jnp.inf)
    big = jnp.int32(1 << 20)
    is_g = lane < N_GROUPS
    lg = jnp.where(is_g, logits, neg)
    eg = jnp.exp(lg - jnp.max(lg, axis=-1, keepdims=True))
    pg = eg / jnp.sum(eg, axis=-1, keepdims=True)
    pg = jnp.where(is_g, pg, -1.0)
    gate_g = jnp.max(pg, axis=-1, keepdims=True)
    grp = jnp.min(jnp.where(pg == gate_g, lane, big), axis=-1, keepdims=True)
    lo = N_GROUPS + grp * EXPERTS_PER_GROUP
    is_e = (lane >= lo) & (lane < lo + EXPERTS_PER_GROUP)
    le = jnp.where(is_e, logits, neg)
    ee = jnp.exp(le - jnp.max(le, axis=-1, keepdims=True))
    pe = ee / jnp.sum(ee, axis=-1, keepdims=True)
    pe = jnp.where(is_e, pe, -1.0)
    p1 = jnp.max(pe, axis=-1, keepdims=True)
    i1 = jnp.min(jnp.where(pe == p1, lane, big), axis=-1, keepdims=True)
    pe2 = jnp.where(lane == i1, -1.0, pe)
    p2 = jnp.max(pe2, axis=-1, keepdims=True)
    i2 = jnp.min(jnp.where(pe2 == p2, lane, big), axis=-1, keepdims=True)
    tot = p1 + p2
    w1 = gate_g * (p1 / tot)
    w2 = gate_g * (p2 / tot)
    e1 = i1 - N_GROUPS
    e2 = i2 - N_GROUPS
    tm = logits.shape[0]
    oh1 = lane == e1
    oh2 = lane == e2
    cnt = jnp.where(oh1 | oh2, 1.0, 0.0)
    ri = lax.broadcasted_iota(jnp.int32, (tm, tm), 0)
    ci = lax.broadcasted_iota(jnp.int32, (tm, tm), 1)
    before = jnp.where(ci < ri, 1.0, 0.0).astype(BF16)
    pos = _dot(before, cnt.astype(BF16)) + cnt_ref[...]
    r1 = jnp.sum(jnp.where(oh1, pos, 0.0), axis=-1, keepdims=True)
    r2 = jnp.sum(jnp.where(oh2, pos, 0.0), axis=-1, keepdims=True)
    cnt_ref[...] = cnt_ref[...] + jnp.sum(cnt, axis=0, keepdims=True)
    vals = (e1.astype(F32), e2.astype(F32), w1, w2, r1, r2)
    out = jnp.zeros(logits.shape, F32)
    for j, val in enumerate(vals):
        out = jnp.where(lane == j, val, out)
    return out


def _init_counts(cnt_ref):
    @pl.when((pl.program_id(0) == 0) & (pl.program_id(1) == 0))
    def _():
        cnt_ref[...] = jnp.zeros(cnt_ref.shape, F32)


def _post_mixer(x, y, mod, nf, wr_hi, wr_lo, br, cnt_ref):
    g1 = mod[2:3]
    sh2 = mod[3:4]
    sc2 = mod[4:5]
    x1 = x + g1 * y
    hn2 = (_rms(x1) * nf) * (1.0 + sc2) + sh2
    return x1, hn2, _route(hn2, wr_hi, wr_lo, br, cnt_ref)


def _gdn_in_kernel(x_ref, mod_ref, ng_ref, wqkv_ref, wz_ref, wab_ref, cw_ref, hist0_ref, alog_ref, dtb_ref,
                   q_ref, k_ref, v_ref, z_ref, gbc_ref, gbr_ref, hist_ref, xp_ref, *, tm, chunk):
    l = pl.program_id(1)

    @pl.when(l == 0)
    def _():
        xp_ref[0:8, :] = jnp.zeros((8, GDN_CONV_CH), F32)
        xp_ref[5:8, :] = hist0_ref[0]

    x = x_ref[0]
    mod = mod_ref[0]
    hn = (_rms(x) * ng_ref[...]) * (1.0 + mod[1:2]) + mod[0:1]
    hb = hn.astype(BF16)
    qkv = _dot(hb, wqkv_ref[...])
    xp_ref[8:8 + tm, :] = qkv
    cw = cw_ref[...]
    y = cw[3:4] * qkv
    for j in range(CONV_W - 1):
        y = y + cw[j:j + 1] * xp_ref[5 + j:5 + j + tm, :]
    new_hist = xp_ref[tm + 5:tm + 8, :]
    xp_ref[5:8, :] = new_hist
    hist_ref[0] = new_hist
    y = _silu(y)
    for h in range(GDN_HEADS):
        s = slice(h * GDN_DK, (h + 1) * GDN_DK)
        qh = y[:, s]
        q_ref[0, :, s] = qh * lax.rsqrt(jnp.sum(qh * qh, axis=-1, keepdims=True) + 1e-6) * (GDN_DK ** -0.5)
        kh = y[:, GDN_QK_W + h * GDN_DK:GDN_QK_W + (h + 1) * GDN_DK]
        k_ref[0, :, s] = kh * lax.rsqrt(jnp.sum(kh * kh, axis=-1, keepdims=True) + 1e-6)
    v_ref[0] = y[:, 2 * GDN_QK_W:]
    z_ref[0] = _dot(hb, wz_ref[...])
    ab = _dot(hb, wab_ref[...])
    g = -jnp.exp(alog_ref[...]) * _softplus(ab + dtb_ref[...])
    beta = _sigmoid(ab)
    ri = lax.broadcasted_iota(jnp.int32, (tm, tm), 0)
    ci = lax.broadcasted_iota(jnp.int32, (tm, tm), 1)
    tri = jnp.where((ri // chunk == ci // chunk) & (ci <= ri), 1.0, 0.0).astype(BF16)
    g1, g2, g3 = _split3(g)
    gcum = (_dot(tri, g1) + _dot(tri, g2)) + _dot(tri, g3)
    lane = lax.broadcasted_iota(jnp.int32, (tm, LANES), 1)
    gb = jnp.where(lane < GDN_HEADS, gcum, beta)
    gbc_ref[0] = gb
    er = lax.broadcasted_iota(jnp.int32, (2 * GDN_HEADS, LANES), 0)
    ec = lax.broadcasted_iota(jnp.int32, (2 * GDN_HEADS, LANES), 1)
    sel = jnp.where(er == ec, 1.0, 0.0).astype(BF16)
    b1, b2, b3 = _split3(gb)
    for n in range(tm // chunk):
        r = slice(n * chunk, (n + 1) * chunk)
        gbr_ref[0, n] = (_dot_nt(sel, b1[r]) + _dot_nt(sel, b2[r])) + _dot_nt(sel, b3[r])


def _gdn_in(x, mod, ng, wqkv, wz, wab, cw, hist0, alog, dtb, tm, chunk):
    b, l, d = x.shape
    grid = (b, l // tm)
    row = lambda shape: pl.BlockSpec(shape, lambda i, j: (i, j, 0))
    outs = pl.pallas_call(
        functools.partial(_gdn_in_kernel, tm=tm, chunk=chunk),
        grid=grid,
        in_specs=[
            row((1, tm, d)),
            pl.BlockSpec((1, 6, d), lambda i, j: (i, 0, 0)),
            _const_spec((1, d)),
            _const_spec(wqkv.shape),
            _const_spec(wz.shape),
            _const_spec(wab.shape),
            _const_spec(cw.shape),
            pl.BlockSpec((1, CONV_W - 1, GDN_CONV_CH), lambda i, j: (i, 0, 0)),
            _const_spec((1, LANES)),
            _const_spec((1, LANES)),
        ],
        out_specs=[
            row((1, tm, GDN_QK_W)), row((1, tm, GDN_QK_W)), row((1, tm, GDN_V_W)), row((1, tm, GDN_V_W)),
            row((1, tm, LANES)),
            pl.BlockSpec((1, tm // chunk, 2 * GDN_HEADS, chunk), lambda i, j: (i, j, 0, 0)),
            pl.BlockSpec((1, CONV_W - 1, GDN_CONV_CH), lambda i, j: (i, 0, 0)),
        ],
        out_shape=[
            jax.ShapeDtypeStruct((b, l, GDN_QK_W), F32), jax.ShapeDtypeStruct((b, l, GDN_QK_W), F32),
            jax.ShapeDtypeStruct((b, l, GDN_V_W), F32), jax.ShapeDtypeStruct((b, l, GDN_V_W), F32),
            jax.ShapeDtypeStruct((b, l, LANES), F32),
            jax.ShapeDtypeStruct((b, l // chunk, 2 * GDN_HEADS, chunk), F32),
            jax.ShapeDtypeStruct((b, CONV_W - 1, GDN_CONV_CH), F32),
        ],
        scratch_shapes=[pltpu.VMEM((tm + 8, GDN_CONV_CH), F32)],
        compiler_params=_params("arbitrary", "arbitrary"),
        name="gdn_in",
    )(x, mod, ng, wqkv, wz, wab, cw, hist0, alog, dtb)
    return outs


def _gdn_chunk_kernel(q_ref, k_ref, v_ref, gbc_ref, gbr_ref, s0_ref, o_ref, sout_ref, s_ref, *, tm, chunk):
    l = pl.program_id(1)

    @pl.when(l == 0)
    def _():
        s_ref[...] = s0_ref[0]

    c2 = 2 * chunk
    n_pairs = GDN_HEADS // 2
    ri = lax.broadcasted_iota(jnp.int32, (chunk, c2), 0)
    cl = lax.broadcasted_iota(jnp.int32, (chunk, c2), 1)
    ci = cl % chunk
    left = cl < chunk
    incl = ri >= ci
    strict = ri > ci
    eye = jnp.where(ri == ci, 1.0, 0.0).astype(F32)
    base = min(GDN_INV_BLOCK, chunk)
    levels = int(math.log2(base))
    diag_blk = (ri // base) == (ci // base)
    merge_masks = []
    blk = base
    while blk < chunk:
        merge_masks.append(((ri // (2 * blk)) == (ci // (2 * blk))) & ((ri // blk) % 2 == 1) & ((ci // blk) % 2 == 0))
        blk *= 2
    heads = range(GDN_HEADS)
    pairs = range(n_pairs)

    def bdiag(x):
        return jnp.concatenate([jnp.where(left, x, 0.0), jnp.where(left, 0.0, x)], axis=0).astype(BF16)

    def bdiag2(x0, x1):
        z = jnp.zeros_like(x0)
        return jnp.concatenate([jnp.concatenate([x0, z], axis=1), jnp.concatenate([z, x1], axis=1)], axis=0).astype(BF16)

    def one_chunk(n, carry):
        r0 = pl.multiple_of(n * chunk, chunk)
        rows = pl.ds(r0, chunk)
        gbc = gbc_ref[0, rows, :]
        gbr = gbr_ref[0, n]
        hs = [slice(h * GDN_DK, (h + 1) * GDN_DK) for h in heads]
        q = [q_ref[0, rows, hs[h]] for h in heads]
        k = [k_ref[0, rows, hs[h]] for h in heads]
        v = [v_ref[0, rows, hs[h]] for h in heads]
        gc = [gbc[:, h:h + 1] for h in heads]
        bc = [gbc[:, GDN_HEADS + h:GDN_HEADS + h + 1] for h in heads]
        kb = [k[h] * bc[h] for h in heads]
        a2 = [_dot_nt(jnp.concatenate([q[h], kb[h]], axis=0).astype(BF16), k[h].astype(BF16)) for h in heads]
        attn, m = [], []
        for hp in pairs:
            h0, h1 = 2 * hp, 2 * hp + 1
            gcp = jnp.where(left, gc[h0], gc[h1])
            grp = jnp.concatenate([gbr[h0:h0 + 1, :], gbr[h1:h1 + 1, :]], axis=1)
            decay = jnp.where(incl, jnp.exp(jnp.where(incl, gcp - grp, 0.0)), 0.0)
            ap = jnp.concatenate([a2[h0], a2[h1]], axis=1)
            attn.append(ap[:chunk] * decay)
            m.append(jnp.where(strict, ap[chunk:] * decay, 0.0))
        nq = [jnp.where(diag_blk, -m[hp], 0.0) for hp in pairs]
        p = [eye + nq[hp] for hp in pairs]
        nq = [_dot(nq[hp].astype(BF16), bdiag(nq[hp])) for hp in pairs]
        for j in range(1, levels):
            if j < levels - 1:
                r = [_dot(nq[hp].astype(BF16), jnp.concatenate([bdiag(nq[hp]), bdiag(p[hp])], axis=1)) for hp in pairs]
                nq = [r[hp][:, :c2] for hp in pairs]
                p = [p[hp] + r[hp][:, c2:] for hp in pairs]
            else:
                p = [p[hp] + _dot(nq[hp].astype(BF16), bdiag(p[hp])) for hp in pairs]
        for low in merge_masks:
            t = [_dot(jnp.where(low, m[hp], 0.0).astype(BF16), bdiag(p[hp])) for hp in pairs]
            p = [p[hp] - _dot(p[hp].astype(BF16), bdiag(t[hp])) for hp in pairs]
        eg = [jnp.exp(gc[h]) for h in heads]
        rhs = [jnp.concatenate([kb[h] * eg[h], v[h] * bc[h]], axis=1) for h in heads]
        wu = [_dot(p[hp].astype(BF16), bdiag2(rhs[2 * hp], rhs[2 * hp + 1])) for hp in pairs]
        wd = GDN_DK + GDN_DV
        w = [wu[h // 2][:, (h % 2) * wd:(h % 2) * wd + GDN_DK] for h in heads]
        u = [wu[h // 2][:, (h % 2) * wd + GDN_DK:(h % 2 + 1) * wd] for h in heads]
        st = [s_ref[h] for h in heads]
        ws = [_dot(jnp.concatenate([w[h], q[h] * eg[h]], axis=0).astype(BF16), st[h].astype(BF16)) for h in heads]
        v_new = [u[h] - ws[h][:chunk] for h in heads]
        av = [_dot(attn[hp].astype(BF16), bdiag2(v_new[2 * hp], v_new[2 * hp + 1])) for hp in pairs]
        for h in heads:
            o_ref[0, rows, hs[h]] = ws[h][chunk:] + av[h // 2][:, (h % 2) * GDN_DV:(h % 2 + 1) * GDN_DV]
        for h in heads:
            g_last = gc[h][chunk - 1:chunk, :]
            kd = k[h] * jnp.exp(g_last - gc[h])
            s_ref[h] = st[h] * jnp.exp(g_last) + _dot_tn(kd.astype(BF16), v_new[h].astype(BF16))
        return carry

    lax.fori_loop(0, tm // chunk, one_chunk, 0)
    sout_ref[0] = s_ref[...]


def _gdn_chunks(q, k, v, gbc, gbr, s0, tm, chunk):
    b, l, _ = q.shape
    row = lambda shape: pl.BlockSpec(shape, lambda i, j: (i, j, 0))
    sspec = pl.BlockSpec((1, GDN_HEADS, GDN_DK, GDN_DV), lambda i, j: (i, 0, 0, 0))
    return pl.pallas_call(
        functools.partial(_gdn_chunk_kernel, tm=tm, chunk=chunk),
        grid=(b, l // tm),
        in_specs=[
            row((1, tm, GDN_QK_W)), row((1, tm, GDN_QK_W)), row((1, tm, GDN_V_W)), row((1, tm, LANES)),
            pl.BlockSpec((1, tm // chunk, 2 * GDN_HEADS, chunk), lambda i, j: (i, j, 0, 0)),
            sspec,
        ],
        out_specs=[row((1, tm, GDN_V_W)), sspec],
        out_shape=[jax.ShapeDtypeStruct((b, l, GDN_V_W), F32),
                   jax.ShapeDtypeStruct((b, GDN_HEADS, GDN_DK, GDN_DV), F32)],
        scratch_shapes=[pltpu.VMEM((GDN_HEADS, GDN_DK, GDN_DV), F32)],
        compiler_params=_params("arbitrary", "arbitrary"),
        name="gdn_chunks",
    )(q, k, v, gbc, gbr, s0)


def _gdn_out_kernel(o_ref, z_ref, x_ref, mod_ref, gn_ref, wout_ref, nf_ref, wrh_ref, wrl_ref, br_ref,
                    x1_ref, hn2_ref, route_ref, cnt_ref):
    _init_counts(cnt_ref)
    o = o_ref[0]
    z = z_ref[0]
    gn = gn_ref[...]
    parts = []
    for h in range(GDN_HEADS):
        s = slice(h * GDN_DV, (h + 1) * GDN_DV)
        parts.append((_rms(o[:, s]) * gn) * _silu(z[:, s]))
    on = jnp.concatenate(parts, axis=1).astype(BF16)
    y = _dot(on, wout_ref[...])
    x1, hn2, route = _post_mixer(x_ref[0], y, mod_ref[0], nf_ref[...], wrh_ref[...], wrl_ref[...], br_ref[...],
                                 cnt_ref)
    x1_ref[0] = x1
    hn2_ref[0] = hn2
    route_ref[0] = route


def _gdn_out(o, z, x, mod, gn, wout, nf, wrh, wrl, br, tm):
    b, l, d = x.shape
    row = lambda shape: pl.BlockSpec(shape, lambda i, j: (i, j, 0))
    return pl.pallas_call(
        _gdn_out_kernel,
        grid=(b, l // tm),
        in_specs=[
            row((1, tm, GDN_V_W)), row((1, tm, GDN_V_W)), row((1, tm, d)),
            pl.BlockSpec((1, 6, d), lambda i, j: (i, 0, 0)),
            _const_spec((1, GDN_DV)), _const_spec(wout.shape), _const_spec((1, d)),
            _const_spec(wrh.shape), _const_spec(wrl.shape), _const_spec((1, LANES)),
        ],
        out_specs=[row((1, tm, d)), row((1, tm, d)), row((1, tm, LANES)), _const_spec((1, LANES))],
        out_shape=[jax.ShapeDtypeStruct((b, l, d), F32), jax.ShapeDtypeStruct((b, l, d), F32),
                   jax.ShapeDtypeStruct((b, l, LANES), F32), jax.ShapeDtypeStruct((1, LANES), F32)],
        compiler_params=_params("arbitrary", "arbitrary"),
        name="gdn_out",
    )(o, z, x, mod, gn, wout, nf, wrh, wrl, br)


def _gelu_tanh(x):
    return 0.5 * x * (1.0 + jnp.tanh(math.sqrt(2.0 / math.pi) * (x + 0.044715 * (x * x * x))))


def _lru_kernel(xa_ref, ya_ref, yb_ref, rprev_ref, mprev_ref,
                mod_ref, ng_ref, win_ref, cw_ref, cb_ref, wax_ref, ba_ref, bx_ref, lam_ref, wout_ref,
                hist0_ref, h0_ref, nf_ref, wrh_ref, wrl_ref, br_ref,
                x1_ref, hn2_ref, route_ref, hist_ref, hlast_ref, cnt_ref, xp_ref, h_ref, *, tm):
    l = pl.program_id(1)
    _init_counts(cnt_ref)

    @pl.when(l == 0)
    def _():
        xp_ref[0:8, :] = jnp.zeros((8, LRU_WIDTH), F32)
        xp_ref[5:8, :] = hist0_ref[0]
        h_ref[...] = h0_ref[0]

    x = _moe_residual(xa_ref[0], ya_ref[0], yb_ref[0], rprev_ref[0], mprev_ref[0])
    mod = mod_ref[0]
    hn = (_rms(x) * ng_ref[...]) * (1.0 + mod[1:2]) + mod[0:1]
    proj = _dot(hn.astype(BF16), win_ref[...])
    gate_br = _gelu_tanh(proj[:, :LRU_WIDTH])
    xb = proj[:, LRU_WIDTH:]
    xp_ref[8:8 + tm, :] = xb
    cw = cw_ref[...]
    xc = cw[3:4] * xb
    for j in range(CONV_W - 1):
        xc = xc + cw[j:j + 1] * xp_ref[5 + j:5 + j + tm, :]
    xc = xc + cb_ref[...]
    new_hist = xp_ref[tm + 5:tm + 8, :]
    xp_ref[5:8, :] = new_hist
    hist_ref[0] = new_hist
    xcb = xc.astype(BF16)
    ra, ia = [], []
    for h in range(LRU_BLOCKS):
        s = slice(h * LRU_BLOCK_W, (h + 1) * LRU_BLOCK_W)
        r2 = _dot(xcb[:, s], wax_ref[h])
        ra.append(r2[:, :LRU_BLOCK_W])
        ia.append(r2[:, LRU_BLOCK_W:])
    r = _sigmoid(jnp.concatenate(ra, axis=1) + ba_ref[...])
    i = _sigmoid(jnp.concatenate(ia, axis=1) + bx_ref[...])
    log_a = (-LRU_C * r) * _softplus(-lam_ref[...])
    a = jnp.exp(log_a)
    mult = jnp.sqrt(-_expm1(2.0 * log_a))
    b = mult * (i * xc)
    sub = lax.broadcasted_iota(jnp.int32, (tm, LRU_WIDTH), 0) % SUBLANES
    sft = 1
    while sft < SUBLANES:
        keep = sub >= sft
        a_prev = jnp.where(keep, pltpu.roll(a, sft, 0), 1.0)
        b_prev = jnp.where(keep, pltpu.roll(b, sft, 0), 0.0)
        b = a * b_prev + b
        a = a * a_prev
        sft *= 2
    h = h_ref[...]
    groups = []
    for g in range(tm // SUBLANES):
        rows = slice(g * SUBLANES, (g + 1) * SUBLANES)
        hg = b[rows] + a[rows] * h
        groups.append(hg)
        h = hg[SUBLANES - 1:SUBLANES, :]
    hs = jnp.concatenate(groups, axis=0)
    h_last = h
    h_ref[...] = h_last
    hlast_ref[0] = h_last
    y = _dot((hs * gate_br).astype(BF16), wout_ref[...])
    x1, hn2, route = _post_mixer(x, y, mod, nf_ref[...], wrh_ref[...], wrl_ref[...], br_ref[...], cnt_ref)
    x1_ref[0] = x1
    hn2_ref[0] = hn2
    route_ref[0] = route


def _lru_layer(xa, ya, yb, rprev, mprev, mod, ng, win, cw, cb, wax, ba, bx, lam, wout, hist0, h0, nf, wrh, wrl, br, tm):
    b, l, d = xa.shape
    row = lambda shape: pl.BlockSpec(shape, lambda i, j: (i, j, 0))
    per_b = lambda shape: pl.BlockSpec(shape, lambda i, j: (i, 0, 0))
    vec = _const_spec((1, d))
    return pl.pallas_call(
        functools.partial(_lru_kernel, tm=tm),
        grid=(b, l // tm),
        in_specs=[
            row((1, tm, d)), row((1, tm, d)), row((1, tm, d)), row((1, tm, LANES)), per_b((1, 6, d)),
            per_b((1, 6, d)), vec, _const_spec(win.shape), _const_spec(cw.shape), vec,
            _const_spec(wax.shape), vec, vec, vec, _const_spec(wout.shape),
            per_b((1, CONV_W - 1, LRU_WIDTH)), per_b((1, 1, LRU_WIDTH)), vec,
            _const_spec(wrh.shape), _const_spec(wrl.shape), _const_spec((1, LANES)),
        ],
        out_specs=[row((1, tm, d)), row((1, tm, d)), row((1, tm, LANES)),
                   per_b((1, CONV_W - 1, LRU_WIDTH)), per_b((1, 1, LRU_WIDTH)), _const_spec((1, LANES))],
        out_shape=[jax.ShapeDtypeStruct((b, l, d), F32), jax.ShapeDtypeStruct((b, l, d), F32),
                   jax.ShapeDtypeStruct((b, l, LANES), F32),
                   jax.ShapeDtypeStruct((b, CONV_W - 1, LRU_WIDTH), F32),
                   jax.ShapeDtypeStruct((b, 1, LRU_WIDTH), F32), jax.ShapeDtypeStruct((1, LANES), F32)],
        scratch_shapes=[pltpu.VMEM((tm + 8, LRU_WIDTH), F32), pltpu.VMEM((1, LRU_WIDTH), F32)],
        compiler_params=_params("arbitrary", "arbitrary"),
        name="lru_layer",
    )(xa, ya, yb, rprev, mprev, mod, ng, win, cw, cb, wax, ba, bx, lam, wout, hist0, h0, nf, wrh, wrl, br)


def _moe_kernel(be_ref, xs_ref, w1_ref, w3_ref, w2_ref, y_ref, w1b_ref, w3b_ref, w2b_ref):
    i = pl.program_id(0)
    prev = be_ref[jnp.maximum(i - 1, 0)]

    @pl.when((i == 0) | (be_ref[i] != prev))
    def _():
        w1b_ref[...] = w1_ref[0, 0].astype(BF16)
        w3b_ref[...] = w3_ref[0, 0].astype(BF16)
        w2b_ref[...] = w2_ref[0, 0].astype(BF16)

    xb = xs_ref[...].astype(BF16)
    hid = _silu(_dot(xb, w1b_ref[...])) * _dot(xb, w3b_ref[...])
    y_ref[...] = _dot(hid.astype(BF16), w2b_ref[...])


def _moe_blocks(blk_expert, xs, w1, w3, w2, layer):
    p, d = xs.shape
    nb = p // MOE_BLOCK
    grid_spec = pltpu.PrefetchScalarGridSpec(
        num_scalar_prefetch=1,
        grid=(nb,),
        in_specs=[
            pl.BlockSpec((MOE_BLOCK, d), lambda i, be: (i, 0)),
            pl.BlockSpec((1, 1, d, D_FF_EXPERT), lambda i, be: (layer, be[i], 0, 0)),
            pl.BlockSpec((1, 1, d, D_FF_EXPERT), lambda i, be: (layer, be[i], 0, 0)),
            pl.BlockSpec((1, 1, D_FF_EXPERT, d), lambda i, be: (layer, be[i], 0, 0)),
        ],
        out_specs=pl.BlockSpec((MOE_BLOCK, d), lambda i, be: (i, 0)),
        scratch_shapes=[pltpu.VMEM((d, D_FF_EXPERT), BF16), pltpu.VMEM((d, D_FF_EXPERT), BF16),
                        pltpu.VMEM((D_FF_EXPERT, d), BF16)],
    )
    return pl.pallas_call(
        _moe_kernel,
        grid_spec=grid_spec,
        out_shape=jax.ShapeDtypeStruct((p, d), F32),
        compiler_params=_params("arbitrary"),
        name="moe_blocks",
    )(blk_expert, xs, w1, w3, w2)


def _moe_residual(x1, ya, yb, route, mod):
    return x1 + mod[5:6] * (ya * route[:, 2:3] + yb * route[:, 3:4])


def _final_kernel(x1_ref, ya_ref, yb_ref, route_ref, mod_ref, no_ref, o_ref):
    x2 = _moe_residual(x1_ref[0], ya_ref[0], yb_ref[0], route_ref[0], mod_ref[0])
    o_ref[0] = _rms(x2) * no_ref[...]


def _final(x1, ya, yb, route, mod, norm_out, tm):
    b, l, d = x1.shape
    row = lambda shape: pl.BlockSpec(shape, lambda i, j: (i, j, 0))
    return pl.pallas_call(
        _final_kernel,
        grid=(b, l // tm),
        in_specs=[row((1, tm, d)), row((1, tm, d)), row((1, tm, d)), row((1, tm, LANES)),
                  pl.BlockSpec((1, 6, d), lambda i, j: (i, 0, 0)), _const_spec((1, d))],
        out_specs=row((1, tm, d)),
        out_shape=jax.ShapeDtypeStruct((b, l, d), F32),
        compiler_params=_params("arbitrary", "arbitrary"),
        name="final",
    )(x1, ya, yb, route, mod, norm_out)


def _moe_experts(hn2, route, counts, w1, w3, w2, layer):
    b, l, d = hn2.shape
    n = b * l
    a = n * EXPERT_TOPK
    rt = route.reshape(n, LANES)
    e_ab = rt[:, 0:EXPERT_TOPK].astype(jnp.int32)
    r_ab = rt[:, 4:4 + EXPERT_TOPK].astype(jnp.int32)
    counts = counts[0, :N_EXPERTS].astype(jnp.int32)
    padded = (counts + MOE_BLOCK - 1) // MOE_BLOCK * MOE_BLOCK
    pad_end = jnp.cumsum(padded)
    pad_start = pad_end - padded
    dest = pad_start[e_ab] + r_ab
    n_blocks = -(-a // MOE_BLOCK) + N_EXPERTS
    p = n_blocks * MOE_BLOCK
    tok = jnp.broadcast_to(jnp.arange(n, dtype=jnp.int32)[:, None], (n, EXPERT_TOPK))
    slot_tok = jnp.zeros((p,), jnp.int32).at[dest.reshape(-1)].set(tok.reshape(-1))
    blk_first = jnp.arange(n_blocks, dtype=jnp.int32) * MOE_BLOCK
    blk_expert = jnp.minimum(jnp.sum((pad_end[None, :] <= blk_first[:, None]).astype(jnp.int32), axis=1),
                             N_EXPERTS - 1)
    xs = hn2.reshape(n, d)[slot_tok]
    ys = _moe_blocks(blk_expert, xs, w1, w3, w2, layer)
    return ys[dest[:, 0]].reshape(b, l, d), ys[dest[:, 1]].reshape(b, l, d)


def _trunk(x, mods, gdn_s, gdn_conv, lru_h, lru_conv, wp):
    b, l, d = x.shape
    tm = min(ROW_TILE, l)
    chunk = min(GDN_CHUNK, l)
    assert l % tm == 0 and tm % chunk == 0 and chunk & (chunk - 1) == 0
    mod = mods[0]
    q, k, v, z, gbc, gbr, gconv_new = _gdn_in(x, mod, wp['norm_mix0'], wp['gdn_wqkv'], wp['gdn_wz'], wp['gdn_wab'],
                                              wp['gdn_conv_w'], gdn_conv, wp['gdn_alog'], wp['gdn_dtb'], tm, chunk)
    o, s_new = _gdn_chunks(q, k, v, gbc, gbr, gdn_s, tm, chunk)
    x1, hn2, route, counts = _gdn_out(o, z, x, mod, wp['gdn_norm'], wp['gdn_wout'], wp['norm_ffn0'],
                                      wp['wr_hi0'], wp['wr_lo0'], wp['br0'], tm)
    ya, yb = _moe_experts(hn2, route, counts, wp['moe_w1'], wp['moe_w3'], wp['moe_w2'], 0)
    x1, hn2, route, lconv_new, h_new, counts = _lru_layer(
        x1, ya, yb, route, mod, mods[1], wp['norm_mix1'], wp['lru_win'], wp['lru_conv_w'], wp['lru_conv_b'],
        wp['lru_wax'], wp['lru_ba'], wp['lru_bx'], wp['lru_lam'], wp['lru_wout'], lru_conv, lru_h, wp['norm_ffn1'],
        wp['wr_hi1'], wp['wr_lo1'], wp['br1'], tm)
    ya, yb = _moe_experts(hn2, route, counts, wp['moe_w1'], wp['moe_w3'], wp['moe_w2'], 1)
    y = _final(x1, ya, yb, route, mods[1], wp['norm_out'], tm)
    return y, s_new[None], gconv_new[None], h_new.reshape(1, b, LRU_WIDTH), lconv_new[None]


def _pad_lanes(v, width=LANES):
    v = v.reshape(1, -1)
    return jnp.pad(v, ((0, 0), (0, width - v.shape[1])))


def _router_weights(w_rg, b_rg, w_re, b_re):
    w = jnp.pad(jnp.concatenate([w_rg, w_re], axis=1), ((0, 0), (0, LANES - N_GROUPS - N_EXPERTS)))
    hi = w.astype(BF16)
    lo = (w - hi.astype(F32)).astype(BF16)
    return hi, lo, _pad_lanes(jnp.concatenate([b_rg, b_re]))


def kernel(x_prompt, x_sample, state_gdn_S, state_gdn_conv, state_lru_h, state_lru_conv, c_prompt, c_sample, w_ada, b_ada, norm_mix, norm_ffn, norm_out, gdn_w_in, gdn_conv_w, gdn_a_log, gdn_dt_bias, gdn_norm, gdn_w_out, lru_w_in, lru_conv_w, lru_conv_b, lru_w_a, lru_b_a, lru_w_x, lru_b_x, lru_lambda, lru_w_out, moe_w_rg, moe_b_rg, moe_w_re, moe_b_re, moe_w1, moe_w3, moe_w2):
    d = D_MODEL
    bp = x_prompt.shape[0]
    bs = x_sample.shape[0]
    win = gdn_w_in[0]
    wab = jnp.pad(win[:, GDN_CONV_CH + GDN_V_W:], ((0, 0), (0, LANES - 2 * GDN_HEADS)))
    wp = dict(
        norm_mix0=norm_mix[0].reshape(1, d), norm_mix1=norm_mix[1].reshape(1, d),
        norm_ffn0=norm_ffn[0].reshape(1, d), norm_ffn1=norm_ffn[1].reshape(1, d),
        norm_out=norm_out.reshape(1, d),
        gdn_wqkv=win[:, :GDN_CONV_CH].astype(BF16),
        gdn_wz=win[:, GDN_CONV_CH:GDN_CONV_CH + GDN_V_W].astype(BF16),
        gdn_wab=wab.astype(BF16),
        gdn_conv_w=gdn_conv_w[0],
        gdn_alog=_pad_lanes(gdn_a_log[0]), gdn_dtb=_pad_lanes(gdn_dt_bias[0]),
        gdn_norm=gdn_norm[0].reshape(1, GDN_DV), gdn_wout=gdn_w_out[0].astype(BF16),
        lru_win=lru_w_in[0].astype(BF16), lru_conv_w=lru_conv_w[0], lru_conv_b=lru_conv_b[0].reshape(1, d),
        lru_wax=jnp.concatenate([lru_w_a[0], lru_w_x[0]], axis=-1).astype(BF16),
        lru_ba=lru_b_a[0].reshape(1, d), lru_bx=lru_b_x[0].reshape(1, d), lru_lam=lru_lambda[0].reshape(1, d),
        lru_wout=lru_w_out[0].astype(BF16),
        moe_w1=moe_w1, moe_w3=moe_w3, moe_w2=moe_w2,
    )
    for i in range(DEPTH):
        wp[f'wr_hi{i}'], wp[f'wr_lo{i}'], wp[f'br{i}'] = _router_weights(moe_w_rg[i], moe_b_rg[i], moe_w_re[i], moe_b_re[i])

    mods = _ada_mod(jnp.concatenate([c_prompt, c_sample], axis=0), w_ada, b_ada)
    mods = mods.reshape(DEPTH, bp + bs, 6, d)
    mods_p = [mods[i, :bp] for i in range(DEPTH)]
    mods_s = [mods[i, bp:] for i in range(DEPTH)]

    dt = x_prompt.dtype
    z_s = jnp.zeros((bp, GDN_HEADS, GDN_DK, GDN_DV), dt)
    z_gc = jnp.zeros((bp, CONV_W - 1, GDN_CONV_CH), dt)
    z_h = jnp.zeros((bp, 1, LRU_WIDTH), dt)
    z_lc = jnp.zeros((bp, CONV_W - 1, LRU_WIDTH), dt)
    y_p, gs_p, gc_p, lh_p, lc_p = _trunk(x_prompt, mods_p, z_s, z_gc, z_h, z_lc, wp)
    y_s, gs_s, gc_s, lh_s, lc_s = _trunk(x_sample, mods_s, state_gdn_S[0], state_gdn_conv[0],
                                         state_lru_h[0].reshape(bs, 1, LRU_WIDTH), state_lru_conv[0], wp)
    return (y_p, y_s, gs_p, gc_p, lh_p, lc_p, gs_s, gc_s, lh_s, lc_s)
```

```python
import functools
import math

import jax
import jax.numpy as jnp
from jax import lax
from jax.experimental import pallas as pl
from jax.experimental.pallas import tpu as pltpu

F32 = jnp.float32
BF16 = jnp.bfloat16

D_MODEL = 1024
DEPTH = 2
CONV_W = 4
NORM_EPS = 1e-6
GDN_HEADS = 8
GDN_DK = 128
GDN_DV = 128
GDN_QK_W = GDN_HEADS * GDN_DK
GDN_V_W = GDN_HEADS * GDN_DV
GDN_CONV_CH = 2 * GDN_QK_W + GDN_V_W
GDN_CHUNK = 64
GDN_INV_BLOCK = 16
LRU_WIDTH = D_MODEL
LRU_BLOCKS = 8
LRU_BLOCK_W = LRU_WIDTH // LRU_BLOCKS
LRU_C = 8.0
N_GROUPS = 4
EXPERTS_PER_GROUP = 8
N_EXPERTS = N_GROUPS * EXPERTS_PER_GROUP
EXPERT_TOPK = 2
D_FF_EXPERT = 512
MOE_BLOCK = 256
LANES = 128
SUBLANES = 8
ROW_TILE = 256
VMEM_LIMIT = 56 * 1024 * 1024


def _dot(a, b):
    return jnp.dot(a, b, preferred_element_type=F32)


def _dot_nt(a, b):
    return lax.dot_general(a, b, (((1,), (1,)), ((), ())), preferred_element_type=F32)


def _dot_tn(a, b):
    return lax.dot_general(a, b, (((0,), (0,)), ((), ())), preferred_element_type=F32)


def _split3(x):
    a = x.astype(BF16)
    r = x - a.astype(F32)
    b = r.astype(BF16)
    c = (r - b.astype(F32)).astype(BF16)
    return a, b, c


def _rms(x):
    return x * lax.rsqrt(jnp.mean(x * x, axis=-1, keepdims=True) + NORM_EPS)


def _sigmoid(x):
    return 1.0 / (1.0 + jnp.exp(-x))


def _silu(x):
    return x * _sigmoid(x)


def _softplus(x):
    return jnp.maximum(x, 0.0) + jnp.log1p(jnp.exp(-jnp.abs(x)))


def _expm1(x):
    u = jnp.exp(x)
    near = (u > 0.5) & (u < 2.0) & (u != 1.0)
    corrected = (u - 1.0) * x / jnp.where(near, jnp.log(u), 1.0)
    return jnp.where(u == 1.0, x, jnp.where(near, corrected, u - 1.0))


def _params(*sem):
    return pltpu.CompilerParams(dimension_semantics=sem, vmem_limit_bytes=VMEM_LIMIT)


def _const_spec(shape):
    nd = len(shape)
    return pl.BlockSpec(shape, lambda *_: (0,) * nd)


def _ada_kernel(c_ref, w_ref, b_ref, o_ref):
    s = _silu(c_ref[...]).astype(BF16)
    o_ref[0] = _dot(s, w_ref[0].astype(BF16)) + b_ref[0]


def _ada_mod(c_all, w_ada, b_ada):
    bt = c_all.shape[0]
    tn = 1536
    n6 = 6 * D_MODEL
    return pl.pallas_call(
        _ada_kernel,
        grid=(DEPTH, n6 // tn),
        in_specs=[
            pl.BlockSpec((bt, D_MODEL), lambda i, j: (0, 0)),
            pl.BlockSpec((1, D_MODEL, tn), lambda i, j: (i, 0, j)),
            pl.BlockSpec((1, 1, tn), lambda i, j: (i, 0, j)),
        ],
        out_specs=pl.BlockSpec((1, bt, tn), lambda i, j: (i, 0, j)),
        out_shape=jax.ShapeDtypeStruct((DEPTH, bt, n6), F32),
        compiler_params=_params("arbitrary", "arbitrary"),
        name="ada_mod",
    )(c_all, w_ada, b_ada.reshape(DEPTH, 1, n6))


def _route(hn2, wr_hi, wr_lo, br, cnt_ref):
    h_hi = hn2.astype(BF16)
    h_lo = (hn2 - h_hi.astype(F32)).astype(BF16)
    logits = _dot(h_hi, wr_hi) + (_dot(h_hi, wr_lo) + _dot(h_lo, wr_hi)) + br
    lane = lax.broadcasted_iota(jnp.int32, logits.shape, 1)
    neg = jnp.float32(-jnp.inf)
    big = jnp.int32(1 << 20)
    is_g = lane < N_GROUPS
    lg = jnp.where(is_g, logits, neg)
    eg = jnp.exp(lg - jnp.max(lg, axis=-1, keepdims=True))
    pg = eg / jnp.sum(eg, axis=-1, keepdims=True)
    pg = jnp.where(is_g, pg, -1.0)
    gate_g = jnp.max(pg, axis=-1, keepdims=True)
    grp = jnp.min(jnp.where(pg == gate_g, lane, big), axis=-1, keepdims=True)
    lo = N_GROUPS + grp * EXPERTS_PER_GROUP
    is_e = (lane >= lo) & (lane < lo + EXPERTS_PER_GROUP)
    le = jnp.where(is_e, logits, neg)
    ee = jnp.exp(le - jnp.max(le, axis=-1, keepdims=True))
    pe = ee / jnp.sum(ee, axis=-1, keepdims=True)
    pe = jnp.where(is_e, pe, -1.0)
    p1 = jnp.max(pe, axis=-1, keepdims=True)
    i1 = jnp.min(jnp.where(pe == p1, lane, big), axis=-1, keepdims=True)
    pe2 = jnp.where(lane == i1, -1.0, pe)
    p2 = jnp.max(pe2, axis=-1, keepdims=True)
    i2 = jnp.min(jnp.where(pe2 == p2, lane, big), axis=-1, keepdims=True)
    tot = p1 + p2
    w1 = gate_g * (p1 / tot)
    w2 = gate_g * (p2 / tot)
    e1 = i1 - N_GROUPS
    e2 = i2 - N_GROUPS
    tm = logits.shape[0]
    oh1 = lane == e1
    oh2 = lane == e2
    cnt = jnp.where(oh1 | oh2, 1.0, 0.0)
    ri = lax.broadcasted_iota(jnp.int32, (tm, tm), 0)
    ci = lax.broadcasted_iota(jnp.int32, (tm, tm), 1)
    before = jnp.where(ci < ri, 1.0, 0.0).astype(BF16)
    pos = _dot(before, cnt.astype(BF16)) + cnt_ref[...]
    r1 = jnp.sum(jnp.where(oh1, pos, 0.0), axis=-1, keepdims=True)
    r2 = jnp.sum(jnp.where(oh2, pos, 0.0), axis=-1, keepdims=True)
    cnt_ref[...] = cnt_ref[...] + jnp.sum(cnt, axis=0, keepdims=True)
    vals = (e1.astype(F32), e2.astype(F32), w1, w2, r1, r2)
    out = jnp.zeros(logits.shape, F32)
    for j, val in enumerate(vals):
        out = jnp.where(lane == j, val, out)
    return out


def _init_counts(cnt_ref):
    @pl.when((pl.program_id(0) == 0) & (pl.program_id(1) == 0))
    def _():
        cnt_ref[...] = jnp.zeros(cnt_ref.shape, F32)


def _post_mixer(x, y, mod, nf, wr_hi, wr_lo, br, cnt_ref):
    g1 = mod[2:3]
    sh2 = mod[3:4]
    sc2 = mod[4:5]
    x1 = x + g1 * y
    hn2 = (_rms(x1) * nf) * (1.0 + sc2) + sh2
    return x1, hn2, _route(hn2, wr_hi, wr_lo, br, cnt_ref)


def _gdn_in_kernel(x_ref, mod_ref, ng_ref, wqkv_ref, wz_ref, wab_ref, cw_ref, hist0_ref, alog_ref, dtb_ref,
                   q_ref, k_ref, v_ref, z_ref, gbc_ref, gbr_ref, hist_ref, xp_ref, *, tm, chunk):
    l = pl.program_id(1)

    @pl.when(l == 0)
    def _():
        xp_ref[0:8, :] = jnp.zeros((8, GDN_CONV_CH), F32)
        xp_ref[5:8, :] = hist0_ref[0]

    x = x_ref[0]
    mod = mod_ref[0]
    hn = (_rms(x) * ng_ref[...]) * (1.0 + mod[1:2]) + mod[0:1]
    hb = hn.astype(BF16)
    qkv = _dot(hb, wqkv_ref[...])
    xp_ref[8:8 + tm, :] = qkv
    cw = cw_ref[...]
    y = cw[3:4] * qkv
    for j in range(CONV_W - 1):
        y = y + cw[j:j + 1] * xp_ref[5 + j:5 + j + tm, :]
    new_hist = xp_ref[tm + 5:tm + 8, :]
    xp_ref[5:8, :] = new_hist
    hist_ref[0] = new_hist
    y = _silu(y)
    for h in range(GDN_HEADS):
        s = slice(h * GDN_DK, (h + 1) * GDN_DK)
        qh = y[:, s]
        q_ref[0, :, s] = qh * lax.rsqrt(jnp.sum(qh * qh, axis=-1, keepdims=True) + 1e-6) * (GDN_DK ** -0.5)
        kh = y[:, GDN_QK_W + h * GDN_DK:GDN_QK_W + (h + 1) * GDN_DK]
        k_ref[0, :, s] = kh * lax.rsqrt(jnp.sum(kh * kh, axis=-1, keepdims=True) + 1e-6)
    v_ref[0] = y[:, 2 * GDN_QK_W:]
    z_ref[0] = _dot(hb, wz_ref[...])
    ab = _dot(hb, wab_ref[...])
    g = -jnp.exp(alog_ref[...]) * _softplus(ab + dtb_ref[...])
    beta = _sigmoid(ab)
    ri = lax.broadcasted_iota(jnp.int32, (tm, tm), 0)
    ci = lax.broadcasted_iota(jnp.int32, (tm, tm), 1)
    tri = jnp.where((ri // chunk == ci // chunk) & (ci <= ri), 1.0, 0.0).astype(BF16)
    g1, g2, g3 = _split3(g)
    gcum = (_dot(tri, g1) + _dot(tri, g2)) + _dot(tri, g3)
    lane = lax.broadcasted_iota(jnp.int32, (tm, LANES), 1)
    gb = jnp.where(lane < GDN_HEADS, gcum, beta)
    gbc_ref[0] = gb
    er = lax.broadcasted_iota(jnp.int32, (2 * GDN_HEADS, LANES), 0)
    ec = lax.broadcasted_iota(jnp.int32, (2 * GDN_HEADS, LANES), 1)
    sel = jnp.where(er == ec, 1.0, 0.0).astype(BF16)
    b1, b2, b3 = _split3(gb)
    for n in range(tm // chunk):
        r = slice(n * chunk, (n + 1) * chunk)
        gbr_ref[0, n] = (_dot_nt(sel, b1[r]) + _dot_nt(sel, b2[r])) + _dot_nt(sel, b3[r])


def _gdn_in(x, mod, ng, wqkv, wz, wab, cw, hist0, alog, dtb, tm, chunk):
    b, l, d = x.shape
    grid = (b, l // tm)
    row = lambda shape: pl.BlockSpec(shape, lambda i, j: (i, j, 0))
    outs = pl.pallas_call(
        functools.partial(_gdn_in_kernel, tm=tm, chunk=chunk),
        grid=grid,
        in_specs=[
            row((1, tm, d)),
            pl.BlockSpec((1, 6, d), lambda i, j: (i, 0, 0)),
            _const_spec((1, d)),
            _const_spec(wqkv.shape),
            _const_spec(wz.shape),
            _const_spec(wab.shape),
            _const_spec(cw.shape),
            pl.BlockSpec((1, CONV_W - 1, GDN_CONV_CH), lambda i, j: (i, 0, 0)),
            _const_spec((1, LANES)),
            _const_spec((1, LANES)),
        ],
        out_specs=[
            row((1, tm, GDN_QK_W)), row((1, tm, GDN_QK_W)), row((1, tm, GDN_V_W)), row((1, tm, GDN_V_W)),
            row((1, tm, LANES)),
            pl.BlockSpec((1, tm // chunk, 2 * GDN_HEADS, chunk), lambda i, j: (i, j, 0, 0)),
            pl.BlockSpec((1, CONV_W - 1, GDN_CONV_CH), lambda i, j: (i, 0, 0)),
        ],
        out_shape=[
            jax.ShapeDtypeStruct((b, l, GDN_QK_W), F32), jax.ShapeDtypeStruct((b, l, GDN_QK_W), F32),
            jax.ShapeDtypeStruct((b, l, GDN_V_W), F32), jax.ShapeDtypeStruct((b, l, GDN_V_W), F32),
            jax.ShapeDtypeStruct((b, l, LANES), F32),
            jax.ShapeDtypeStruct((b, l // chunk, 2 * GDN_HEADS, chunk), F32),
            jax.ShapeDtypeStruct((b, CONV_W - 1, GDN_CONV_CH), F32),
        ],
        scratch_shapes=[pltpu.VMEM((tm + 8, GDN_CONV_CH), F32)],
        compiler_params=_params("arbitrary", "arbitrary"),
        name="gdn_in",
    )(x, mod, ng, wqkv, wz, wab, cw, hist0, alog, dtb)
    return outs


def _gdn_chunk_kernel(q_ref, k_ref, v_ref, gbc_ref, gbr_ref, s0_ref, o_ref, sout_ref, s_ref, *, tm, chunk):
    l = pl.program_id(1)

    @pl.when(l == 0)
    def _():
        s_ref[...] = s0_ref[0]

    c2 = 2 * chunk
    n_pairs = GDN_HEADS // 2
    ri = lax.broadcasted_iota(jnp.int32, (chunk, c2), 0)
    cl = lax.broadcasted_iota(jnp.int32, (chunk, c2), 1)
    ci = cl % chunk
    left = cl < chunk
    incl = ri >= ci
    strict = ri > ci
    eye = jnp.where(ri == ci, 1.0, 0.0).astype(F32)
    base = min(GDN_INV_BLOCK, chunk)
    levels = int(math.log2(base))
    diag_blk = (ri // base) == (ci // base)
    merge_masks = []
    blk = base
    while blk < chunk:
        merge_masks.append(((ri // (2 * blk)) == (ci // (2 * blk))) & ((ri // blk) % 2 == 1) & ((ci // blk) % 2 == 0))
        blk *= 2
    heads = range(GDN_HEADS)
    pairs = range(n_pairs)

    def bdiag(x):
        return jnp.concatenate([jnp.where(left, x, 0.0), jnp.where(left, 0.0, x)], axis=0).astype(BF16)

    def bdiag2(x0, x1):
        z = jnp.zeros_like(x0)
        return jnp.concatenate([jnp.concatenate([x0, z], axis=1), jnp.concatenate([z, x1], axis=1)], axis=0).astype(BF16)

    def one_chunk(n, carry):
        r0 = pl.multiple_of(n * chunk, chunk)
        rows = pl.ds(r0, chunk)
        gbc = gbc_ref[0, rows, :]
        gbr = gbr_ref[0, n]
        hs = [slice(h * GDN_DK, (h + 1) * GDN_DK) for h in heads]
        q = [q_ref[0, rows, hs[h]] for h in heads]
        k = [k_ref[0, rows, hs[h]] for h in heads]
        v = [v_ref[0, rows, hs[h]] for h in heads]
        gc = [gbc[:, h:h + 1] for h in heads]
        bc = [gbc[:, GDN_HEADS + h:GDN_HEADS + h + 1] for h in heads]
        kb = [k[h] * bc[h] for h in heads]
        a2 = [_dot_nt(jnp.concatenate([q[h], kb[h]], axis=0).astype(BF16), k[h].astype(BF16)) for h in heads]
        attn, m = [], []
        for hp in pairs:
            h0, h1 = 2 * hp, 2 * hp + 1
            gcp = jnp.where(left, gc[h0], gc[h1])
            grp = jnp.concatenate([gbr[h0:h0 + 1, :], gbr[h1:h1 + 1, :]], axis=1)
            decay = jnp.where(incl, jnp.exp(jnp.where(incl, gcp - grp, 0.0)), 0.0)
            ap = jnp.concatenate([a2[h0], a2[h1]], axis=1)
            attn.append(ap[:chunk] * decay)
            m.append(jnp.where(strict, ap[chunk:] * decay, 0.0))
        nq = [jnp.where(diag_blk, -m[hp], 0.0) for hp in pairs]
        p = [eye + nq[hp] for hp in pairs]
        nq = [_dot(nq[hp].astype(BF16), bdiag(nq[hp])) for hp in pairs]
        for j in range(1, levels):
            if j < levels - 1:
                r = [_dot(nq[hp].astype(BF16), jnp.concatenate([bdiag(nq[hp]), bdiag(p[hp])], axis=1)) for hp in pairs]
                nq = [r[hp][:, :c2] for hp in pairs]
                p = [p[hp] + r[hp][:, c2:] for hp in pairs]
            else:
                p = [p[hp] + _dot(nq[hp].astype(BF16), bdiag(p[hp])) for hp in pairs]
        for low in merge_masks:
            t = [_dot(jnp.where(low, m[hp], 0.0).astype(BF16), bdiag(p[hp])) for hp in pairs]
            p = [p[hp] - _dot(p[hp].astype(BF16), bdiag(t[hp])) for hp in pairs]
        eg = [jnp.exp(gc[h]) for h in heads]
        rhs = [jnp.concatenate([kb[h] * eg[h], v[h] * bc[h]], axis=1) for h in heads]
        wu = [_dot(p[hp].astype(BF16), bdiag2(rhs[2 * hp], rhs[2 * hp + 1])) for hp in pairs]
        wd = GDN_DK + GDN_DV
        w = [wu[h // 2][:, (h % 2) * wd:(h % 2) * wd + GDN_DK] for h in heads]
        u = [wu[h // 2][:, (h % 2) * wd + GDN_DK:(h % 2 + 1) * wd] for h in heads]
        st = [s_ref[h] for h in heads]
        ws = [_dot(jnp.concatenate([w[h], q[h] * eg[h]], axis=0).astype(BF16), st[h].astype(BF16)) for h in heads]
        v_new = [u[h] - ws[h][:chunk] for h in heads]
        av = [_dot(attn[hp].astype(BF16), bdiag2(v_new[2 * hp], v_new[2 * hp + 1])) for hp in pairs]
        for h in heads:
            o_ref[0, rows, hs[h]] = ws[h][chunk:] + av[h // 2][:, (h % 2) * GDN_DV:(h % 2 + 1) * GDN_DV]
        for h in heads:
            g_last = gc[h][chunk - 1:chunk, :]
            kd = k[h] * jnp.exp(g_last - gc[h])
            s_ref[h] = st[h] * jnp.exp(g_last) + _dot_tn(kd.astype(BF16), v_new[h].astype(BF16))
        return carry

    lax.fori_loop(0, tm // chunk, one_chunk, 0)
    sout_ref[0] = s_ref[...]


def _gdn_chunks(q, k, v, gbc, gbr, s0, tm, chunk):
    b, l, _ = q.shape
    row = lambda shape: pl.BlockSpec(shape, lambda i, j: (i, j, 0))
    sspec = pl.BlockSpec((1, GDN_HEADS, GDN_DK, GDN_DV), lambda i, j: (i, 0, 0, 0))
    return pl.pallas_call(
        functools.partial(_gdn_chunk_kernel, tm=tm, chunk=chunk),
        grid=(b, l // tm),
        in_specs=[
            row((1, tm, GDN_QK_W)), row((1, tm, GDN_QK_W)), row((1, tm, GDN_V_W)), row((1, tm, LANES)),
            pl.BlockSpec((1, tm // chunk, 2 * GDN_HEADS, chunk), lambda i, j: (i, j, 0, 0)),
            sspec,
        ],
        out_specs=[row((1, tm, GDN_V_W)), sspec],
        out_shape=[jax.ShapeDtypeStruct((b, l, GDN_V_W), F32),
                   jax.ShapeDtypeStruct((b, GDN_HEADS, GDN_DK, GDN_DV), F32)],
        scratch_shapes=[pltpu.VMEM((GDN_HEADS, GDN_DK, GDN_DV), F32)],
        compiler_params=_params("arbitrary", "arbitrary"),
        name="gdn_chunks",
    )(q, k, v, gbc, gbr, s0)


def _gdn_out_kernel(o_ref, z_ref, x_ref, mod_ref, gn_ref, wout_ref, nf_ref, wrh_ref, wrl_ref, br_ref,
                    x1_ref, hn2_ref, route_ref, cnt_ref):
    _init_counts(cnt_ref)
    o = o_ref[0]
    z = z_ref[0]
    gn = gn_ref[...]
    parts = []
    for h in range(GDN_HEADS):
        s = slice(h * GDN_DV, (h + 1) * GDN_DV)
        parts.append((_rms(o[:, s]) * gn) * _silu(z[:, s]))
    on = jnp.concatenate(parts, axis=1).astype(BF16)
    y = _dot(on, wout_ref[...])
    x1, hn2, route = _post_mixer(x_ref[0], y, mod_ref[0], nf_ref[...], wrh_ref[...], wrl_ref[...], br_ref[...],
                                 cnt_ref)
    x1_ref[0] = x1
    hn2_ref[0] = hn2
    route_ref[0] = route


def _gdn_out(o, z, x, mod, gn, wout, nf, wrh, wrl, br, tm):
    b, l, d = x.shape
    row = lambda shape: pl.BlockSpec(shape, lambda i, j: (i, j, 0))
    return pl.pallas_call(
        _gdn_out_kernel,
        grid=(b, l // tm),
        in_specs=[
            row((1, tm, GDN_V_W)), row((1, tm, GDN_V_W)), row((1, tm, d)),
            pl.BlockSpec((1, 6, d), lambda i, j: (i, 0, 0)),
            _const_spec((1, GDN_DV)), _const_spec(wout.shape), _const_spec((1, d)),
            _const_spec(wrh.shape), _const_spec(wrl.shape), _const_spec((1, LANES)),
        ],
        out_specs=[row((1, tm, d)), row((1, tm, d)), row((1, tm, LANES)), _const_spec((1, LANES))],
        out_shape=[jax.ShapeDtypeStruct((b, l, d), F32), jax.ShapeDtypeStruct((b, l, d), F32),
                   jax.ShapeDtypeStruct((b, l, LANES), F32), jax.ShapeDtypeStruct((1, LANES), F32)],
        compiler_params=_params("arbitrary", "arbitrary"),
        name="gdn_out",
    )(o, z, x, mod, gn, wout, nf, wrh, wrl, br)


def _gelu_tanh(x):
    return 0.5 * x * (1.0 + jnp.tanh(math.sqrt(2.0 / math.pi) * (x + 0.044715 * (x * x * x))))


def _lru_kernel(xa_ref, ya_ref, yb_ref, rprev_ref, mprev_ref,
                mod_ref, ng_ref, win_ref, cw_ref, cb_ref, wax_ref, ba_ref, bx_ref, lam_ref, wout_ref,
                hist0_ref, h0_ref, nf_ref, wrh_ref, wrl_ref, br_ref,
                x1_ref, hn2_ref, route_ref, hist_ref, hlast_ref, cnt_ref, xp_ref, h_ref, *, tm):
    l = pl.program_id(1)
    _init_counts(cnt_ref)

    @pl.when(l == 0)
    def _():
        xp_ref[0:8, :] = jnp.zeros((8, LRU_WIDTH), F32)
        xp_ref[5:8, :] = hist0_ref[0]
        h_ref[...] = h0_ref[0]

    x = _moe_residual(xa_ref[0], ya_ref[0], yb_ref[0], rprev_ref[0], mprev_ref[0])
    mod = mod_ref[0]
    hn = (_rms(x) * ng_ref[...]) * (1.0 + mod[1:2]) + mod[0:1]
    proj = _dot(hn.astype(BF16), win_ref[...])
    gate_br = _gelu_tanh(proj[:, :LRU_WIDTH])
    xb = proj[:, LRU_WIDTH:]
    xp_ref[8:8 + tm, :] = xb
    cw = cw_ref[...]
    xc = cw[3:4] * xb
    for j in range(CONV_W - 1):
        xc = xc + cw[j:j + 1] * xp_ref[5 + j:5 + j + tm, :]
    xc = xc + cb_ref[...]
    new_hist = xp_ref[tm + 5:tm + 8, :]
    xp_ref[5:8, :] = new_hist
    hist_ref[0] = new_hist
    xcb = xc.astype(BF16)
    ra, ia = [], []
    for h in range(LRU_BLOCKS):
        s = slice(h * LRU_BLOCK_W, (h + 1) * LRU_BLOCK_W)
        r2 = _dot(xcb[:, s], wax_ref[h])
        ra.append(r2[:, :LRU_BLOCK_W])
        ia.append(r2[:, LRU_BLOCK_W:])
    r = _sigmoid(jnp.concatenate(ra, axis=1) + ba_ref[...])
    i = _sigmoid(jnp.concatenate(ia, axis=1) + bx_ref[...])
    log_a = (-LRU_C * r) * _softplus(-lam_ref[...])
    a = jnp.exp(log_a)
    mult = jnp.sqrt(-_expm1(2.0 * log_a))
    b = mult * (i * xc)
    sub = lax.broadcasted_iota(jnp.int32, (tm, LRU_WIDTH), 0) % SUBLANES
    sft = 1
    while sft < SUBLANES:
        keep = sub >= sft
        a_prev = jnp.where(keep, pltpu.roll(a, sft, 0), 1.0)
        b_prev = jnp.where(keep, pltpu.roll(b, sft, 0), 0.0)
        b = a * b_prev + b
        a = a * a_prev
        sft *= 2
    h = h_ref[...]
    groups = []
    for g in range(tm // SUBLANES):
        rows = slice(g * SUBLANES, (g + 1) * SUBLANES)
        hg = b[rows] + a[rows] * h
        groups.append(hg)
        h = hg[SUBLANES - 1:SUBLANES, :]
    hs = jnp.concatenate(groups, axis=0)
    h_last = h
    h_ref[...] = h_last
    hlast_ref[0] = h_last
    y = _dot((hs * gate_br).astype(BF16), wout_ref[...])
    x1, hn2, route = _post_mixer(x, y, mod, nf_ref[...], wrh_ref[...], wrl_ref[...], br_ref[...], cnt_ref)
    x1_ref[0] = x1
    hn2_ref[0] = hn2
    route_ref[0] = route


def _lru_layer(xa, ya, yb, rprev, mprev, mod, ng, win, cw, cb, wax, ba, bx, lam, wout, hist0, h0, nf, wrh, wrl, br, tm):
    b, l, d = xa.shape
    row = lambda shape: pl.BlockSpec(shape, lambda i, j: (i, j, 0))
    per_b = lambda shape: pl.BlockSpec(shape, lambda i, j: (i, 0, 0))
    vec = _const_spec((1, d))
    return pl.pallas_call(
        functools.partial(_lru_kernel, tm=tm),
        grid=(b, l // tm),
        in_specs=[
            row((1, tm, d)), row((1, tm, d)), row((1, tm, d)), row((1, tm, LANES)), per_b((1, 6, d)),
            per_b((1, 6, d)), vec, _const_spec(win.shape), _const_spec(cw.shape), vec,
            _const_spec(wax.shape), vec, vec, vec, _const_spec(wout.shape),
            per_b((1, CONV_W - 1, LRU_WIDTH)), per_b((1, 1, LRU_WIDTH)), vec,
            _const_spec(wrh.shape), _const_spec(wrl.shape), _const_spec((1, LANES)),
        ],
        out_specs=[row((1, tm, d)), row((1, tm, d)), row((1, tm, LANES)),
                   per_b((1, CONV_W - 1, LRU_WIDTH)), per_b((1, 1, LRU_WIDTH)), _const_spec((1, LANES))],
        out_shape=[jax.ShapeDtypeStruct((b, l, d), F32), jax.ShapeDtypeStruct((b, l, d), F32),
                   jax.ShapeDtypeStruct((b, l, LANES), F32),
                   jax.ShapeDtypeStruct((b, CONV_W - 1, LRU_WIDTH), F32),
                   jax.ShapeDtypeStruct((b, 1, LRU_WIDTH), F32), jax.ShapeDtypeStruct((1, LANES), F32)],
        scratch_shapes=[pltpu.VMEM((tm + 8, LRU_WIDTH), F32), pltpu.VMEM((1, LRU_WIDTH), F32)],
        compiler_params=_params("arbitrary", "arbitrary"),
        name="lru_layer",
    )(xa, ya, yb, rprev, mprev, mod, ng, win, cw, cb, wax, ba, bx, lam, wout, hist0, h0, nf, wrh, wrl, br)


def _moe_kernel(be_ref, st_ref, x_hbm, w1_ref, w3_ref, w2_ref, y_ref, w1b_ref, w3b_ref, w2b_ref, xbuf_ref, sem_ref):
    i = pl.program_id(0)
    last = pl.num_programs(0) - 1
    slot = i % 2

    def row_copy(blk, r, buf):
        tok = st_ref[blk * MOE_BLOCK + r]
        return pltpu.make_async_copy(x_hbm.at[pl.ds(tok, 1), :], xbuf_ref.at[buf, pl.ds(r, 1), :], sem_ref.at[buf])

    def wait_block(buf):
        pltpu.make_async_copy(x_hbm.at[pl.ds(0, MOE_BLOCK), :], xbuf_ref.at[buf], sem_ref.at[buf]).wait()

    @pl.when(i == 0)
    def _():
        for r in range(MOE_BLOCK):
            row_copy(0, r, 0).start()

    prev = be_ref[jnp.maximum(i - 1, 0)]

    @pl.when((i == 0) | (be_ref[i] != prev))
    def _():
        w1b_ref[...] = w1_ref[0, 0].astype(BF16)
        w3b_ref[...] = w3_ref[0, 0].astype(BF16)
        w2b_ref[...] = w2_ref[0, 0].astype(BF16)

    nxt = jnp.minimum(i + 1, last)

    def body(buf):
        wait_block(buf)
        n_piece = 6
        per = MOE_BLOCK // n_piece + 1
        issued = [0]

        def issue_some():
            hi = min(issued[0] + per, MOE_BLOCK)
            for r in range(issued[0], hi):
                row_copy(nxt, r, 1 - buf).start()
            issued[0] = hi

        xb = xbuf_ref[buf].astype(BF16)
        half = D_FF_EXPERT // 2
        hid = []
        for c in range(2):
            cols = slice(c * half, (c + 1) * half)
            issue_some()
            h1 = _dot(xb, w1b_ref[:, cols])
            issue_some()
            h3 = _dot(xb, w3b_ref[:, cols])
            hid.append((_silu(h1) * h3).astype(BF16))
        issue_some()
        y = _dot(hid[0], w2b_ref[0:half, :])
        issue_some()
        y_ref[...] = y + _dot(hid[1], w2b_ref[half:, :])
        assert issued[0] == MOE_BLOCK

        @pl.when(i == last)
        def _():
            wait_block(1 - buf)

    for buf in range(2):
        pl.when(slot == buf)(functools.partial(body, buf))


def _moe_blocks(blk_expert, slot_tok, x, w1, w3, w2, layer):
    n, d = x.shape
    p = slot_tok.shape[0]
    nb = p // MOE_BLOCK
    grid_spec = pltpu.PrefetchScalarGridSpec(
        num_scalar_prefetch=2,
        grid=(nb,),
        in_specs=[
            pl.BlockSpec(memory_space=pl.ANY),
            pl.BlockSpec((1, 1, d, D_FF_EXPERT), lambda i, be, st: (layer, be[i], 0, 0)),
            pl.BlockSpec((1, 1, d, D_FF_EXPERT), lambda i, be, st: (layer, be[i], 0, 0)),
            pl.BlockSpec((1, 1, D_FF_EXPERT, d), lambda i, be, st: (layer, be[i], 0, 0)),
        ],
        out_specs=pl.BlockSpec((MOE_BLOCK, d), lambda i, be, st: (i, 0)),
        scratch_shapes=[pltpu.VMEM((d, D_FF_EXPERT), BF16), pltpu.VMEM((d, D_FF_EXPERT), BF16),
                        pltpu.VMEM((D_FF_EXPERT, d), BF16), pltpu.VMEM((2, MOE_BLOCK, d), F32),
                        pltpu.SemaphoreType.DMA((2,))],
    )
    return pl.pallas_call(
        _moe_kernel,
        grid_spec=grid_spec,
        out_shape=jax.ShapeDtypeStruct((p, d), F32),
        compiler_params=_params("arbitrary"),
        name="moe_blocks",
    )(blk_expert, slot_tok, x, w1, w3, w2)


def _moe_residual(x1, ya, yb, route, mod):
    return x1 + mod[5:6] * (ya * route[:, 2:3] + yb * route[:, 3:4])


def _final_kernel(x1_ref, ya_ref, yb_ref, route_ref, mod_ref, no_ref, o_ref):
    x2 = _moe_residual(x1_ref[0], ya_ref[0], yb_ref[0], route_ref[0], mod_ref[0])
    o_ref[0] = _rms(x2) * no_ref[...]


def _final(x1, ya, yb, route, mod, norm_out, tm):
    b, l, d = x1.shape
    row = lambda shape: pl.BlockSpec(shape, lambda i, j: (i, j, 0))
    return pl.pallas_call(
        _final_kernel,
        grid=(b, l // tm),
        in_specs=[row((1, tm, d)), row((1, tm, d)), row((1, tm, d)), row((1, tm, LANES)),
                  pl.BlockSpec((1, 6, d), lambda i, j: (i, 0, 0)), _const_spec((1, d))],
        out_specs=row((1, tm, d)),
        out_shape=jax.ShapeDtypeStruct((b, l, d), F32),
        compiler_params=_params("arbitrary", "arbitrary"),
        name="final",
    )(x1, ya, yb, route, mod, norm_out)


def _moe_experts(hn2, route, counts, w1, w3, w2, layer):
    b, l, d = hn2.shape
    n = b * l
    a = n * EXPERT_TOPK
    rt = route.reshape(n, LANES)
    e_ab = rt[:, 0:EXPERT_TOPK].astype(jnp.int32)
    r_ab = rt[:, 4:4 + EXPERT_TOPK].astype(jnp.int32)
    counts = counts[0, :N_EXPERTS].astype(jnp.int32)
    padded = (counts + MOE_BLOCK - 1) // MOE_BLOCK * MOE_BLOCK
    pad_end = jnp.cumsum(padded)
    pad_start = pad_end - padded
    dest = pad_start[e_ab] + r_ab
    n_blocks = -(-a // MOE_BLOCK) + N_EXPERTS
    p = n_blocks * MOE_BLOCK
    tok = jnp.broadcast_to(jnp.arange(n, dtype=jnp.int32)[:, None], (n, EXPERT_TOPK))
    slot_tok = jnp.zeros((p,), jnp.int32).at[dest.reshape(-1)].set(tok.reshape(-1))
    blk_first = jnp.arange(n_blocks, dtype=jnp.int32) * MOE_BLOCK
    blk_expert = jnp.minimum(jnp.sum((pad_end[None, :] <= blk_first[:, None]).astype(jnp.int32), axis=1),
                             N_EXPERTS - 1)
    ys = _moe_blocks(blk_expert, slot_tok, hn2.reshape(n, d), w1, w3, w2, layer)
    return ys[dest[:, 0]].reshape(b, l, d), ys[dest[:, 1]].reshape(b, l, d)


def _trunk(x, mods, gdn_s, gdn_conv, lru_h, lru_conv, wp):
    b, l, d = x.shape
    tm = min(ROW_TILE, l)
    chunk = min(GDN_CHUNK, l)
    assert l % tm == 0 and tm % chunk == 0 and chunk & (chunk - 1) == 0
    mod = mods[0]
    q, k, v, z, gbc, gbr, gconv_new = _gdn_in(x, mod, wp['norm_mix0'], wp['gdn_wqkv'], wp['gdn_wz'], wp['gdn_wab'],
                                              wp['gdn_conv_w'], gdn_conv, wp['gdn_alog'], wp['gdn_dtb'], tm, chunk)
    o, s_new = _gdn_chunks(q, k, v, gbc, gbr, gdn_s, tm, chunk)
    x1, hn2, route, counts = _gdn_out(o, z, x, mod, wp['gdn_norm'], wp['gdn_wout'], wp['norm_ffn0'],
                                      wp['wr_hi0'], wp['wr_lo0'], wp['br0'], tm)
    ya, yb = _moe_experts(hn2, route, counts, wp['moe_w1'], wp['moe_w3'], wp['moe_w2'], 0)
    x1, hn2, route, lconv_new, h_new, counts = _lru_layer(
        x1, ya, yb, route, mod, mods[1], wp['norm_mix1'], wp['lru_win'], wp['lru_conv_w'], wp['lru_conv_b'],
        wp['lru_wax'], wp['lru_ba'], wp['lru_bx'], wp['lru_lam'], wp['lru_wout'], lru_conv, lru_h, wp['norm_ffn1'],
        wp['wr_hi1'], wp['wr_lo1'], wp['br1'], tm)
    ya, yb = _moe_experts(hn2, route, counts, wp['moe_w1'], wp['moe_w3'], wp['moe_w2'], 1)
    y = _final(x1, ya, yb, route, mods[1], wp['norm_out'], tm)
    return y, s_new[None], gconv_new[None], h_new.reshape(1, b, LRU_WIDTH), lconv_new[None]


def _pad_lanes(v, width=LANES):
    v = v.reshape(1, -1)
    return jnp.pad(v, ((0, 0), (0, width - v.shape[1])))


def _router_weights(w_rg, b_rg, w_re, b_re):
    w = jnp.pad(jnp.concatenate([w_rg, w_re], axis=1), ((0, 0), (0, LANES - N_GROUPS - N_EXPERTS)))
    hi = w.astype(BF16)
    lo = (w - hi.astype(F32)).astype(BF16)
    return hi, lo, _pad_lanes(jnp.concatenate([b_rg, b_re]))


def kernel(x_prompt, x_sample, state_gdn_S, state_gdn_conv, state_lru_h, state_lru_conv, c_prompt, c_sample, w_ada, b_ada, norm_mix, norm_ffn, norm_out, gdn_w_in, gdn_conv_w, gdn_a_log, gdn_dt_bias, gdn_norm, gdn_w_out, lru_w_in, lru_conv_w, lru_conv_b, lru_w_a, lru_b_a, lru_w_x, lru_b_x, lru_lambda, lru_w_out, moe_w_rg, moe_b_rg, moe_w_re, moe_b_re, moe_w1, moe_w3, moe_w2):
    d = D_MODEL
    bp = x_prompt.shape[0]
    bs = x_sample.shape[0]
    win = gdn_w_in[0]
    wab = jnp.pad(win[:, GDN_CONV_CH + GDN_V_W:], ((0, 0), (0, LANES - 2 * GDN_HEADS)))
    wp = dict(
        norm_mix0=norm_mix[0].reshape(1, d), norm_mix1=norm_mix[1].reshape(1, d),
        norm_ffn0=norm_ffn[0].reshape(1, d), norm_ffn1=norm_ffn[1].reshape(1, d),
        norm_out=norm_out.reshape(1, d),
        gdn_wqkv=win[:, :GDN_CONV_CH].astype(BF16),
        gdn_wz=win[:, GDN_CONV_CH:GDN_CONV_CH + GDN_V_W].astype(BF16),
        gdn_wab=wab.astype(BF16),
        gdn_conv_w=gdn_conv_w[0],
        gdn_alog=_pad_lanes(gdn_a_log[0]), gdn_dtb=_pad_lanes(gdn_dt_bias[0]),
        gdn_norm=gdn_norm[0].reshape(1, GDN_DV), gdn_wout=gdn_w_out[0].astype(BF16),
        lru_win=lru_w_in[0].astype(BF16), lru_conv_w=lru_conv_w[0], lru_conv_b=lru_conv_b[0].reshape(1, d),
        lru_wax=jnp.concatenate([lru_w_a[0], lru_w_x[0]], axis=-1).astype(BF16),
        lru_ba=lru_b_a[0].reshape(1, d), lru_bx=lru_b_x[0].reshape(1, d), lru_lam=lru_lambda[0].reshape(1, d),
        lru_wout=lru_w_out[0].astype(BF16),
        moe_w1=moe_w1, moe_w3=moe_w3, moe_w2=moe_w2,
    )
    for i in range(DEPTH):
        wp[f'wr_hi{i}'], wp[f'wr_lo{i}'], wp[f'br{i}'] = _router_weights(moe_w_rg[i], moe_b_rg[i], moe_w_re[i], moe_b_re[i])

    mods = _ada_mod(jnp.concatenate([c_prompt, c_sample], axis=0), w_ada, b_ada)
    mods = mods.reshape(DEPTH, bp + bs, 6, d)
    mods_p = [mods[i, :bp] for i in range(DEPTH)]
    mods_s = [mods[i, bp:] for i in range(DEPTH)]

    dt = x_prompt.dtype
    z_s = jnp.zeros((bp, GDN_HEADS, GDN_DK, GDN_DV), dt)
    z_gc = jnp.zeros((bp, CONV_W - 1, GDN_CONV_CH), dt)
    z_h = jnp.zeros((bp, 1, LRU_WIDTH), dt)
    z_lc = jnp.zeros((bp, CONV_W - 1, LRU_WIDTH), dt)
    y_p, gs_p, gc_p, lh_p, lc_p = _trunk(x_prompt, mods_p, z_s, z_gc, z_h, z_lc, wp)
    y_s, gs_s, gc_s, lh_s, lc_s = _trunk(x_sample, mods_s, state_gdn_S[0], state_gdn_conv[0],
                                         state_lru_h[0].reshape(bs, 1, LRU_WIDTH), state_lru_conv[0], wp)
    return (y_p, y_s, gs_p, gc_p, lh_p, lc_p, gs_s, gc_s, lh_s, lc_s)
```

```python
import functools
import math

import jax
import jax.numpy as jnp
from jax import lax
from jax.experimental import pallas as pl
from jax.experimental.pallas import tpu as pltpu

F32 = jnp.float32
BF16 = jnp.bfloat16

D_MODEL = 1024
DEPTH = 2
CONV_W = 4
NORM_EPS = 1e-6
GDN_HEADS = 8
GDN_DK = 128
GDN_DV = 128
GDN_QK_W = GDN_HEADS * GDN_DK
GDN_V_W = GDN_HEADS * GDN_DV
GDN_CONV_CH = 2 * GDN_QK_W + GDN_V_W
GDN_CHUNK = 64
GDN_INV_BLOCK = 16
LRU_WIDTH = D_MODEL
LRU_BLOCKS = 8
LRU_BLOCK_W = LRU_WIDTH // LRU_BLOCKS
LRU_C = 8.0
N_GROUPS = 4
EXPERTS_PER_GROUP = 8
N_EXPERTS = N_GROUPS * EXPERTS_PER_GROUP
EXPERT_TOPK = 2
D_FF_EXPERT = 512
MOE_BLOCK = 256
LANES = 128
SUBLANES = 8
ROW_TILE = 256
VMEM_LIMIT = 56 * 1024 * 1024


def _dot(a, b):
    return jnp.dot(a, b, preferred_element_type=F32)


def _dot_nt(a, b):
    return lax.dot_general(a, b, (((1,), (1,)), ((), ())), preferred_element_type=F32)


def _dot_tn(a, b):
    return lax.dot_general(a, b, (((0,), (0,)), ((), ())), preferred_element_type=F32)


def _split3(x):
    a = x.astype(BF16)
    r = x - a.astype(F32)
    b = r.astype(BF16)
    c = (r - b.astype(F32)).astype(BF16)
    return a, b, c


def _rms(x):
    return x * lax.rsqrt(jnp.mean(x * x, axis=-1, keepdims=True) + NORM_EPS)


def _sigmoid(x):
    return 1.0 / (1.0 + jnp.exp(-x))


def _silu(x):
    return x * _sigmoid(x)


def _softplus(x):
    return jnp.maximum(x, 0.0) + jnp.log1p(jnp.exp(-jnp.abs(x)))


def _expm1(x):
    u = jnp.exp(x)
    near = (u > 0.5) & (u < 2.0) & (u != 1.0)
    corrected = (u - 1.0) * x / jnp.where(near, jnp.log(u), 1.0)
    return jnp.where(u == 1.0, x, jnp.where(near, corrected, u - 1.0))


def _params(*sem):
    return pltpu.CompilerParams(dimension_semantics=sem, vmem_limit_bytes=VMEM_LIMIT)


def _const_spec(shape):
    nd = len(shape)
    return pl.BlockSpec(shape, lambda *_: (0,) * nd)


def _ada_kernel(c_ref, w_ref, b_ref, o_ref):
    s = _silu(c_ref[...]).astype(BF16)
    o_ref[0] = _dot(s, w_ref[0].astype(BF16)) + b_ref[0]


def _ada_mod(c_all, w_ada, b_ada):
    bt = c_all.shape[0]
    tn = 1536
    n6 = 6 * D_MODEL
    return pl.pallas_call(
        _ada_kernel,
        grid=(DEPTH, n6 // tn),
        in_specs=[
            pl.BlockSpec((bt, D_MODEL), lambda i, j: (0, 0)),
            pl.BlockSpec((1, D_MODEL, tn), lambda i, j: (i, 0, j)),
            pl.BlockSpec((1, 1, tn), lambda i, j: (i, 0, j)),
        ],
        out_specs=pl.BlockSpec((1, bt, tn), lambda i, j: (i, 0, j)),
        out_shape=jax.ShapeDtypeStruct((DEPTH, bt, n6), F32),
        compiler_params=_params("arbitrary", "arbitrary"),
        name="ada_mod",
    )(c_all, w_ada, b_ada.reshape(DEPTH, 1, n6))


def _route(hn2, wr_hi, wr_lo, br, cnt_ref):
    h_hi = hn2.astype(BF16)
    h_lo = (hn2 - h_hi.astype(F32)).astype(BF16)
    logits = _dot(h_hi, wr_hi) + (_dot(h_hi, wr_lo) + _dot(h_lo, wr_hi)) + br
    lane = lax.broadcasted_iota(jnp.int32, logits.shape, 1)
    neg = jnp.float32(-jnp.inf)
    big = jnp.int32(1 << 20)
    is_g = lane < N_GROUPS
    lg = jnp.where(is_g, logits, neg)
    eg = jnp.exp(lg - jnp.max(lg, axis=-1, keepdims=True))
    pg = eg / jnp.sum(eg, axis=-1, keepdims=True)
    pg = jnp.where(is_g, pg, -1.0)
    gate_g = jnp.max(pg, axis=-1, keepdims=True)
    grp = jnp.min(jnp.where(pg == gate_g, lane, big), axis=-1, keepdims=True)
    lo = N_GROUPS + grp * EXPERTS_PER_GROUP
    is_e = (lane >= lo) & (lane < lo + EXPERTS_PER_GROUP)
    le = jnp.where(is_e, logits, neg)
    ee = jnp.exp(le - jnp.max(le, axis=-1, keepdims=True))
    pe = ee / jnp.sum(ee, axis=-1, keepdims=True)
    pe = jnp.where(is_e, pe, -1.0)
    p1 = jnp.max(pe, axis=-1, keepdims=True)
    i1 = jnp.min(jnp.where(pe == p1, lane, big), axis=-1, keepdims=True)
    pe2 = jnp.where(lane == i1, -1.0, pe)
    p2 = jnp.max(pe2, axis=-1, keepdims=True)
    i2 = jnp.min(jnp.where(pe2 == p2, lane, big), axis=-1, keepdims=True)
    tot = p1 + p2
    w1 = gate_g * (p1 / tot)
    w2 = gate_g * (p2 / tot)
    e1 = i1 - N_GROUPS
    e2 = i2 - N_GROUPS
    tm = logits.shape[0]
    oh1 = lane == e1
    oh2 = lane == e2
    cnt = jnp.where(oh1 | oh2, 1.0, 0.0)
    ri = lax.broadcasted_iota(jnp.int32, (tm, tm), 0)
    ci = lax.broadcasted_iota(jnp.int32, (tm, tm), 1)
    before = jnp.where(ci < ri, 1.0, 0.0).astype(BF16)
    pos = _dot(before, cnt.astype(BF16)) + cnt_ref[...]
    r1 = jnp.sum(jnp.where(oh1, pos, 0.0), axis=-1, keepdims=True)
    r2 = jnp.sum(jnp.where(oh2, pos, 0.0), axis=-1, keepdims=True)
    cnt_ref[...] = cnt_ref[...] + jnp.sum(cnt, axis=0, keepdims=True)
    vals = (e1.astype(F32), e2.astype(F32), w1, w2, r1, r2)
    out = jnp.zeros(logits.shape, F32)
    for j, val in enumerate(vals):
        out = jnp.where(lane == j, val, out)
    return out


def _tok_spec(tm):
    return pl.BlockSpec((1, tm, D_MODEL // LANES, LANES), lambda i, j: (i, j, 0, 0))


def _store_token_major(ref, lead, x):
    for c in range(D_MODEL // LANES):
        ref[lead + (slice(None), c, slice(None))] = x[:, c * LANES:(c + 1) * LANES]


def _load_token_major(ref, lead):
    return jnp.concatenate([ref[lead + (slice(None), c, slice(None))] for c in range(D_MODEL // LANES)], axis=1)


def _init_counts(cnt_ref):
    @pl.when((pl.program_id(0) == 0) & (pl.program_id(1) == 0))
    def _():
        cnt_ref[...] = jnp.zeros(cnt_ref.shape, F32)


def _post_mixer(x, y, mod, nf, wr_hi, wr_lo, br, cnt_ref):
    g1 = mod[2:3]
    sh2 = mod[3:4]
    sc2 = mod[4:5]
    x1 = x + g1 * y
    hn2 = (_rms(x1) * nf) * (1.0 + sc2) + sh2
    return x1, hn2, _route(hn2, wr_hi, wr_lo, br, cnt_ref)


def _gdn_in_kernel(x_ref, mod_ref, ng_ref, wqkv_ref, wz_ref, wab_ref, cw_ref, hist0_ref, alog_ref, dtb_ref,
                   q_ref, k_ref, v_ref, z_ref, gbc_ref, gbr_ref, hist_ref, xp_ref, *, tm, chunk):
    l = pl.program_id(1)

    @pl.when(l == 0)
    def _():
        xp_ref[0:8, :] = jnp.zeros((8, GDN_CONV_CH), F32)
        xp_ref[5:8, :] = hist0_ref[0]

    x = x_ref[0]
    mod = mod_ref[0]
    hn = (_rms(x) * ng_ref[...]) * (1.0 + mod[1:2]) + mod[0:1]
    hb = hn.astype(BF16)
    qkv = _dot(hb, wqkv_ref[...])
    xp_ref[8:8 + tm, :] = qkv
    cw = cw_ref[...]
    y = cw[3:4] * qkv
    for j in range(CONV_W - 1):
        y = y + cw[j:j + 1] * xp_ref[5 + j:5 + j + tm, :]
    new_hist = xp_ref[tm + 5:tm + 8, :]
    xp_ref[5:8, :] = new_hist
    hist_ref[0] = new_hist
    y = _silu(y)
    for h in range(GDN_HEADS):
        s = slice(h * GDN_DK, (h + 1) * GDN_DK)
        qh = y[:, s]
        q_ref[0, :, s] = qh * lax.rsqrt(jnp.sum(qh * qh, axis=-1, keepdims=True) + 1e-6) * (GDN_DK ** -0.5)
        kh = y[:, GDN_QK_W + h * GDN_DK:GDN_QK_W + (h + 1) * GDN_DK]
        k_ref[0, :, s] = kh * lax.rsqrt(jnp.sum(kh * kh, axis=-1, keepdims=True) + 1e-6)
    v_ref[0] = y[:, 2 * GDN_QK_W:]
    z_ref[0] = _dot(hb, wz_ref[...])
    ab = _dot(hb, wab_ref[...])
    g = -jnp.exp(alog_ref[...]) * _softplus(ab + dtb_ref[...])
    beta = _sigmoid(ab)
    ri = lax.broadcasted_iota(jnp.int32, (tm, tm), 0)
    ci = lax.broadcasted_iota(jnp.int32, (tm, tm), 1)
    tri = jnp.where((ri // chunk == ci // chunk) & (ci <= ri), 1.0, 0.0).astype(BF16)
    g1, g2, g3 = _split3(g)
    gcum = (_dot(tri, g1) + _dot(tri, g2)) + _dot(tri, g3)
    lane = lax.broadcasted_iota(jnp.int32, (tm, LANES), 1)
    gb = jnp.where(lane < GDN_HEADS, gcum, beta)
    gbc_ref[0] = gb
    er = lax.broadcasted_iota(jnp.int32, (2 * GDN_HEADS, LANES), 0)
    ec = lax.broadcasted_iota(jnp.int32, (2 * GDN_HEADS, LANES), 1)
    sel = jnp.where(er == ec, 1.0, 0.0).astype(BF16)
    b1, b2, b3 = _split3(gb)
    for n in range(tm // chunk):
        r = slice(n * chunk, (n + 1) * chunk)
        gbr_ref[0, n] = (_dot_nt(sel, b1[r]) + _dot_nt(sel, b2[r])) + _dot_nt(sel, b3[r])


def _gdn_in(x, mod, ng, wqkv, wz, wab, cw, hist0, alog, dtb, tm, chunk):
    b, l, d = x.shape
    grid = (b, l // tm)
    row = lambda shape: pl.BlockSpec(shape, lambda i, j: (i, j, 0))
    outs = pl.pallas_call(
        functools.partial(_gdn_in_kernel, tm=tm, chunk=chunk),
        grid=grid,
        in_specs=[
            row((1, tm, d)),
            pl.BlockSpec((1, 6, d), lambda i, j: (i, 0, 0)),
            _const_spec((1, d)),
            _const_spec(wqkv.shape),
            _const_spec(wz.shape),
            _const_spec(wab.shape),
            _const_spec(cw.shape),
            pl.BlockSpec((1, CONV_W - 1, GDN_CONV_CH), lambda i, j: (i, 0, 0)),
            _const_spec((1, LANES)),
            _const_spec((1, LANES)),
        ],
        out_specs=[
            row((1, tm, GDN_QK_W)), row((1, tm, GDN_QK_W)), row((1, tm, GDN_V_W)), row((1, tm, GDN_V_W)),
            row((1, tm, LANES)),
            pl.BlockSpec((1, tm // chunk, 2 * GDN_HEADS, chunk), lambda i, j: (i, j, 0, 0)),
            pl.BlockSpec((1, CONV_W - 1, GDN_CONV_CH), lambda i, j: (i, 0, 0)),
        ],
        out_shape=[
            jax.ShapeDtypeStruct((b, l, GDN_QK_W), F32), jax.ShapeDtypeStruct((b, l, GDN_QK_W), F32),
            jax.ShapeDtypeStruct((b, l, GDN_V_W), F32), jax.ShapeDtypeStruct((b, l, GDN_V_W), F32),
            jax.ShapeDtypeStruct((b, l, LANES), F32),
            jax.ShapeDtypeStruct((b, l // chunk, 2 * GDN_HEADS, chunk), F32),
            jax.ShapeDtypeStruct((b, CONV_W - 1, GDN_CONV_CH), F32),
        ],
        scratch_shapes=[pltpu.VMEM((tm + 8, GDN_CONV_CH), F32)],
        compiler_params=_params("arbitrary", "arbitrary"),
        name="gdn_in",
    )(x, mod, ng, wqkv, wz, wab, cw, hist0, alog, dtb)
    return outs


def _gdn_chunk_kernel(q_ref, k_ref, v_ref, gbc_ref, gbr_ref, s0_ref, o_ref, sout_ref, s_ref, *, tm, chunk):
    l = pl.program_id(1)

    @pl.when(l == 0)
    def _():
        s_ref[...] = s0_ref[0]

    c2 = 2 * chunk
    n_pairs = GDN_HEADS // 2
    ri = lax.broadcasted_iota(jnp.int32, (chunk, c2), 0)
    cl = lax.broadcasted_iota(jnp.int32, (chunk, c2), 1)
    ci = cl % chunk
    left = cl < chunk
    incl = ri >= ci
    strict = ri > ci
    eye = jnp.where(ri == ci, 1.0, 0.0).astype(F32)
    base = min(GDN_INV_BLOCK, chunk)
    levels = int(math.log2(base))
    diag_blk = (ri // base) == (ci // base)
    merge_masks = []
    blk = base
    while blk < chunk:
        merge_masks.append(((ri // (2 * blk)) == (ci // (2 * blk))) & ((ri // blk) % 2 == 1) & ((ci // blk) % 2 == 0))
        blk *= 2
    heads = range(GDN_HEADS)
    pairs = range(n_pairs)

    def bdiag(x):
        return jnp.concatenate([jnp.where(left, x, 0.0), jnp.where(left, 0.0, x)], axis=0).astype(BF16)

    def bdiag2(x0, x1):
        z = jnp.zeros_like(x0)
        return jnp.concatenate([jnp.concatenate([x0, z], axis=1), jnp.concatenate([z, x1], axis=1)], axis=0).astype(BF16)

    def one_chunk(n, carry):
        r0 = pl.multiple_of(n * chunk, chunk)
        rows = pl.ds(r0, chunk)
        gbc = gbc_ref[0, rows, :]
        gbr = gbr_ref[0, n]
        hs = [slice(h * GDN_DK, (h + 1) * GDN_DK) for h in heads]
        q = [q_ref[0, rows, hs[h]] for h in heads]
        k = [k_ref[0, rows, hs[h]] for h in heads]
        v = [v_ref[0, rows, hs[h]] for h in heads]
        gc = [gbc[:, h:h + 1] for h in heads]
        bc = [gbc[:, GDN_HEADS + h:GDN_HEADS + h + 1] for h in heads]
        kb = [k[h] * bc[h] for h in heads]
        a2 = [_dot_nt(jnp.concatenate([q[h], kb[h]], axis=0).astype(BF16), k[h].astype(BF16)) for h in heads]
        attn, m = [], []
        for hp in pairs:
            h0, h1 = 2 * hp, 2 * hp + 1
            gcp = jnp.where(left, gc[h0], gc[h1])
            grp = jnp.concatenate([gbr[h0:h0 + 1, :], gbr[h1:h1 + 1, :]], axis=1)
            decay = jnp.where(incl, jnp.exp(jnp.where(incl, gcp - grp, 0.0)), 0.0)
            ap = jnp.concatenate([a2[h0], a2[h1]], axis=1)
            attn.append(ap[:chunk] * decay)
            m.append(jnp.where(strict, ap[chunk:] * decay, 0.0))
        nq = [jnp.where(diag_blk, -m[hp], 0.0) for hp in pairs]
        p = [eye + nq[hp] for hp in pairs]
        nq = [_dot(nq[hp].astype(BF16), bdiag(nq[hp])) for hp in pairs]
        for j in range(1, levels):
            if j < levels - 1:
                r = [_dot(nq[hp].astype(BF16), jnp.concatenate([bdiag(nq[hp]), bdiag(p[hp])], axis=1)) for hp in pairs]
                nq = [r[hp][:, :c2] for hp in pairs]
                p = [p[hp] + r[hp][:, c2:] for hp in pairs]
            else:
                p = [p[hp] + _dot(nq[hp].astype(BF16), bdiag(p[hp])) for hp in pairs]
        for low in merge_masks:
            t = [_dot(jnp.where(low, m[hp], 0.0).astype(BF16), bdiag(p[hp])) for hp in pairs]
            p = [p[hp] - _dot(p[hp].astype(BF16), bdiag(t[hp])) for hp in pairs]
        eg = [jnp.exp(gc[h]) for h in heads]
        rhs = [jnp.concatenate([kb[h] * eg[h], v[h] * bc[h]], axis=1) for h in heads]
        wu = [_dot(p[hp].astype(BF16), bdiag2(rhs[2 * hp], rhs[2 * hp + 1])) for hp in pairs]
        wd = GDN_DK + GDN_DV
        w = [wu[h // 2][:, (h % 2) * wd:(h % 2) * wd + GDN_DK] for h in heads]
        u = [wu[h // 2][:, (h % 2) * wd + GDN_DK:(h % 2 + 1) * wd] for h in heads]
        st = [s_ref[h] for h in heads]
        ws = [_dot(jnp.concatenate([w[h], q[h] * eg[h]], axis=0).astype(BF16), st[h].astype(BF16)) for h in heads]
        v_new = [u[h] - ws[h][:chunk] for h in heads]
        av = [_dot(attn[hp].astype(BF16), bdiag2(v_new[2 * hp], v_new[2 * hp + 1])) for hp in pairs]
        for h in heads:
            o_ref[0, rows, hs[h]] = ws[h][chunk:] + av[h // 2][:, (h % 2) * GDN_DV:(h % 2 + 1) * GDN_DV]
        for h in heads:
            g_last = gc[h][chunk - 1:chunk, :]
            kd = k[h] * jnp.exp(g_last - gc[h])
            s_ref[h] = st[h] * jnp.exp(g_last) + _dot_tn(kd.astype(BF16), v_new[h].astype(BF16))
        return carry

    lax.fori_loop(0, tm // chunk, one_chunk, 0)
    sout_ref[0] = s_ref[...]


def _gdn_chunks(q, k, v, gbc, gbr, s0, tm, chunk):
    b, l, _ = q.shape
    row = lambda shape: pl.BlockSpec(shape, lambda i, j: (i, j, 0))
    sspec = pl.BlockSpec((1, GDN_HEADS, GDN_DK, GDN_DV), lambda i, j: (i, 0, 0, 0))
    return pl.pallas_call(
        functools.partial(_gdn_chunk_kernel, tm=tm, chunk=chunk),
        grid=(b, l // tm),
        in_specs=[
            row((1, tm, GDN_QK_W)), row((1, tm, GDN_QK_W)), row((1, tm, GDN_V_W)), row((1, tm, LANES)),
            pl.BlockSpec((1, tm // chunk, 2 * GDN_HEADS, chunk), lambda i, j: (i, j, 0, 0)),
            sspec,
        ],
        out_specs=[row((1, tm, GDN_V_W)), sspec],
        out_shape=[jax.ShapeDtypeStruct((b, l, GDN_V_W), F32),
                   jax.ShapeDtypeStruct((b, GDN_HEADS, GDN_DK, GDN_DV), F32)],
        scratch_shapes=[pltpu.VMEM((GDN_HEADS, GDN_DK, GDN_DV), F32)],
        compiler_params=_params("arbitrary", "arbitrary"),
        name="gdn_chunks",
    )(q, k, v, gbc, gbr, s0)


def _gdn_out_kernel(o_ref, z_ref, x_ref, mod_ref, gn_ref, wout_ref, nf_ref, wrh_ref, wrl_ref, br_ref,
                    x1_ref, hn2_ref, route_ref, cnt_ref):
    _init_counts(cnt_ref)
    o = o_ref[0]
    z = z_ref[0]
    gn = gn_ref[...]
    parts = []
    for h in range(GDN_HEADS):
        s = slice(h * GDN_DV, (h + 1) * GDN_DV)
        parts.append((_rms(o[:, s]) * gn) * _silu(z[:, s]))
    on = jnp.concatenate(parts, axis=1).astype(BF16)
    y = _dot(on, wout_ref[...])
    x1, hn2, route = _post_mixer(x_ref[0], y, mod_ref[0], nf_ref[...], wrh_ref[...], wrl_ref[...], br_ref[...],
                                 cnt_ref)
    x1_ref[0] = x1
    _store_token_major(hn2_ref, (0,), hn2)
    route_ref[0] = route


def _gdn_out(o, z, x, mod, gn, wout, nf, wrh, wrl, br, tm):
    b, l, d = x.shape
    row = lambda shape: pl.BlockSpec(shape, lambda i, j: (i, j, 0))
    return pl.pallas_call(
        _gdn_out_kernel,
        grid=(b, l // tm),
        in_specs=[
            row((1, tm, GDN_V_W)), row((1, tm, GDN_V_W)), row((1, tm, d)),
            pl.BlockSpec((1, 6, d), lambda i, j: (i, 0, 0)),
            _const_spec((1, GDN_DV)), _const_spec(wout.shape), _const_spec((1, d)),
            _const_spec(wrh.shape), _const_spec(wrl.shape), _const_spec((1, LANES)),
        ],
        out_specs=[row((1, tm, d)), _tok_spec(tm), row((1, tm, LANES)), _const_spec((1, LANES))],
        out_shape=[jax.ShapeDtypeStruct((b, l, d), F32), jax.ShapeDtypeStruct((b, l, d // LANES, LANES), F32),
                   jax.ShapeDtypeStruct((b, l, LANES), F32), jax.ShapeDtypeStruct((1, LANES), F32)],
        compiler_params=_params("arbitrary", "arbitrary"),
        name="gdn_out",
    )(o, z, x, mod, gn, wout, nf, wrh, wrl, br)


def _gelu_tanh(x):
    return 0.5 * x * (1.0 + jnp.tanh(math.sqrt(2.0 / math.pi) * (x + 0.044715 * (x * x * x))))


def _lru_kernel(xa_ref, ya_ref, yb_ref, rprev_ref, mprev_ref,
                mod_ref, ng_ref, win_ref, cw_ref, cb_ref, wax_ref, ba_ref, bx_ref, lam_ref, wout_ref,
                hist0_ref, h0_ref, nf_ref, wrh_ref, wrl_ref, br_ref,
                x1_ref, hn2_ref, route_ref, hist_ref, hlast_ref, cnt_ref, xp_ref, h_ref, *, tm):
    l = pl.program_id(1)
    _init_counts(cnt_ref)

    @pl.when(l == 0)
    def _():
        xp_ref[0:8, :] = jnp.zeros((8, LRU_WIDTH), F32)
        xp_ref[5:8, :] = hist0_ref[0]
        h_ref[...] = h0_ref[0]

    x = _moe_residual(xa_ref[0], _load_token_major(ya_ref, (0,)), _load_token_major(yb_ref, (0,)), rprev_ref[0],
                      mprev_ref[0])
    mod = mod_ref[0]
    hn = (_rms(x) * ng_ref[...]) * (1.0 + mod[1:2]) + mod[0:1]
    proj = _dot(hn.astype(BF16), win_ref[...])
    gate_br = _gelu_tanh(proj[:, :LRU_WIDTH])
    xb = proj[:, LRU_WIDTH:]
    xp_ref[8:8 + tm, :] = xb
    cw = cw_ref[...]
    xc = cw[3:4] * xb
    for j in range(CONV_W - 1):
        xc = xc + cw[j:j + 1] * xp_ref[5 + j:5 + j + tm, :]
    xc = xc + cb_ref[...]
    new_hist = xp_ref[tm + 5:tm + 8, :]
    xp_ref[5:8, :] = new_hist
    hist_ref[0] = new_hist
    xcb = xc.astype(BF16)
    ra, ia = [], []
    for h in range(LRU_BLOCKS):
        s = slice(h * LRU_BLOCK_W, (h + 1) * LRU_BLOCK_W)
        r2 = _dot(xcb[:, s], wax_ref[h])
        ra.append(r2[:, :LRU_BLOCK_W])
        ia.append(r2[:, LRU_BLOCK_W:])
    r = _sigmoid(jnp.concatenate(ra, axis=1) + ba_ref[...])
    i = _sigmoid(jnp.concatenate(ia, axis=1) + bx_ref[...])
    log_a = (-LRU_C * r) * _softplus(-lam_ref[...])
    a = jnp.exp(log_a)
    mult = jnp.sqrt(-_expm1(2.0 * log_a))
    b = mult * (i * xc)
    sub = lax.broadcasted_iota(jnp.int32, (tm, LRU_WIDTH), 0) % SUBLANES
    sft = 1
    while sft < SUBLANES:
        keep = sub >= sft
        a_prev = jnp.where(keep, pltpu.roll(a, sft, 0), 1.0)
        b_prev = jnp.where(keep, pltpu.roll(b, sft, 0), 0.0)
        b = a * b_prev + b
        a = a * a_prev
        sft *= 2
    h = h_ref[...]
    groups = []
    for g in range(tm // SUBLANES):
        rows = slice(g * SUBLANES, (g + 1) * SUBLANES)
        hg = b[rows] + a[rows] * h
        groups.append(hg)
        h = hg[SUBLANES - 1:SUBLANES, :]
    hs = jnp.concatenate(groups, axis=0)
    h_last = h
    h_ref[...] = h_last
    hlast_ref[0] = h_last
    y = _dot((hs * gate_br).astype(BF16), wout_ref[...])
    x1, hn2, route = _post_mixer(x, y, mod, nf_ref[...], wrh_ref[...], wrl_ref[...], br_ref[...], cnt_ref)
    x1_ref[0] = x1
    _store_token_major(hn2_ref, (0,), hn2)
    route_ref[0] = route


def _lru_layer(xa, ya, yb, rprev, mprev, mod, ng, win, cw, cb, wax, ba, bx, lam, wout, hist0, h0, nf, wrh, wrl, br, tm):
    b, l, d = xa.shape
    row = lambda shape: pl.BlockSpec(shape, lambda i, j: (i, j, 0))
    per_b = lambda shape: pl.BlockSpec(shape, lambda i, j: (i, 0, 0))
    vec = _const_spec((1, d))
    return pl.pallas_call(
        functools.partial(_lru_kernel, tm=tm),
        grid=(b, l // tm),
        in_specs=[
            row((1, tm, d)), _tok_spec(tm), _tok_spec(tm), row((1, tm, LANES)), per_b((1, 6, d)),
            per_b((1, 6, d)), vec, _const_spec(win.shape), _const_spec(cw.shape), vec,
            _const_spec(wax.shape), vec, vec, vec, _const_spec(wout.shape),
            per_b((1, CONV_W - 1, LRU_WIDTH)), per_b((1, 1, LRU_WIDTH)), vec,
            _const_spec(wrh.shape), _const_spec(wrl.shape), _const_spec((1, LANES)),
        ],
        out_specs=[row((1, tm, d)), _tok_spec(tm), row((1, tm, LANES)),
                   per_b((1, CONV_W - 1, LRU_WIDTH)), per_b((1, 1, LRU_WIDTH)), _const_spec((1, LANES))],
        out_shape=[jax.ShapeDtypeStruct((b, l, d), F32), jax.ShapeDtypeStruct((b, l, d // LANES, LANES), F32),
                   jax.ShapeDtypeStruct((b, l, LANES), F32),
                   jax.ShapeDtypeStruct((b, CONV_W - 1, LRU_WIDTH), F32),
                   jax.ShapeDtypeStruct((b, 1, LRU_WIDTH), F32), jax.ShapeDtypeStruct((1, LANES), F32)],
        scratch_shapes=[pltpu.VMEM((tm + 8, LRU_WIDTH), F32), pltpu.VMEM((1, LRU_WIDTH), F32)],
        compiler_params=_params("arbitrary", "arbitrary"),
        name="lru_layer",
    )(xa, ya, yb, rprev, mprev, mod, ng, win, cw, cb, wax, ba, bx, lam, wout, hist0, h0, nf, wrh, wrl, br)


def _moe_kernel(be_ref, st_ref, x_hbm, w1_ref, w3_ref, w2_ref, y_ref, w1b_ref, w3b_ref, w2b_ref, xbuf_ref, sem_ref):
    i = pl.program_id(0)
    last = pl.num_programs(0) - 1
    slot = i % 2

    def row_copy(blk, r, buf):
        tok = st_ref[blk * MOE_BLOCK + r]
        return pltpu.make_async_copy(x_hbm.at[tok], xbuf_ref.at[buf, r], sem_ref.at[buf])

    def wait_block(buf):
        pltpu.make_async_copy(x_hbm.at[pl.ds(0, MOE_BLOCK)], xbuf_ref.at[buf], sem_ref.at[buf]).wait()

    @pl.when(i == 0)
    def _():
        for r in range(MOE_BLOCK):
            row_copy(0, r, 0).start()

    prev = be_ref[jnp.maximum(i - 1, 0)]

    @pl.when((i == 0) | (be_ref[i] != prev))
    def _():
        w1b_ref[...] = w1_ref[0, 0].astype(BF16)
        w3b_ref[...] = w3_ref[0, 0].astype(BF16)
        w2b_ref[...] = w2_ref[0, 0].astype(BF16)

    nxt = jnp.minimum(i + 1, last)

    def body(buf):
        wait_block(buf)
        n_piece = 6
        per = MOE_BLOCK // n_piece + 1
        issued = [0]

        def issue_some():
            hi = min(issued[0] + per, MOE_BLOCK)
            for r in range(issued[0], hi):
                row_copy(nxt, r, 1 - buf).start()
            issued[0] = hi

        xb = _load_token_major(xbuf_ref, (buf,)).astype(BF16)
        half = D_FF_EXPERT // 2
        hid = []
        for c in range(2):
            cols = slice(c * half, (c + 1) * half)
            issue_some()
            h1 = _dot(xb, w1b_ref[:, cols])
            issue_some()
            h3 = _dot(xb, w3b_ref[:, cols])
            hid.append((_silu(h1) * h3).astype(BF16))
        issue_some()
        y = _dot(hid[0], w2b_ref[0:half, :])
        issue_some()
        _store_token_major(y_ref, (), y + _dot(hid[1], w2b_ref[half:, :]))
        assert issued[0] == MOE_BLOCK

        @pl.when(i == last)
        def _():
            wait_block(1 - buf)

    for buf in range(2):
        pl.when(slot == buf)(functools.partial(body, buf))


def _moe_blocks(blk_expert, slot_tok, x, w1, w3, w2, layer):
    d = D_MODEL
    p = slot_tok.shape[0]
    nb = p // MOE_BLOCK
    grid_spec = pltpu.PrefetchScalarGridSpec(
        num_scalar_prefetch=2,
        grid=(nb,),
        in_specs=[
            pl.BlockSpec(memory_space=pl.ANY),
            pl.BlockSpec((1, 1, d, D_FF_EXPERT), lambda i, be, st: (layer, be[i], 0, 0)),
            pl.BlockSpec((1, 1, d, D_FF_EXPERT), lambda i, be, st: (layer, be[i], 0, 0)),
            pl.BlockSpec((1, 1, D_FF_EXPERT, d), lambda i, be, st: (layer, be[i], 0, 0)),
        ],
        out_specs=pl.BlockSpec((MOE_BLOCK, d // LANES, LANES), lambda i, be, st: (i, 0, 0)),
        scratch_shapes=[pltpu.VMEM((d, D_FF_EXPERT), BF16), pltpu.VMEM((d, D_FF_EXPERT), BF16),
                        pltpu.VMEM((D_FF_EXPERT, d), BF16), pltpu.VMEM((2, MOE_BLOCK, d // LANES, LANES), F32),
                        pltpu.SemaphoreType.DMA((2,))],
    )
    return pl.pallas_call(
        _moe_kernel,
        grid_spec=grid_spec,
        out_shape=jax.ShapeDtypeStruct((p, d // LANES, LANES), F32),
        compiler_params=_params("arbitrary"),
        name="moe_blocks",
    )(blk_expert, slot_tok, x, w1, w3, w2)


def _moe_residual(x1, ya, yb, route, mod):
    return x1 + mod[5:6] * (ya * route[:, 2:3] + yb * route[:, 3:4])


def _final_kernel(x1_ref, ya_ref, yb_ref, route_ref, mod_ref, no_ref, o_ref):
    x2 = _moe_residual(x1_ref[0], _load_token_major(ya_ref, (0,)), _load_token_major(yb_ref, (0,)), route_ref[0],
                       mod_ref[0])
    o_ref[0] = _rms(x2) * no_ref[...]


def _final(x1, ya, yb, route, mod, norm_out, tm):
    b, l, d = x1.shape
    row = lambda shape: pl.BlockSpec(shape, lambda i, j: (i, j, 0))
    return pl.pallas_call(
        _final_kernel,
        grid=(b, l // tm),
        in_specs=[row((1, tm, d)), _tok_spec(tm), _tok_spec(tm), row((1, tm, LANES)),
                  pl.BlockSpec((1, 6, d), lambda i, j: (i, 0, 0)), _const_spec((1, d))],
        out_specs=row((1, tm, d)),
        out_shape=jax.ShapeDtypeStruct((b, l, d), F32),
        compiler_params=_params("arbitrary", "arbitrary"),
        name="final",
    )(x1, ya, yb, route, mod, norm_out)


def _moe_experts(hn2, route, counts, w1, w3, w2, layer):
    b, l = hn2.shape[:2]
    d = D_MODEL
    n = b * l
    a = n * EXPERT_TOPK
    rt = route.reshape(n, LANES)
    e_ab = rt[:, 0:EXPERT_TOPK].astype(jnp.int32)
    r_ab = rt[:, 4:4 + EXPERT_TOPK].astype(jnp.int32)
    counts = counts[0, :N_EXPERTS].astype(jnp.int32)
    padded = (counts + MOE_BLOCK - 1) // MOE_BLOCK * MOE_BLOCK
    pad_end = jnp.cumsum(padded)
    pad_start = pad_end - padded
    dest = pad_start[e_ab] + r_ab
    n_blocks = -(-a // MOE_BLOCK) + N_EXPERTS
    p = n_blocks * MOE_BLOCK
    tok = jnp.broadcast_to(jnp.arange(n, dtype=jnp.int32)[:, None], (n, EXPERT_TOPK))
    slot_tok = (jnp.arange(p, dtype=jnp.int32) % n).at[dest.reshape(-1)].set(tok.reshape(-1))
    blk_first = jnp.arange(n_blocks, dtype=jnp.int32) * MOE_BLOCK
    blk_expert = jnp.minimum(jnp.sum((pad_end[None, :] <= blk_first[:, None]).astype(jnp.int32), axis=1),
                             N_EXPERTS - 1)
    ys = _moe_blocks(blk_expert, slot_tok, hn2.reshape(n, d // LANES, LANES), w1, w3, w2, layer)
    tok_shape = (b, l, d // LANES, LANES)
    return ys[dest[:, 0]].reshape(tok_shape), ys[dest[:, 1]].reshape(tok_shape)


def _trunk(x, mods, gdn_s, gdn_conv, lru_h, lru_conv, wp):
    b, l, d = x.shape
    tm = min(ROW_TILE, l)
    chunk = min(GDN_CHUNK, l)
    assert l % tm == 0 and tm % chunk == 0 and chunk & (chunk - 1) == 0
    mod = mods[0]
    q, k, v, z, gbc, gbr, gconv_new = _gdn_in(x, mod, wp['norm_mix0'], wp['gdn_wqkv'], wp['gdn_wz'], wp['gdn_wab'],
                                              wp['gdn_conv_w'], gdn_conv, wp['gdn_alog'], wp['gdn_dtb'], tm, chunk)
    o, s_new = _gdn_chunks(q, k, v, gbc, gbr, gdn_s, tm, chunk)
    x1, hn2, route, counts = _gdn_out(o, z, x, mod, wp['gdn_norm'], wp['gdn_wout'], wp['norm_ffn0'],
                                      wp['wr_hi0'], wp['wr_lo0'], wp['br0'], tm)
    ya, yb = _moe_experts(hn2, route, counts, wp['moe_w1'], wp['moe_w3'], wp['moe_w2'], 0)
    x1, hn2, route, lconv_new, h_new, counts = _lru_layer(
        x1, ya, yb, route, mod, mods[1], wp['norm_mix1'], wp['lru_win'], wp['lru_conv_w'], wp['lru_conv_b'],
        wp['lru_wax'], wp['lru_ba'], wp['lru_bx'], wp['lru_lam'], wp['lru_wout'], lru_conv, lru_h, wp['norm_ffn1'],
        wp['wr_hi1'], wp['wr_lo1'], wp['br1'], tm)
    ya, yb = _moe_experts(hn2, route, counts, wp['moe_w1'], wp['moe_w3'], wp['moe_w2'], 1)
    y = _final(x1, ya, yb, route, mods[1], wp['norm_out'], tm)
    return y, s_new[None], gconv_new[None], h_new.reshape(1, b, LRU_WIDTH), lconv_new[None]


def _pad_lanes(v, width=LANES):
    v = v.reshape(1, -1)
    return jnp.pad(v, ((0, 0), (0, width - v.shape[1])))


def _router_weights(w_rg, b_rg, w_re, b_re):
    w = jnp.pad(jnp.concatenate([w_rg, w_re], axis=1), ((0, 0), (0, LANES - N_GROUPS - N_EXPERTS)))
    hi = w.astype(BF16)
    lo = (w - hi.astype(F32)).astype(BF16)
    return hi, lo, _pad_lanes(jnp.concatenate([b_rg, b_re]))


def kernel(x_prompt, x_sample, state_gdn_S, state_gdn_conv, state_lru_h, state_lru_conv, c_prompt, c_sample, w_ada, b_ada, norm_mix, norm_ffn, norm_out, gdn_w_in, gdn_conv_w, gdn_a_log, gdn_dt_bias, gdn_norm, gdn_w_out, lru_w_in, lru_conv_w, lru_conv_b, lru_w_a, lru_b_a, lru_w_x, lru_b_x, lru_lambda, lru_w_out, moe_w_rg, moe_b_rg, moe_w_re, moe_b_re, moe_w1, moe_w3, moe_w2):
    d = D_MODEL
    bp = x_prompt.shape[0]
    bs = x_sample.shape[0]
    win = gdn_w_in[0]
    wab = jnp.pad(win[:, GDN_CONV_CH + GDN_V_W:], ((0, 0), (0, LANES - 2 * GDN_HEADS)))
    wp = dict(
        norm_mix0=norm_mix[0].reshape(1, d), norm_mix1=norm_mix[1].reshape(1, d),
        norm_ffn0=norm_ffn[0].reshape(1, d), norm_ffn1=norm_ffn[1].reshape(1, d),
        norm_out=norm_out.reshape(1, d),
        gdn_wqkv=win[:, :GDN_CONV_CH].astype(BF16),
        gdn_wz=win[:, GDN_CONV_CH:GDN_CONV_CH + GDN_V_W].astype(BF16),
        gdn_wab=wab.astype(BF16),
        gdn_conv_w=gdn_conv_w[0],
        gdn_alog=_pad_lanes(gdn_a_log[0]), gdn_dtb=_pad_lanes(gdn_dt_bias[0]),
        gdn_norm=gdn_norm[0].reshape(1, GDN_DV), gdn_wout=gdn_w_out[0].astype(BF16),
        lru_win=lru_w_in[0].astype(BF16), lru_conv_w=lru_conv_w[0], lru_conv_b=lru_conv_b[0].reshape(1, d),
        lru_wax=jnp.concatenate([lru_w_a[0], lru_w_x[0]], axis=-1).astype(BF16),
        lru_ba=lru_b_a[0].reshape(1, d), lru_bx=lru_b_x[0].reshape(1, d), lru_lam=lru_lambda[0].reshape(1, d),
        lru_wout=lru_w_out[0].astype(BF16),
        moe_w1=moe_w1, moe_w3=moe_w3, moe_w2=moe_w2,
    )
    for i in range(DEPTH):
        wp[f'wr_hi{i}'], wp[f'wr_lo{i}'], wp[f'br{i}'] = _router_weights(moe_w_rg[i], moe_b_rg[i], moe_w_re[i], moe_b_re[i])

    mods = _ada_mod(jnp.concatenate([c_prompt, c_sample], axis=0), w_ada, b_ada)
    mods = mods.reshape(DEPTH, bp + bs, 6, d)
    mods_p = [mods[i, :bp] for i in range(DEPTH)]
    mods_s = [mods[i, bp:] for i in range(DEPTH)]

    dt = x_prompt.dtype
    z_s = jnp.zeros((bp, GDN_HEADS, GDN_DK, GDN_DV), dt)
    z_gc = jnp.zeros((bp, CONV_W - 1, GDN_CONV_CH), dt)
    z_h = jnp.zeros((bp, 1, LRU_WIDTH), dt)
    z_lc = jnp.zeros((bp, CONV_W - 1, LRU_WIDTH), dt)
    y_p, gs_p, gc_p, lh_p, lc_p = _trunk(x_prompt, mods_p, z_s, z_gc, z_h, z_lc, wp)
    y_s, gs_s, gc_s, lh_s, lc_s = _trunk(x_sample, mods_s, state_gdn_S[0], state_gdn_conv[0],
                                         state_lru_h[0].reshape(bs, 1, LRU_WIDTH), state_lru_conv[0], wp)
    return (y_p, y_s, gs_p, gc_p, lh_p, lc_p, gs_s, gc_s, lh_s, lc_s)
```

```python
import functools
import math

import jax
import jax.numpy as jnp
from jax import lax
from jax.experimental import pallas as pl
from jax.experimental.pallas import tpu as pltpu

F32 = jnp.float32
BF16 = jnp.bfloat16

D_MODEL = 1024
DEPTH = 2
CONV_W = 4
NORM_EPS = 1e-6
GDN_HEADS = 8
GDN_DK = 128
GDN_DV = 128
GDN_QK_W = GDN_HEADS * GDN_DK
GDN_V_W = GDN_HEADS * GDN_DV
GDN_CONV_CH = 2 * GDN_QK_W + GDN_V_W
GDN_CHUNK = 64
GDN_INV_BLOCK = 16
LRU_WIDTH = D_MODEL
LRU_BLOCKS = 8
LRU_BLOCK_W = LRU_WIDTH // LRU_BLOCKS
LRU_C = 8.0
N_GROUPS = 4
EXPERTS_PER_GROUP = 8
N_EXPERTS = N_GROUPS * EXPERTS_PER_GROUP
EXPERT_TOPK = 2
D_FF_EXPERT = 512
MOE_BLOCK = 256
LANES = 128
SUBLANES = 8
ROW_TILE = 256
VMEM_LIMIT = 56 * 1024 * 1024


def _dot(a, b):
    return jnp.dot(a, b, preferred_element_type=F32)


def _dot_nt(a, b):
    return lax.dot_general(a, b, (((1,), (1,)), ((), ())), preferred_element_type=F32)


def _dot_tn(a, b):
    return lax.dot_general(a, b, (((0,), (0,)), ((), ())), preferred_element_type=F32)


def _split3(x):
    a = x.astype(BF16)
    r = x - a.astype(F32)
    b = r.astype(BF16)
    c = (r - b.astype(F32)).astype(BF16)
    return a, b, c


def _rms(x):
    return x * lax.rsqrt(jnp.mean(x * x, axis=-1, keepdims=True) + NORM_EPS)


def _sigmoid(x):
    return 1.0 / (1.0 + jnp.exp(-x))


def _silu(x):
    return x * _sigmoid(x)


def _softplus(x):
    return jnp.maximum(x, 0.0) + jnp.log1p(jnp.exp(-jnp.abs(x)))


def _expm1(x):
    u = jnp.exp(x)
    near = (u > 0.5) & (u < 2.0) & (u != 1.0)
    corrected = (u - 1.0) * x / jnp.where(near, jnp.log(u), 1.0)
    return jnp.where(u == 1.0, x, jnp.where(near, corrected, u - 1.0))


def _params(*sem):
    return pltpu.CompilerParams(dimension_semantics=sem, vmem_limit_bytes=VMEM_LIMIT)


def _const_spec(shape):
    nd = len(shape)
    return pl.BlockSpec(shape, lambda *_: (0,) * nd)


def _ada_kernel(c_ref, w_ref, b_ref, o_ref):
    s = _silu(c_ref[...]).astype(BF16)
    o_ref[0] = _dot(s, w_ref[0].astype(BF16)) + b_ref[0]


def _ada_mod(c_all, w_ada, b_ada):
    bt = c_all.shape[0]
    tn = 1536
    n6 = 6 * D_MODEL
    return pl.pallas_call(
        _ada_kernel,
        grid=(DEPTH, n6 // tn),
        in_specs=[
            pl.BlockSpec((bt, D_MODEL), lambda i, j: (0, 0)),
            pl.BlockSpec((1, D_MODEL, tn), lambda i, j: (i, 0, j)),
            pl.BlockSpec((1, 1, tn), lambda i, j: (i, 0, j)),
        ],
        out_specs=pl.BlockSpec((1, bt, tn), lambda i, j: (i, 0, j)),
        out_shape=jax.ShapeDtypeStruct((DEPTH, bt, n6), F32),
        compiler_params=_params("arbitrary", "arbitrary"),
        name="ada_mod",
    )(c_all, w_ada, b_ada.reshape(DEPTH, 1, n6))


def _route(hn2, wr, br, cnt_ref):
    logits = _dot(hn2.astype(BF16), wr) + br
    lane = lax.broadcasted_iota(jnp.int32, logits.shape, 1)
    neg = jnp.float32(-jnp.inf)
    big = jnp.int32(1 << 20)
    is_g = lane < N_GROUPS
    lg = jnp.where(is_g, logits, neg)
    eg = jnp.exp(lg - jnp.max(lg, axis=-1, keepdims=True))
    pg = eg / jnp.sum(eg, axis=-1, keepdims=True)
    pg = jnp.where(is_g, pg, -1.0)
    gate_g = jnp.max(pg, axis=-1, keepdims=True)
    grp = jnp.min(jnp.where(pg == gate_g, lane, big), axis=-1, keepdims=True)
    lo = N_GROUPS + grp * EXPERTS_PER_GROUP
    is_e = (lane >= lo) & (lane < lo + EXPERTS_PER_GROUP)
    le = jnp.where(is_e, logits, neg)
    ee = jnp.exp(le - jnp.max(le, axis=-1, keepdims=True))
    pe = ee / jnp.sum(ee, axis=-1, keepdims=True)
    pe = jnp.where(is_e, pe, -1.0)
    p1 = jnp.max(pe, axis=-1, keepdims=True)
    i1 = jnp.min(jnp.where(pe == p1, lane, big), axis=-1, keepdims=True)
    pe2 = jnp.where(lane == i1, -1.0, pe)
    p2 = jnp.max(pe2, axis=-1, keepdims=True)
    i2 = jnp.min(jnp.where(pe2 == p2, lane, big), axis=-1, keepdims=True)
    tot = p1 + p2
    w1 = gate_g * (p1 / tot)
    w2 = gate_g * (p2 / tot)
    e1 = i1 - N_GROUPS
    e2 = i2 - N_GROUPS
    tm = logits.shape[0]
    oh1 = lane == e1
    oh2 = lane == e2
    cnt = jnp.where(oh1 | oh2, 1.0, 0.0)
    ri = lax.broadcasted_iota(jnp.int32, (tm, tm), 0)
    ci = lax.broadcasted_iota(jnp.int32, (tm, tm), 1)
    before = jnp.where(ci < ri, 1.0, 0.0).astype(BF16)
    pos = _dot(before, cnt.astype(BF16)) + cnt_ref[...]
    r1 = jnp.sum(jnp.where(oh1, pos, 0.0), axis=-1, keepdims=True)
    r2 = jnp.sum(jnp.where(oh2, pos, 0.0), axis=-1, keepdims=True)
    cnt_ref[...] = cnt_ref[...] + jnp.sum(cnt, axis=0, keepdims=True)
    vals = (e1.astype(F32), e2.astype(F32), w1, w2, r1, r2)
    out = jnp.zeros(logits.shape, F32)
    for j, val in enumerate(vals):
        out = jnp.where(lane == j, val, out)
    return out


def _tok_spec(tm):
    return pl.BlockSpec((1, tm, D_MODEL // LANES, LANES), lambda i, j, *_: (i, j, 0, 0))


def _store_token_major(ref, lead, x):
    for c in range(D_MODEL // LANES):
        ref[lead + (slice(None), c, slice(None))] = x[:, c * LANES:(c + 1) * LANES]


def _load_token_major(ref, lead, rows=slice(None)):
    return jnp.concatenate([ref[lead + (rows, c, slice(None))] for c in range(D_MODEL // LANES)], axis=1)


GATHER_PRIORITY = 1


class _PairGather:
    def __init__(self, dest_ref, ys_hbm, buf_ref, sem_ref, tm, n_pieces):
        self.dest_ref, self.ys_hbm, self.buf_ref, self.sem_ref, self.tm = dest_ref, ys_hbm, buf_ref, sem_ref, tm
        n_steps = pl.num_programs(0) * pl.num_programs(1)
        self.n = n_steps * tm
        self.step = pl.program_id(0) * pl.num_programs(1) + pl.program_id(1)
        self.last = n_steps - 1
        self.slot = self.step % 2
        self.nxt = jnp.minimum(self.step + 1, self.last)
        self.per = -(-2 * tm // n_pieces)
        self.issued = 0

    def _copy(self, tile, j, buf):
        k, r = divmod(j, self.tm)
        row = self.dest_ref[k * self.n + tile * self.tm + r]
        return pltpu.make_async_copy(self.ys_hbm.at[row], self.buf_ref.at[buf, j], self.sem_ref.at[buf])

    def _wait(self, buf):
        pltpu.make_async_copy(self.ys_hbm.at[pl.ds(0, 2 * self.tm)], self.buf_ref.at[buf], self.sem_ref.at[buf]).wait()

    def start(self):
        @pl.when(self.step == 0)
        def _():
            for j in range(2 * self.tm):
                self._copy(0, j, 0).start(priority=GATHER_PRIORITY)

        self._wait(self.slot)
        ya = _load_token_major(self.buf_ref, (self.slot,), slice(0, self.tm))
        yb = _load_token_major(self.buf_ref, (self.slot,), slice(self.tm, 2 * self.tm))
        return ya, yb

    def issue_piece(self):
        hi = min(self.issued + self.per, 2 * self.tm)
        for j in range(self.issued, hi):
            self._copy(self.nxt, j, 1 - self.slot).start(priority=GATHER_PRIORITY)
        self.issued = hi

    def finish(self):
        assert self.issued == 2 * self.tm

        @pl.when(self.step == self.last)
        def _():
            self._wait(1 - self.slot)


def _init_counts(cnt_ref):
    @pl.when((pl.program_id(0) == 0) & (pl.program_id(1) == 0))
    def _():
        cnt_ref[...] = jnp.zeros(cnt_ref.shape, F32)


def _post_mixer(x, y, mod, nf, wr, br, cnt_ref):
    g1 = mod[2:3]
    sh2 = mod[3:4]
    sc2 = mod[4:5]
    x1 = x + g1 * y
    hn2 = (_rms(x1) * nf) * (1.0 + sc2) + sh2
    return x1, hn2, _route(hn2, wr, br, cnt_ref)


def _gdn_in_kernel(x_ref, mod_ref, ng_ref, wqkv_ref, wz_ref, wab_ref, cw_ref, hist0_ref, alog_ref, dtb_ref,
                   q_ref, k_ref, v_ref, z_ref, gbc_ref, gbr_ref, hist_ref, xp_ref, *, tm, chunk):
    l = pl.program_id(1)

    @pl.when(l == 0)
    def _():
        xp_ref[0:8, :] = jnp.zeros((8, GDN_CONV_CH), F32)
        xp_ref[5:8, :] = hist0_ref[0]

    x = x_ref[0]
    mod = mod_ref[0]
    hn = (_rms(x) * ng_ref[...]) * (1.0 + mod[1:2]) + mod[0:1]
    hb = hn.astype(BF16)
    qkv = _dot(hb, wqkv_ref[...])
    xp_ref[8:8 + tm, :] = qkv
    cw = cw_ref[...]
    y = cw[3:4] * qkv
    for j in range(CONV_W - 1):
        y = y + cw[j:j + 1] * xp_ref[5 + j:5 + j + tm, :]
    new_hist = xp_ref[tm + 5:tm + 8, :]
    xp_ref[5:8, :] = new_hist
    hist_ref[0] = new_hist
    y = _silu(y)
    for h in range(GDN_HEADS):
        s = slice(h * GDN_DK, (h + 1) * GDN_DK)
        qh = y[:, s]
        q_ref[0, :, s] = qh * lax.rsqrt(jnp.sum(qh * qh, axis=-1, keepdims=True) + 1e-6) * (GDN_DK ** -0.5)
        kh = y[:, GDN_QK_W + h * GDN_DK:GDN_QK_W + (h + 1) * GDN_DK]
        k_ref[0, :, s] = kh * lax.rsqrt(jnp.sum(kh * kh, axis=-1, keepdims=True) + 1e-6)
    v_ref[0] = y[:, 2 * GDN_QK_W:]
    z_ref[0] = _dot(hb, wz_ref[...])
    ab = _dot(hb, wab_ref[...])
    g = -jnp.exp(alog_ref[...]) * _softplus(ab + dtb_ref[...])
    beta = _sigmoid(ab)
    ri = lax.broadcasted_iota(jnp.int32, (tm, tm), 0)
    ci = lax.broadcasted_iota(jnp.int32, (tm, tm), 1)
    tri = jnp.where((ri // chunk == ci // chunk) & (ci <= ri), 1.0, 0.0).astype(BF16)
    g1, g2, g3 = _split3(g)
    gcum = (_dot(tri, g1) + _dot(tri, g2)) + _dot(tri, g3)
    lane = lax.broadcasted_iota(jnp.int32, (tm, LANES), 1)
    gb = jnp.where(lane < GDN_HEADS, gcum, beta)
    gbc_ref[0] = gb
    er = lax.broadcasted_iota(jnp.int32, (2 * GDN_HEADS, LANES), 0)
    ec = lax.broadcasted_iota(jnp.int32, (2 * GDN_HEADS, LANES), 1)
    sel = jnp.where(er == ec, 1.0, 0.0).astype(BF16)
    b1, b2, b3 = _split3(gb)
    for n in range(tm // chunk):
        r = slice(n * chunk, (n + 1) * chunk)
        gbr_ref[0, n] = (_dot_nt(sel, b1[r]) + _dot_nt(sel, b2[r])) + _dot_nt(sel, b3[r])


def _gdn_in(x, mod, ng, wqkv, wz, wab, cw, hist0, alog, dtb, tm, chunk):
    b, l, d = x.shape
    grid = (b, l // tm)
    row = lambda shape: pl.BlockSpec(shape, lambda i, j: (i, j, 0))
    outs = pl.pallas_call(
        functools.partial(_gdn_in_kernel, tm=tm, chunk=chunk),
        grid=grid,
        in_specs=[
            row((1, tm, d)),
            pl.BlockSpec((1, 6, d), lambda i, j: (i, 0, 0)),
            _const_spec((1, d)),
            _const_spec(wqkv.shape),
            _const_spec(wz.shape),
            _const_spec(wab.shape),
            _const_spec(cw.shape),
            pl.BlockSpec((1, CONV_W - 1, GDN_CONV_CH), lambda i, j: (i, 0, 0)),
            _const_spec((1, LANES)),
            _const_spec((1, LANES)),
        ],
        out_specs=[
            row((1, tm, GDN_QK_W)), row((1, tm, GDN_QK_W)), row((1, tm, GDN_V_W)), row((1, tm, GDN_V_W)),
            row((1, tm, LANES)),
            pl.BlockSpec((1, tm // chunk, 2 * GDN_HEADS, chunk), lambda i, j: (i, j, 0, 0)),
            pl.BlockSpec((1, CONV_W - 1, GDN_CONV_CH), lambda i, j: (i, 0, 0)),
        ],
        out_shape=[
            jax.ShapeDtypeStruct((b, l, GDN_QK_W), F32), jax.ShapeDtypeStruct((b, l, GDN_QK_W), F32),
            jax.ShapeDtypeStruct((b, l, GDN_V_W), F32), jax.ShapeDtypeStruct((b, l, GDN_V_W), F32),
            jax.ShapeDtypeStruct((b, l, LANES), F32),
            jax.ShapeDtypeStruct((b, l // chunk, 2 * GDN_HEADS, chunk), F32),
            jax.ShapeDtypeStruct((b, CONV_W - 1, GDN_CONV_CH), F32),
        ],
        scratch_shapes=[pltpu.VMEM((tm + 8, GDN_CONV_CH), F32)],
        compiler_params=_params("arbitrary", "arbitrary"),
        name="gdn_in",
    )(x, mod, ng, wqkv, wz, wab, cw, hist0, alog, dtb)
    return outs


def _gdn_chunk_kernel(q_ref, k_ref, v_ref, gbc_ref, gbr_ref, s0_ref, o_ref, sout_ref, s_ref, *, tm, chunk):
    l = pl.program_id(1)

    @pl.when(l == 0)
    def _():
        s_ref[...] = s0_ref[0]

    c2 = 2 * chunk
    n_pairs = GDN_HEADS // 2
    ri = lax.broadcasted_iota(jnp.int32, (chunk, c2), 0)
    cl = lax.broadcasted_iota(jnp.int32, (chunk, c2), 1)
    ci = cl % chunk
    left = cl < chunk
    incl = ri >= ci
    strict = ri > ci
    eye = jnp.where(ri == ci, 1.0, 0.0).astype(F32)
    base = min(GDN_INV_BLOCK, chunk)
    levels = int(math.log2(base))
    diag_blk = (ri // base) == (ci // base)
    merge_masks = []
    blk = base
    while blk < chunk:
        merge_masks.append(((ri // (2 * blk)) == (ci // (2 * blk))) & ((ri // blk) % 2 == 1) & ((ci // blk) % 2 == 0))
        blk *= 2
    heads = range(GDN_HEADS)
    pairs = range(n_pairs)

    def bdiag(x):
        return jnp.concatenate([jnp.where(left, x, 0.0), jnp.where(left, 0.0, x)], axis=0).astype(BF16)

    def bdiag2(x0, x1):
        z = jnp.zeros_like(x0)
        return jnp.concatenate([jnp.concatenate([x0, z], axis=1), jnp.concatenate([z, x1], axis=1)], axis=0).astype(BF16)

    def one_chunk(n, carry):
        r0 = pl.multiple_of(n * chunk, chunk)
        rows = pl.ds(r0, chunk)
        gbc = gbc_ref[0, rows, :]
        gbr = gbr_ref[0, n]
        hs = [slice(h * GDN_DK, (h + 1) * GDN_DK) for h in heads]
        q = [q_ref[0, rows, hs[h]] for h in heads]
        k = [k_ref[0, rows, hs[h]] for h in heads]
        v = [v_ref[0, rows, hs[h]] for h in heads]
        gc = [gbc[:, h:h + 1] for h in heads]
        bc = [gbc[:, GDN_HEADS + h:GDN_HEADS + h + 1] for h in heads]
        kb = [k[h] * bc[h] for h in heads]
        a2 = [_dot_nt(jnp.concatenate([q[h], kb[h]], axis=0).astype(BF16), k[h].astype(BF16)) for h in heads]
        attn, m = [], []
        for hp in pairs:
            h0, h1 = 2 * hp, 2 * hp + 1
            gcp = jnp.where(left, gc[h0], gc[h1])
            grp = jnp.concatenate([gbr[h0:h0 + 1, :], gbr[h1:h1 + 1, :]], axis=1)
            decay = jnp.where(incl, jnp.exp(jnp.where(incl, gcp - grp, 0.0)), 0.0)
            ap = jnp.concatenate([a2[h0], a2[h1]], axis=1)
            attn.append(ap[:chunk] * decay)
            m.append(jnp.where(strict, ap[chunk:] * decay, 0.0))
        nq = [jnp.where(diag_blk, -m[hp], 0.0) for hp in pairs]
        p = [eye + nq[hp] for hp in pairs]
        nq = [_dot(nq[hp].astype(BF16), bdiag(nq[hp])) for hp in pairs]
        for j in range(1, levels):
            if j < levels - 1:
                r = [_dot(nq[hp].astype(BF16), jnp.concatenate([bdiag(nq[hp]), bdiag(p[hp])], axis=1)) for hp in pairs]
                nq = [r[hp][:, :c2] for hp in pairs]
                p = [p[hp] + r[hp][:, c2:] for hp in pairs]
            else:
                p = [p[hp] + _dot(nq[hp].astype(BF16), bdiag(p[hp])) for hp in pairs]
        for low in merge_masks:
            t = [_dot(jnp.where(low, m[hp], 0.0).astype(BF16), bdiag(p[hp])) for hp in pairs]
            p = [p[hp] - _dot(p[hp].astype(BF16), bdiag(t[hp])) for hp in pairs]
        eg = [jnp.exp(gc[h]) for h in heads]
        rhs = [jnp.concatenate([kb[h] * eg[h], v[h] * bc[h]], axis=1) for h in heads]
        wu = [_dot(p[hp].astype(BF16), bdiag2(rhs[2 * hp], rhs[2 * hp + 1])) for hp in pairs]
        wd = GDN_DK + GDN_DV
        w = [wu[h // 2][:, (h % 2) * wd:(h % 2) * wd + GDN_DK] for h in heads]
        u = [wu[h // 2][:, (h % 2) * wd + GDN_DK:(h % 2 + 1) * wd] for h in heads]
        st = [s_ref[h] for h in heads]
        ws = [_dot(jnp.concatenate([w[h], q[h] * eg[h]], axis=0).astype(BF16), st[h].astype(BF16)) for h in heads]
        v_new = [u[h] - ws[h][:chunk] for h in heads]
        av = [_dot(attn[hp].astype(BF16), bdiag2(v_new[2 * hp], v_new[2 * hp + 1])) for hp in pairs]
        for h in heads:
            o_ref[0, rows, hs[h]] = ws[h][chunk:] + av[h // 2][:, (h % 2) * GDN_DV:(h % 2 + 1) * GDN_DV]
        for h in heads:
            g_last = gc[h][chunk - 1:chunk, :]
            kd = k[h] * jnp.exp(g_last - gc[h])
            s_ref[h] = st[h] * jnp.exp(g_last) + _dot_tn(kd.astype(BF16), v_new[h].astype(BF16))
        return carry

    lax.fori_loop(0, tm // chunk, one_chunk, 0)
    sout_ref[0] = s_ref[...]


def _gdn_chunks(q, k, v, gbc, gbr, s0, tm, chunk):
    b, l, _ = q.shape
    row = lambda shape: pl.BlockSpec(shape, lambda i, j: (i, j, 0))
    sspec = pl.BlockSpec((1, GDN_HEADS, GDN_DK, GDN_DV), lambda i, j: (i, 0, 0, 0))
    return pl.pallas_call(
        functools.partial(_gdn_chunk_kernel, tm=tm, chunk=chunk),
        grid=(b, l // tm),
        in_specs=[
            row((1, tm, GDN_QK_W)), row((1, tm, GDN_QK_W)), row((1, tm, GDN_V_W)), row((1, tm, LANES)),
            pl.BlockSpec((1, tm // chunk, 2 * GDN_HEADS, chunk), lambda i, j: (i, j, 0, 0)),
            sspec,
        ],
        out_specs=[row((1, tm, GDN_V_W)), sspec],
        out_shape=[jax.ShapeDtypeStruct((b, l, GDN_V_W), F32),
                   jax.ShapeDtypeStruct((b, GDN_HEADS, GDN_DK, GDN_DV), F32)],
        scratch_shapes=[pltpu.VMEM((GDN_HEADS, GDN_DK, GDN_DV), F32)],
        compiler_params=_params("arbitrary", "arbitrary"),
        name="gdn_chunks",
    )(q, k, v, gbc, gbr, s0)


def _gdn_out_kernel(o_ref, z_ref, x_ref, mod_ref, gn_ref, wout_ref, nf_ref, wr_ref, br_ref,
                    x1_ref, hn2_ref, route_ref, cnt_ref):
    _init_counts(cnt_ref)
    o = o_ref[0]
    z = z_ref[0]
    gn = gn_ref[...]
    parts = []
    for h in range(GDN_HEADS):
        s = slice(h * GDN_DV, (h + 1) * GDN_DV)
        parts.append((_rms(o[:, s]) * gn) * _silu(z[:, s]))
    on = jnp.concatenate(parts, axis=1).astype(BF16)
    y = _dot(on, wout_ref[...])
    x1, hn2, route = _post_mixer(x_ref[0], y, mod_ref[0], nf_ref[...], wr_ref[...], br_ref[...], cnt_ref)
    x1_ref[0] = x1
    _store_token_major(hn2_ref, (0,), hn2)
    route_ref[0] = route


def _gdn_out(o, z, x, mod, gn, wout, nf, wr, br, tm):
    b, l, d = x.shape
    row = lambda shape: pl.BlockSpec(shape, lambda i, j: (i, j, 0))
    return pl.pallas_call(
        _gdn_out_kernel,
        grid=(b, l // tm),
        in_specs=[
            row((1, tm, GDN_V_W)), row((1, tm, GDN_V_W)), row((1, tm, d)),
            pl.BlockSpec((1, 6, d), lambda i, j: (i, 0, 0)),
            _const_spec((1, GDN_DV)), _const_spec(wout.shape), _const_spec((1, d)),
            _const_spec(wr.shape), _const_spec((1, LANES)),
        ],
        out_specs=[row((1, tm, d)), _tok_spec(tm), row((1, tm, LANES)), _const_spec((1, LANES))],
        out_shape=[jax.ShapeDtypeStruct((b, l, d), F32), jax.ShapeDtypeStruct((b, l, d // LANES, LANES), F32),
                   jax.ShapeDtypeStruct((b, l, LANES), F32), jax.ShapeDtypeStruct((1, LANES), F32)],
        compiler_params=_params("arbitrary", "arbitrary"),
        name="gdn_out",
    )(o, z, x, mod, gn, wout, nf, wr, br)


def _gelu_tanh(x):
    return 0.5 * x * (1.0 + jnp.tanh(math.sqrt(2.0 / math.pi) * (x + 0.044715 * (x * x * x))))


def _lru_kernel(dest_ref, xa_ref, ys_hbm, rprev_ref, mprev_ref,
                mod_ref, ng_ref, win_ref, cw_ref, cb_ref, wax_ref, ba_ref, bx_ref, lam_ref, wout_ref,
                hist0_ref, h0_ref, nf_ref, wr_ref, br_ref,
                x1_ref, hn2_ref, route_ref, hist_ref, hlast_ref, cnt_ref, xp_ref, h_ref, ybuf_ref, sem_ref, *, tm):
    l = pl.program_id(1)
    _init_counts(cnt_ref)
    gather = _PairGather(dest_ref, ys_hbm, ybuf_ref, sem_ref, tm, n_pieces=8)

    @pl.when(l == 0)
    def _():
        xp_ref[0:8, :] = jnp.zeros((8, LRU_WIDTH), F32)
        xp_ref[5:8, :] = hist0_ref[0]
        h_ref[...] = h0_ref[0]

    ya, yb = gather.start()
    x = _moe_residual(xa_ref[0], ya, yb, rprev_ref[0], mprev_ref[0])
    gather.issue_piece()
    mod = mod_ref[0]
    hn = (_rms(x) * ng_ref[...]) * (1.0 + mod[1:2]) + mod[0:1]
    gather.issue_piece()
    proj = _dot(hn.astype(BF16), win_ref[...])
    gather.issue_piece()
    gate_br = _gelu_tanh(proj[:, :LRU_WIDTH])
    xb = proj[:, LRU_WIDTH:]
    xp_ref[8:8 + tm, :] = xb
    cw = cw_ref[...]
    xc = cw[3:4] * xb
    for j in range(CONV_W - 1):
        xc = xc + cw[j:j + 1] * xp_ref[5 + j:5 + j + tm, :]
    xc = xc + cb_ref[...]
    new_hist = xp_ref[tm + 5:tm + 8, :]
    xp_ref[5:8, :] = new_hist
    hist_ref[0] = new_hist
    gather.issue_piece()
    xcb = xc.astype(BF16)
    ra, ia = [], []
    for h in range(LRU_BLOCKS):
        s = slice(h * LRU_BLOCK_W, (h + 1) * LRU_BLOCK_W)
        r2 = _dot(xcb[:, s], wax_ref[h])
        ra.append(r2[:, :LRU_BLOCK_W])
        ia.append(r2[:, LRU_BLOCK_W:])
    r = _sigmoid(jnp.concatenate(ra, axis=1) + ba_ref[...])
    i = _sigmoid(jnp.concatenate(ia, axis=1) + bx_ref[...])
    gather.issue_piece()
    log_a = (-LRU_C * r) * _softplus(-lam_ref[...])
    a = jnp.exp(log_a)
    mult = jnp.sqrt(-_expm1(2.0 * log_a))
    b = mult * (i * xc)
    sub = lax.broadcasted_iota(jnp.int32, (tm, LRU_WIDTH), 0) % SUBLANES
    sft = 1
    while sft < SUBLANES:
        keep = sub >= sft
        a_prev = jnp.where(keep, pltpu.roll(a, sft, 0), 1.0)
        b_prev = jnp.where(keep, pltpu.roll(b, sft, 0), 0.0)
        b = a * b_prev + b
        a = a * a_prev
        sft *= 2
    gather.issue_piece()
    h = h_ref[...]
    groups = []
    for g in range(tm // SUBLANES):
        rows = slice(g * SUBLANES, (g + 1) * SUBLANES)
        hg = b[rows] + a[rows] * h
        groups.append(hg)
        h = hg[SUBLANES - 1:SUBLANES, :]
    hs = jnp.concatenate(groups, axis=0)
    h_last = h
    h_ref[...] = h_last
    hlast_ref[0] = h_last
    gather.issue_piece()
    y = _dot((hs * gate_br).astype(BF16), wout_ref[...])
    gather.issue_piece()
    x1, hn2, route = _post_mixer(x, y, mod, nf_ref[...], wr_ref[...], br_ref[...], cnt_ref)
    x1_ref[0] = x1
    _store_token_major(hn2_ref, (0,), hn2)
    route_ref[0] = route
    gather.finish()


def _lru_layer(dest, xa, ys, rprev, mprev, mod, ng, win, cw, cb, wax, ba, bx, lam, wout, hist0, h0, nf, wr, br, tm):
    b, l, d = xa.shape
    row = lambda shape: pl.BlockSpec(shape, lambda i, j, *_: (i, j, 0))
    per_b = lambda shape: pl.BlockSpec(shape, lambda i, j, *_: (i, 0, 0))
    vec = _const_spec((1, d))
    grid_spec = pltpu.PrefetchScalarGridSpec(
        num_scalar_prefetch=1,
        grid=(b, l // tm),
        in_specs=[
            row((1, tm, d)), pl.BlockSpec(memory_space=pl.ANY), row((1, tm, LANES)), per_b((1, 6, d)),
            per_b((1, 6, d)), vec, _const_spec(win.shape), _const_spec(cw.shape), vec,
            _const_spec(wax.shape), vec, vec, vec, _const_spec(wout.shape),
            per_b((1, CONV_W - 1, LRU_WIDTH)), per_b((1, 1, LRU_WIDTH)), vec,
            _const_spec(wr.shape), _const_spec((1, LANES)),
        ],
        out_specs=[row((1, tm, d)), _tok_spec(tm), row((1, tm, LANES)),
                   per_b((1, CONV_W - 1, LRU_WIDTH)), per_b((1, 1, LRU_WIDTH)), _const_spec((1, LANES))],
        scratch_shapes=[pltpu.VMEM((tm + 8, LRU_WIDTH), F32), pltpu.VMEM((1, LRU_WIDTH), F32),
                        pltpu.VMEM((2, 2 * tm, d // LANES, LANES), F32), pltpu.SemaphoreType.DMA((2,))],
    )
    return pl.pallas_call(
        functools.partial(_lru_kernel, tm=tm),
        grid_spec=grid_spec,
        out_shape=[jax.ShapeDtypeStruct((b, l, d), F32), jax.ShapeDtypeStruct((b, l, d // LANES, LANES), F32),
                   jax.ShapeDtypeStruct((b, l, LANES), F32),
                   jax.ShapeDtypeStruct((b, CONV_W - 1, LRU_WIDTH), F32),
                   jax.ShapeDtypeStruct((b, 1, LRU_WIDTH), F32), jax.ShapeDtypeStruct((1, LANES), F32)],
        compiler_params=_params("arbitrary", "arbitrary"),
        name="lru_layer",
    )(dest, xa, ys, rprev, mprev, mod, ng, win, cw, cb, wax, ba, bx, lam, wout, hist0, h0, nf, wr, br)


def _moe_kernel(be_ref, st_ref, x_hbm, w1_ref, w3_ref, w2_ref, y_ref, w1b_ref, w3b_ref, w2b_ref, xbuf_ref, sem_ref):
    i = pl.program_id(0)
    last = pl.num_programs(0) - 1
    slot = i % 2

    def row_copy(blk, r, buf):
        tok = st_ref[blk * MOE_BLOCK + r]
        return pltpu.make_async_copy(x_hbm.at[tok], xbuf_ref.at[buf, r], sem_ref.at[buf])

    def wait_block(buf):
        pltpu.make_async_copy(x_hbm.at[pl.ds(0, MOE_BLOCK)], xbuf_ref.at[buf], sem_ref.at[buf]).wait()

    @pl.when(i == 0)
    def _():
        for r in range(MOE_BLOCK):
            row_copy(0, r, 0).start(priority=GATHER_PRIORITY)

    prev = be_ref[jnp.maximum(i - 1, 0)]

    @pl.when((i == 0) | (be_ref[i] != prev))
    def _():
        w1b_ref[...] = w1_ref[0, 0].astype(BF16)
        w3b_ref[...] = w3_ref[0, 0].astype(BF16)
        w2b_ref[...] = w2_ref[0, 0].astype(BF16)

    nxt = jnp.minimum(i + 1, last)

    def body(buf):
        wait_block(buf)
        n_piece = 6
        per = MOE_BLOCK // n_piece + 1
        issued = [0]

        def issue_some():
            hi = min(issued[0] + per, MOE_BLOCK)
            for r in range(issued[0], hi):
                row_copy(nxt, r, 1 - buf).start(priority=GATHER_PRIORITY)
            issued[0] = hi

        xb = _load_token_major(xbuf_ref, (buf,)).astype(BF16)
        half = D_FF_EXPERT // 2
        hid = []
        for c in range(2):
            cols = slice(c * half, (c + 1) * half)
            issue_some()
            h1 = _dot(xb, w1b_ref[:, cols])
            issue_some()
            h3 = _dot(xb, w3b_ref[:, cols])
            hid.append((_silu(h1) * h3).astype(BF16))
        issue_some()
        y = _dot(hid[0], w2b_ref[0:half, :])
        issue_some()
        _store_token_major(y_ref, (), y + _dot(hid[1], w2b_ref[half:, :]))
        assert issued[0] == MOE_BLOCK

        @pl.when(i == last)
        def _():
            wait_block(1 - buf)

    for buf in range(2):
        pl.when(slot == buf)(functools.partial(body, buf))


def _moe_blocks(blk_expert, slot_tok, x, w1, w3, w2, layer):
    d = D_MODEL
    p = slot_tok.shape[0]
    nb = p // MOE_BLOCK
    grid_spec = pltpu.PrefetchScalarGridSpec(
        num_scalar_prefetch=2,
        grid=(nb,),
        in_specs=[
            pl.BlockSpec(memory_space=pl.ANY),
            pl.BlockSpec((1, 1, d, D_FF_EXPERT), lambda i, be, st: (layer, be[i], 0, 0)),
            pl.BlockSpec((1, 1, d, D_FF_EXPERT), lambda i, be, st: (layer, be[i], 0, 0)),
            pl.BlockSpec((1, 1, D_FF_EXPERT, d), lambda i, be, st: (layer, be[i], 0, 0)),
        ],
        out_specs=pl.BlockSpec((MOE_BLOCK, d // LANES, LANES), lambda i, be, st: (i, 0, 0)),
        scratch_shapes=[pltpu.VMEM((d, D_FF_EXPERT), BF16), pltpu.VMEM((d, D_FF_EXPERT), BF16),
                        pltpu.VMEM((D_FF_EXPERT, d), BF16), pltpu.VMEM((2, MOE_BLOCK, d // LANES, LANES), F32),
                        pltpu.SemaphoreType.DMA((2,))],
    )
    return pl.pallas_call(
        _moe_kernel,
        grid_spec=grid_spec,
        out_shape=jax.ShapeDtypeStruct((p, d // LANES, LANES), F32),
        compiler_params=_params("arbitrary"),
        name="moe_blocks",
    )(blk_expert, slot_tok, x, w1, w3, w2)


def _moe_residual(x1, ya, yb, route, mod):
    return x1 + mod[5:6] * (ya * route[:, 2:3] + yb * route[:, 3:4])


def _final_kernel(dest_ref, x1_ref, ys_hbm, route_ref, mod_ref, no_ref, o_ref, ybuf_ref, sem_ref, *, tm):
    gather = _PairGather(dest_ref, ys_hbm, ybuf_ref, sem_ref, tm, n_pieces=1)
    ya, yb = gather.start()
    gather.issue_piece()
    x2 = _moe_residual(x1_ref[0], ya, yb, route_ref[0], mod_ref[0])
    o_ref[0] = _rms(x2) * no_ref[...]
    gather.finish()


def _final(dest, x1, ys, route, mod, norm_out, tm):
    b, l, d = x1.shape
    row = lambda shape: pl.BlockSpec(shape, lambda i, j, *_: (i, j, 0))
    grid_spec = pltpu.PrefetchScalarGridSpec(
        num_scalar_prefetch=1,
        grid=(b, l // tm),
        in_specs=[row((1, tm, d)), pl.BlockSpec(memory_space=pl.ANY), row((1, tm, LANES)),
                  pl.BlockSpec((1, 6, d), lambda i, j, *_: (i, 0, 0)), _const_spec((1, d))],
        out_specs=row((1, tm, d)),
        scratch_shapes=[pltpu.VMEM((2, 2 * tm, d // LANES, LANES), F32), pltpu.SemaphoreType.DMA((2,))],
    )
    return pl.pallas_call(
        functools.partial(_final_kernel, tm=tm),
        grid_spec=grid_spec,
        out_shape=jax.ShapeDtypeStruct((b, l, d), F32),
        compiler_params=_params("arbitrary", "arbitrary"),
        name="final",
    )(dest, x1, ys, route, mod, norm_out)


def _moe_experts(hn2, route, counts, w1, w3, w2, layer):
    b, l = hn2.shape[:2]
    d = D_MODEL
    n = b * l
    a = n * EXPERT_TOPK
    rt = route.reshape(n, LANES)
    e_ab = rt[:, 0:EXPERT_TOPK].astype(jnp.int32)
    r_ab = rt[:, 4:4 + EXPERT_TOPK].astype(jnp.int32)
    counts = counts[0, :N_EXPERTS].astype(jnp.int32)
    padded = (counts + MOE_BLOCK - 1) // MOE_BLOCK * MOE_BLOCK
    pad_end = jnp.cumsum(padded)
    pad_start = pad_end - padded
    dest = pad_start[e_ab] + r_ab
    n_blocks = -(-a // MOE_BLOCK) + N_EXPERTS
    p = n_blocks * MOE_BLOCK
    tok = jnp.broadcast_to(jnp.arange(n, dtype=jnp.int32)[:, None], (n, EXPERT_TOPK))
    slot_tok = (jnp.arange(p, dtype=jnp.int32) % n).at[dest.reshape(-1)].set(tok.reshape(-1))
    blk_first = jnp.arange(n_blocks, dtype=jnp.int32) * MOE_BLOCK
    blk_expert = jnp.minimum(jnp.sum((pad_end[None, :] <= blk_first[:, None]).astype(jnp.int32), axis=1),
                             N_EXPERTS - 1)
    ys = _moe_blocks(blk_expert, slot_tok, hn2.reshape(n, d // LANES, LANES), w1, w3, w2, layer)
    return ys, jnp.concatenate([dest[:, 0], dest[:, 1]])


def _trunk(x, mods, gdn_s, gdn_conv, lru_h, lru_conv, wp):
    b, l, d = x.shape
    tm = min(ROW_TILE, l)
    chunk = min(GDN_CHUNK, l)
    assert l % tm == 0 and tm % chunk == 0 and chunk & (chunk - 1) == 0
    mod = mods[0]
    q, k, v, z, gbc, gbr, gconv_new = _gdn_in(x, mod, wp['norm_mix0'], wp['gdn_wqkv'], wp['gdn_wz'], wp['gdn_wab'],
                                              wp['gdn_conv_w'], gdn_conv, wp['gdn_alog'], wp['gdn_dtb'], tm, chunk)
    o, s_new = _gdn_chunks(q, k, v, gbc, gbr, gdn_s, tm, chunk)
    x1, hn2, route, counts = _gdn_out(o, z, x, mod, wp['gdn_norm'], wp['gdn_wout'], wp['norm_ffn0'],
                                      wp['wr0'], wp['br0'], tm)
    ys, dest = _moe_experts(hn2, route, counts, wp['moe_w1'], wp['moe_w3'], wp['moe_w2'], 0)
    x1, hn2, route, lconv_new, h_new, counts = _lru_layer(
        dest, x1, ys, route, mod, mods[1], wp['norm_mix1'], wp['lru_win'], wp['lru_conv_w'], wp['lru_conv_b'],
        wp['lru_wax'], wp['lru_ba'], wp['lru_bx'], wp['lru_lam'], wp['lru_wout'], lru_conv, lru_h, wp['norm_ffn1'],
        wp['wr1'], wp['br1'], tm)
    ys, dest = _moe_experts(hn2, route, counts, wp['moe_w1'], wp['moe_w3'], wp['moe_w2'], 1)
    y = _final(dest, x1, ys, route, mods[1], wp['norm_out'], tm)
    return y, s_new[None], gconv_new[None], h_new.reshape(1, b, LRU_WIDTH), lconv_new[None]


def _pad_lanes(v, width=LANES):
    v = v.reshape(1, -1)
    return jnp.pad(v, ((0, 0), (0, width - v.shape[1])))


def _router_weights(w_rg, b_rg, w_re, b_re):
    w = jnp.pad(jnp.concatenate([w_rg, w_re], axis=1), ((0, 0), (0, LANES - N_GROUPS - N_EXPERTS)))
    return w.astype(BF16), _pad_lanes(jnp.concatenate([b_rg, b_re]))


def kernel(x_prompt, x_sample, state_gdn_S, state_gdn_conv, state_lru_h, state_lru_conv, c_prompt, c_sample, w_ada, b_ada, norm_mix, norm_ffn, norm_out, gdn_w_in, gdn_conv_w, gdn_a_log, gdn_dt_bias, gdn_norm, gdn_w_out, lru_w_in, lru_conv_w, lru_conv_b, lru_w_a, lru_b_a, lru_w_x, lru_b_x, lru_lambda, lru_w_out, moe_w_rg, moe_b_rg, moe_w_re, moe_b_re, moe_w1, moe_w3, moe_w2):
    d = D_MODEL
    bp = x_prompt.shape[0]
    bs = x_sample.shape[0]
    win = gdn_w_in[0]
    wab = jnp.pad(win[:, GDN_CONV_CH + GDN_V_W:], ((0, 0), (0, LANES - 2 * GDN_HEADS)))
    wp = dict(
        norm_mix0=norm_mix[0].reshape(1, d), norm_mix1=norm_mix[1].reshape(1, d),
        norm_ffn0=norm_ffn[0].reshape(1, d), norm_ffn1=norm_ffn[1].reshape(1, d),
        norm_out=norm_out.reshape(1, d),
        gdn_wqkv=win[:, :GDN_CONV_CH].astype(BF16),
        gdn_wz=win[:, GDN_CONV_CH:GDN_CONV_CH + GDN_V_W].astype(BF16),
        gdn_wab=wab.astype(BF16),
        gdn_conv_w=gdn_conv_w[0],
        gdn_alog=_pad_lanes(gdn_a_log[0]), gdn_dtb=_pad_lanes(gdn_dt_bias[0]),
        gdn_norm=gdn_norm[0].reshape(1, GDN_DV), gdn_wout=gdn_w_out[0].astype(BF16),
        lru_win=lru_w_in[0].astype(BF16), lru_conv_w=lru_conv_w[0], lru_conv_b=lru_conv_b[0].reshape(1, d),
        lru_wax=jnp.concatenate([lru_w_a[0], lru_w_x[0]], axis=-1).astype(BF16),
        lru_ba=lru_b_a[0].reshape(1, d), lru_bx=lru_b_x[0].reshape(1, d), lru_lam=lru_lambda[0].reshape(1, d),
        lru_wout=lru_w_out[0].astype(BF16),
        moe_w1=moe_w1, moe_w3=moe_w3, moe_w2=moe_w2,
    )
    for i in range(DEPTH):
        wp[f'wr{i}'], wp[f'br{i}'] = _router_weights(moe_w_rg[i], moe_b_rg[i], moe_w_re[i], moe_b_re[i])

    mods = _ada_mod(jnp.concatenate([c_prompt, c_sample], axis=0), w_ada, b_ada)
    mods = mods.reshape(DEPTH, bp + bs, 6, d)
    mods_p = [mods[i, :bp] for i in range(DEPTH)]
    mods_s = [mods[i, bp:] for i in range(DEPTH)]

    dt = x_prompt.dtype
    z_s = jnp.zeros((bp, GDN_HEADS, GDN_DK, GDN_DV), dt)
    z_gc = jnp.zeros((bp, CONV_W - 1, GDN_CONV_CH), dt)
    z_h = jnp.zeros((bp, 1, LRU_WIDTH), dt)
    z_lc = jnp.zeros((bp, CONV_W - 1, LRU_WIDTH), dt)
    y_p, gs_p, gc_p, lh_p, lc_p = _trunk(x_prompt, mods_p, z_s, z_gc, z_h, z_lc, wp)
    y_s, gs_s, gc_s, lh_s, lc_s = _trunk(x_sample, mods_s, state_gdn_S[0], state_gdn_conv[0],
                                         state_lru_h[0].reshape(bs, 1, LRU_WIDTH), state_lru_conv[0], wp)
    return (y_p, y_s, gs_p, gc_p, lh_p, lc_p, gs_s, gc_s, lh_s, lc_s)
```

```python
import functools
import math

import jax
import jax.numpy as jnp
from jax import lax
from jax.experimental import pallas as pl
from jax.experimental.pallas import tpu as pltpu

F32 = jnp.float32
BF16 = jnp.bfloat16

D_MODEL = 1024
DEPTH = 2
CONV_W = 4
NORM_EPS = 1e-6
GDN_HEADS = 8
GDN_DK = 128
GDN_DV = 128
GDN_QK_W = GDN_HEADS * GDN_DK
GDN_V_W = GDN_HEADS * GDN_DV
GDN_CONV_CH = 2 * GDN_QK_W + GDN_V_W
GDN_CHUNK = 64
GDN_INV_BLOCK = 16
LRU_WIDTH = D_MODEL
LRU_BLOCKS = 8
LRU_BLOCK_W = LRU_WIDTH // LRU_BLOCKS
LRU_C = 8.0
N_GROUPS = 4
EXPERTS_PER_GROUP = 8
N_EXPERTS = N_GROUPS * EXPERTS_PER_GROUP
EXPERT_TOPK = 2
D_FF_EXPERT = 512
MOE_BLOCK = 256
LANES = 128
SUBLANES = 8
ROW_TILE = 256
VMEM_LIMIT = 56 * 1024 * 1024


def _dot(a, b):
    return jnp.dot(a, b, preferred_element_type=F32)


def _dot_nt(a, b):
    return lax.dot_general(a, b, (((1,), (1,)), ((), ())), preferred_element_type=F32)


def _dot_tn(a, b):
    return lax.dot_general(a, b, (((0,), (0,)), ((), ())), preferred_element_type=F32)


def _split3(x):
    a = x.astype(BF16)
    r = x - a.astype(F32)
    b = r.astype(BF16)
    c = (r - b.astype(F32)).astype(BF16)
    return a, b, c


def _rms(x):
    return x * lax.rsqrt(jnp.mean(x * x, axis=-1, keepdims=True) + NORM_EPS)


def _sigmoid(x):
    return 1.0 / (1.0 + jnp.exp(-x))


def _silu(x):
    return x * _sigmoid(x)


def _softplus(x):
    return jnp.maximum(x, 0.0) + jnp.log1p(jnp.exp(-jnp.abs(x)))


def _expm1(x):
    u = jnp.exp(x)
    near = (u > 0.5) & (u < 2.0) & (u != 1.0)
    corrected = (u - 1.0) * x / jnp.where(near, jnp.log(u), 1.0)
    return jnp.where(u == 1.0, x, jnp.where(near, corrected, u - 1.0))


def _params(*sem):
    return pltpu.CompilerParams(dimension_semantics=sem, vmem_limit_bytes=VMEM_LIMIT)


def _const_spec(shape):
    nd = len(shape)
    return pl.BlockSpec(shape, lambda *_: (0,) * nd)


def _ada_kernel(c_ref, w_ref, b_ref, o_ref):
    s = _silu(c_ref[...]).astype(BF16)
    o_ref[0] = _dot(s, w_ref[0].astype(BF16)) + b_ref[0]


def _ada_mod(c_all, w_ada, b_ada):
    bt = c_all.shape[0]
    tn = 1536
    n6 = 6 * D_MODEL
    return pl.pallas_call(
        _ada_kernel,
        grid=(DEPTH, n6 // tn),
        in_specs=[
            pl.BlockSpec((bt, D_MODEL), lambda i, j: (0, 0)),
            pl.BlockSpec((1, D_MODEL, tn), lambda i, j: (i, 0, j)),
            pl.BlockSpec((1, 1, tn), lambda i, j: (i, 0, j)),
        ],
        out_specs=pl.BlockSpec((1, bt, tn), lambda i, j: (i, 0, j)),
        out_shape=jax.ShapeDtypeStruct((DEPTH, bt, n6), F32),
        compiler_params=_params("arbitrary", "arbitrary"),
        name="ada_mod",
    )(c_all, w_ada, b_ada.reshape(DEPTH, 1, n6))


def _route(hn2, wr, br, cnt_ref):
    logits = _dot(hn2.astype(BF16), wr) + br
    lane = lax.broadcasted_iota(jnp.int32, logits.shape, 1)
    neg = jnp.float32(-jnp.inf)
    big = jnp.int32(1 << 20)
    is_g = lane < N_GROUPS
    lg = jnp.where(is_g, logits, neg)
    eg = jnp.exp(lg - jnp.max(lg, axis=-1, keepdims=True))
    pg = eg / jnp.sum(eg, axis=-1, keepdims=True)
    pg = jnp.where(is_g, pg, -1.0)
    gate_g = jnp.max(pg, axis=-1, keepdims=True)
    grp = jnp.min(jnp.where(pg == gate_g, lane, big), axis=-1, keepdims=True)
    lo = N_GROUPS + grp * EXPERTS_PER_GROUP
    is_e = (lane >= lo) & (lane < lo + EXPERTS_PER_GROUP)
    le = jnp.where(is_e, logits, neg)
    ee = jnp.exp(le - jnp.max(le, axis=-1, keepdims=True))
    pe = ee / jnp.sum(ee, axis=-1, keepdims=True)
    pe = jnp.where(is_e, pe, -1.0)
    p1 = jnp.max(pe, axis=-1, keepdims=True)
    i1 = jnp.min(jnp.where(pe == p1, lane, big), axis=-1, keepdims=True)
    pe2 = jnp.where(lane == i1, -1.0, pe)
    p2 = jnp.max(pe2, axis=-1, keepdims=True)
    i2 = jnp.min(jnp.where(pe2 == p2, lane, big), axis=-1, keepdims=True)
    tot = p1 + p2
    w1 = gate_g * (p1 / tot)
    w2 = gate_g * (p2 / tot)
    e1 = i1 - N_GROUPS
    e2 = i2 - N_GROUPS
    tm = logits.shape[0]
    oh1 = lane == e1
    oh2 = lane == e2
    cnt = jnp.where(oh1 | oh2, 1.0, 0.0)
    ri = lax.broadcasted_iota(jnp.int32, (tm, tm), 0)
    ci = lax.broadcasted_iota(jnp.int32, (tm, tm), 1)
    before = jnp.where(ci < ri, 1.0, 0.0).astype(BF16)
    pos = _dot(before, cnt.astype(BF16)) + cnt_ref[...]
    r1 = jnp.sum(jnp.where(oh1, pos, 0.0), axis=-1, keepdims=True)
    r2 = jnp.sum(jnp.where(oh2, pos, 0.0), axis=-1, keepdims=True)
    cnt_ref[...] = cnt_ref[...] + jnp.sum(cnt, axis=0, keepdims=True)
    vals = (e1.astype(F32), e2.astype(F32), w1, w2, r1, r2)
    out = jnp.zeros(logits.shape, F32)
    for j, val in enumerate(vals):
        out = jnp.where(lane == j, val, out)
    return out


def _tok_spec(tm):
    return pl.BlockSpec((1, tm, D_MODEL // LANES, LANES), lambda i, j, *_: (i, j, 0, 0))


def _store_token_major(ref, lead, x):
    for c in range(D_MODEL // LANES):
        ref[lead + (slice(None), c, slice(None))] = x[:, c * LANES:(c + 1) * LANES]


def _load_token_major(ref, lead, rows=slice(None)):
    return jnp.concatenate([ref[lead + (rows, c, slice(None))] for c in range(D_MODEL // LANES)], axis=1)


N_DMA_QUEUES = 2


class _PairGather:
    def __init__(self, dest_ref, ys_hbm, buf_ref, sem_ref, tm, n_pieces):
        self.dest_ref, self.ys_hbm, self.buf_ref, self.sem_ref, self.tm = dest_ref, ys_hbm, buf_ref, sem_ref, tm
        n_steps = pl.num_programs(0) * pl.num_programs(1)
        self.n = n_steps * tm
        self.step = pl.program_id(0) * pl.num_programs(1) + pl.program_id(1)
        self.last = n_steps - 1
        self.slot = self.step % 2
        self.nxt = jnp.minimum(self.step + 1, self.last)
        self.per = -(-2 * tm // n_pieces)
        self.issued = 0

    def _copy(self, tile, j, buf):
        k, r = divmod(j, self.tm)
        row = self.dest_ref[k * self.n + tile * self.tm + r]
        return pltpu.make_async_copy(self.ys_hbm.at[row], self.buf_ref.at[buf, j], self.sem_ref.at[buf])

    def _wait(self, buf):
        pltpu.make_async_copy(self.ys_hbm.at[pl.ds(0, 2 * self.tm)], self.buf_ref.at[buf], self.sem_ref.at[buf]).wait()

    def start(self):
        @pl.when(self.step == 0)
        def _():
            for j in range(2 * self.tm):
                self._copy(0, j, 0).start(priority=j % N_DMA_QUEUES)

        self._wait(self.slot)
        ya = _load_token_major(self.buf_ref, (self.slot,), slice(0, self.tm))
        yb = _load_token_major(self.buf_ref, (self.slot,), slice(self.tm, 2 * self.tm))
        return ya, yb

    def issue_piece(self):
        hi = min(self.issued + self.per, 2 * self.tm)
        for j in range(self.issued, hi):
            self._copy(self.nxt, j, 1 - self.slot).start(priority=j % N_DMA_QUEUES)
        self.issued = hi

    def finish(self):
        assert self.issued == 2 * self.tm

        @pl.when(self.step == self.last)
        def _():
            self._wait(1 - self.slot)


def _init_counts(cnt_ref):
    @pl.when((pl.program_id(0) == 0) & (pl.program_id(1) == 0))
    def _():
        cnt_ref[...] = jnp.zeros(cnt_ref.shape, F32)


def _post_mixer(x, y, mod, nf, wr, br, cnt_ref):
    g1 = mod[2:3]
    sh2 = mod[3:4]
    sc2 = mod[4:5]
    x1 = x + g1 * y
    hn2 = (_rms(x1) * nf) * (1.0 + sc2) + sh2
    return x1, hn2, _route(hn2, wr, br, cnt_ref)


def _gdn_in_kernel(x_ref, mod_ref, ng_ref, wqkv_ref, wz_ref, wab_ref, cw_ref, hist0_ref, alog_ref, dtb_ref,
                   q_ref, k_ref, v_ref, z_ref, gbc_ref, gbr_ref, hist_ref, xp_ref, *, tm, chunk):
    l = pl.program_id(1)

    @pl.when(l == 0)
    def _():
        xp_ref[0:8, :] = jnp.zeros((8, GDN_CONV_CH), F32)
        xp_ref[5:8, :] = hist0_ref[0]

    x = x_ref[0]
    mod = mod_ref[0]
    hn = (_rms(x) * ng_ref[...]) * (1.0 + mod[1:2]) + mod[0:1]
    hb = hn.astype(BF16)
    qkv = _dot(hb, wqkv_ref[...])
    xp_ref[8:8 + tm, :] = qkv
    cw = cw_ref[...]
    y = cw[3:4] * qkv
    for j in range(CONV_W - 1):
        y = y + cw[j:j + 1] * xp_ref[5 + j:5 + j + tm, :]
    new_hist = xp_ref[tm + 5:tm + 8, :]
    xp_ref[5:8, :] = new_hist
    hist_ref[0] = new_hist
    y = _silu(y)
    for h in range(GDN_HEADS):
        s = slice(h * GDN_DK, (h + 1) * GDN_DK)
        qh = y[:, s]
        q_ref[0, :, s] = qh * lax.rsqrt(jnp.sum(qh * qh, axis=-1, keepdims=True) + 1e-6) * (GDN_DK ** -0.5)
        kh = y[:, GDN_QK_W + h * GDN_DK:GDN_QK_W + (h + 1) * GDN_DK]
        k_ref[0, :, s] = kh * lax.rsqrt(jnp.sum(kh * kh, axis=-1, keepdims=True) + 1e-6)
    v_ref[0] = y[:, 2 * GDN_QK_W:]
    z_ref[0] = _dot(hb, wz_ref[...])
    ab = _dot(hb, wab_ref[...])
    g = -jnp.exp(alog_ref[...]) * _softplus(ab + dtb_ref[...])
    beta = _sigmoid(ab)
    ri = lax.broadcasted_iota(jnp.int32, (tm, tm), 0)
    ci = lax.broadcasted_iota(jnp.int32, (tm, tm), 1)
    tri = jnp.where((ri // chunk == ci // chunk) & (ci <= ri), 1.0, 0.0).astype(BF16)
    g1, g2, g3 = _split3(g)
    gcum = (_dot(tri, g1) + _dot(tri, g2)) + _dot(tri, g3)
    lane = lax.broadcasted_iota(jnp.int32, (tm, LANES), 1)
    gb = jnp.where(lane < GDN_HEADS, gcum, beta)
    gbc_ref[0] = gb
    er = lax.broadcasted_iota(jnp.int32, (2 * GDN_HEADS, LANES), 0)
    ec = lax.broadcasted_iota(jnp.int32, (2 * GDN_HEADS, LANES), 1)
    sel = jnp.where(er == ec, 1.0, 0.0).astype(BF16)
    b1, b2, b3 = _split3(gb)
    for n in range(tm // chunk):
        r = slice(n * chunk, (n + 1) * chunk)
        gbr_ref[0, n] = (_dot_nt(sel, b1[r]) + _dot_nt(sel, b2[r])) + _dot_nt(sel, b3[r])


def _gdn_in(x, mod, ng, wqkv, wz, wab, cw, hist0, alog, dtb, tm, chunk):
    b, l, d = x.shape
    grid = (b, l // tm)
    row = lambda shape: pl.BlockSpec(shape, lambda i, j: (i, j, 0))
    outs = pl.pallas_call(
        functools.partial(_gdn_in_kernel, tm=tm, chunk=chunk),
        grid=grid,
        in_specs=[
            row((1, tm, d)),
            pl.BlockSpec((1, 6, d), lambda i, j: (i, 0, 0)),
            _const_spec((1, d)),
            _const_spec(wqkv.shape),
            _const_spec(wz.shape),
            _const_spec(wab.shape),
            _const_spec(cw.shape),
            pl.BlockSpec((1, CONV_W - 1, GDN_CONV_CH), lambda i, j: (i, 0, 0)),
            _const_spec((1, LANES)),
            _const_spec((1, LANES)),
        ],
        out_specs=[
            row((1, tm, GDN_QK_W)), row((1, tm, GDN_QK_W)), row((1, tm, GDN_V_W)), row((1, tm, GDN_V_W)),
            row((1, tm, LANES)),
            pl.BlockSpec((1, tm // chunk, 2 * GDN_HEADS, chunk), lambda i, j: (i, j, 0, 0)),
            pl.BlockSpec((1, CONV_W - 1, GDN_CONV_CH), lambda i, j: (i, 0, 0)),
        ],
        out_shape=[
            jax.ShapeDtypeStruct((b, l, GDN_QK_W), F32), jax.ShapeDtypeStruct((b, l, GDN_QK_W), F32),
            jax.ShapeDtypeStruct((b, l, GDN_V_W), F32), jax.ShapeDtypeStruct((b, l, GDN_V_W), F32),
            jax.ShapeDtypeStruct((b, l, LANES), F32),
            jax.ShapeDtypeStruct((b, l // chunk, 2 * GDN_HEADS, chunk), F32),
            jax.ShapeDtypeStruct((b, CONV_W - 1, GDN_CONV_CH), F32),
        ],
        scratch_shapes=[pltpu.VMEM((tm + 8, GDN_CONV_CH), F32)],
        compiler_params=_params("arbitrary", "arbitrary"),
        name="gdn_in",
    )(x, mod, ng, wqkv, wz, wab, cw, hist0, alog, dtb)
    return outs


def _gdn_chunk_kernel(q_ref, k_ref, v_ref, gbc_ref, gbr_ref, s0_ref, o_ref, sout_ref, s_ref, *, tm, chunk):
    l = pl.program_id(1)

    @pl.when(l == 0)
    def _():
        s_ref[...] = s0_ref[0]

    c2 = 2 * chunk
    n_pairs = GDN_HEADS // 2
    ri = lax.broadcasted_iota(jnp.int32, (chunk, c2), 0)
    cl = lax.broadcasted_iota(jnp.int32, (chunk, c2), 1)
    ci = cl % chunk
    left = cl < chunk
    incl = ri >= ci
    strict = ri > ci
    eye = jnp.where(ri == ci, 1.0, 0.0).astype(F32)
    base = min(GDN_INV_BLOCK, chunk)
    levels = int(math.log2(base))
    diag_blk = (ri // base) == (ci // base)
    merge_masks = []
    blk = base
    while blk < chunk:
        merge_masks.append(((ri // (2 * blk)) == (ci // (2 * blk))) & ((ri // blk) % 2 == 1) & ((ci // blk) % 2 == 0))
        blk *= 2
    heads = range(GDN_HEADS)
    pairs = range(n_pairs)

    def bdiag(x):
        return jnp.concatenate([jnp.where(left, x, 0.0), jnp.where(left, 0.0, x)], axis=0).astype(BF16)

    def bdiag2(x0, x1):
        z = jnp.zeros_like(x0)
        return jnp.concatenate([jnp.concatenate([x0, z], axis=1), jnp.concatenate([z, x1], axis=1)], axis=0).astype(BF16)

    def one_chunk(n, carry):
        r0 = pl.multiple_of(n * chunk, chunk)
        rows = pl.ds(r0, chunk)
        gbc = gbc_ref[0, rows, :]
        gbr = gbr_ref[0, n]
        hs = [slice(h * GDN_DK, (h + 1) * GDN_DK) for h in heads]
        q = [q_ref[0, rows, hs[h]] for h in heads]
        k = [k_ref[0, rows, hs[h]] for h in heads]
        v = [v_ref[0, rows, hs[h]] for h in heads]
        gc = [gbc[:, h:h + 1] for h in heads]
        bc = [gbc[:, GDN_HEADS + h:GDN_HEADS + h + 1] for h in heads]
        kb = [k[h] * bc[h] for h in heads]
        a2 = [_dot_nt(jnp.concatenate([q[h], kb[h]], axis=0).astype(BF16), k[h].astype(BF16)) for h in heads]
        attn, m = [], []
        for hp in pairs:
            h0, h1 = 2 * hp, 2 * hp + 1
            gcp = jnp.where(left, gc[h0], gc[h1])
            grp = jnp.concatenate([gbr[h0:h0 + 1, :], gbr[h1:h1 + 1, :]], axis=1)
            decay = jnp.where(incl, jnp.exp(jnp.where(incl, gcp - grp, 0.0)), 0.0)
            ap = jnp.concatenate([a2[h0], a2[h1]], axis=1)
            attn.append(ap[:chunk] * decay)
            m.append(jnp.where(strict, ap[chunk:] * decay, 0.0))
        nq = [jnp.where(diag_blk, -m[hp], 0.0) for hp in pairs]
        p = [eye + nq[hp] for hp in pairs]
        nq = [_dot(nq[hp].astype(BF16), bdiag(nq[hp])) for hp in pairs]
        for j in range(1, levels):
            if j < levels - 1:
                r = [_dot(nq[hp].astype(BF16), jnp.concatenate([bdiag(nq[hp]), bdiag(p[hp])], axis=1)) for hp in pairs]
                nq = [r[hp][:, :c2] for hp in pairs]
                p = [p[hp] + r[hp][:, c2:] for hp in pairs]
            else:
                p = [p[hp] + _dot(nq[hp].astype(BF16), bdiag(p[hp])) for hp in pairs]
        for low in merge_masks:
            t = [_dot(jnp.where(low, m[hp], 0.0).astype(BF16), bdiag(p[hp])) for hp in pairs]
            p = [p[hp] - _dot(p[hp].astype(BF16), bdiag(t[hp])) for hp in pairs]
        eg = [jnp.exp(gc[h]) for h in heads]
        rhs = [jnp.concatenate([kb[h] * eg[h], v[h] * bc[h]], axis=1) for h in heads]
        wu = [_dot(p[hp].astype(BF16), bdiag2(rhs[2 * hp], rhs[2 * hp + 1])) for hp in pairs]
        wd = GDN_DK + GDN_DV
        w = [wu[h // 2][:, (h % 2) * wd:(h % 2) * wd + GDN_DK] for h in heads]
        u = [wu[h // 2][:, (h % 2) * wd + GDN_DK:(h % 2 + 1) * wd] for h in heads]
        st = [s_ref[h] for h in heads]
        ws = [_dot(jnp.concatenate([w[h], q[h] * eg[h]], axis=0).astype(BF16), st[h].astype(BF16)) for h in heads]
        v_new = [u[h] - ws[h][:chunk] for h in heads]
        av = [_dot(attn[hp].astype(BF16), bdiag2(v_new[2 * hp], v_new[2 * hp + 1])) for hp in pairs]
        for h in heads:
            o_ref[0, rows, hs[h]] = ws[h][chunk:] + av[h // 2][:, (h % 2) * GDN_DV:(h % 2 + 1) * GDN_DV]
        for h in heads:
            g_last = gc[h][chunk - 1:chunk, :]
            kd = k[h] * jnp.exp(g_last - gc[h])
            s_ref[h] = st[h] * jnp.exp(g_last) + _dot_tn(kd.astype(BF16), v_new[h].astype(BF16))
        return carry

    lax.fori_loop(0, tm // chunk, one_chunk, 0)
    sout_ref[0] = s_ref[...]


def _gdn_chunks(q, k, v, gbc, gbr, s0, tm, chunk):
    b, l, _ = q.shape
    row = lambda shape: pl.BlockSpec(shape, lambda i, j: (i, j, 0))
    sspec = pl.BlockSpec((1, GDN_HEADS, GDN_DK, GDN_DV), lambda i, j: (i, 0, 0, 0))
    return pl.pallas_call(
        functools.partial(_gdn_chunk_kernel, tm=tm, chunk=chunk),
        grid=(b, l // tm),
        in_specs=[
            row((1, tm, GDN_QK_W)), row((1, tm, GDN_QK_W)), row((1, tm, GDN_V_W)), row((1, tm, LANES)),
            pl.BlockSpec((1, tm // chunk, 2 * GDN_HEADS, chunk), lambda i, j: (i, j, 0, 0)),
            sspec,
        ],
        out_specs=[row((1, tm, GDN_V_W)), sspec],
        out_shape=[jax.ShapeDtypeStruct((b, l, GDN_V_W), F32),
                   jax.ShapeDtypeStruct((b, GDN_HEADS, GDN_DK, GDN_DV), F32)],
        scratch_shapes=[pltpu.VMEM((GDN_HEADS, GDN_DK, GDN_DV), F32)],
        compiler_params=_params("arbitrary", "arbitrary"),
        name="gdn_chunks",
    )(q, k, v, gbc, gbr, s0)


def _gdn_out_kernel(o_ref, z_ref, x_ref, mod_ref, gn_ref, wout_ref, nf_ref, wr_ref, br_ref,
                    x1_ref, hn2_ref, route_ref, cnt_ref):
    _init_counts(cnt_ref)
    o = o_ref[0]
    z = z_ref[0]
    gn = gn_ref[...]
    parts = []
    for h in range(GDN_HEADS):
        s = slice(h * GDN_DV, (h + 1) * GDN_DV)
        parts.append((_rms(o[:, s]) * gn) * _silu(z[:, s]))
    on = jnp.concatenate(parts, axis=1).astype(BF16)
    y = _dot(on, wout_ref[...])
    x1, hn2, route = _post_mixer(x_ref[0], y, mod_ref[0], nf_ref[...], wr_ref[...], br_ref[...], cnt_ref)
    x1_ref[0] = x1
    _store_token_major(hn2_ref, (0,), hn2)
    route_ref[0] = route


def _gdn_out(o, z, x, mod, gn, wout, nf, wr, br, tm):
    b, l, d = x.shape
    row = lambda shape: pl.BlockSpec(shape, lambda i, j: (i, j, 0))
    return pl.pallas_call(
        _gdn_out_kernel,
        grid=(b, l // tm),
        in_specs=[
            row((1, tm, GDN_V_W)), row((1, tm, GDN_V_W)), row((1, tm, d)),
            pl.BlockSpec((1, 6, d), lambda i, j: (i, 0, 0)),
            _const_spec((1, GDN_DV)), _const_spec(wout.shape), _const_spec((1, d)),
            _const_spec(wr.shape), _const_spec((1, LANES)),
        ],
        out_specs=[row((1, tm, d)), _tok_spec(tm), row((1, tm, LANES)), _const_spec((1, LANES))],
        out_shape=[jax.ShapeDtypeStruct((b, l, d), F32), jax.ShapeDtypeStruct((b, l, d // LANES, LANES), F32),
                   jax.ShapeDtypeStruct((b, l, LANES), F32), jax.ShapeDtypeStruct((1, LANES), F32)],
        compiler_params=_params("arbitrary", "arbitrary"),
        name="gdn_out",
    )(o, z, x, mod, gn, wout, nf, wr, br)


def _gelu_tanh(x):
    return 0.5 * x * (1.0 + jnp.tanh(math.sqrt(2.0 / math.pi) * (x + 0.044715 * (x * x * x))))


def _lru_kernel(dest_ref, xa_ref, ys_hbm, rprev_ref, mprev_ref,
                mod_ref, ng_ref, win_ref, cw_ref, cb_ref, wax_ref, ba_ref, bx_ref, lam_ref, wout_ref,
                hist0_ref, h0_ref, nf_ref, wr_ref, br_ref,
                x1_ref, hn2_ref, route_ref, hist_ref, hlast_ref, cnt_ref, xp_ref, h_ref, ybuf_ref, sem_ref, *, tm):
    l = pl.program_id(1)
    _init_counts(cnt_ref)
    gather = _PairGather(dest_ref, ys_hbm, ybuf_ref, sem_ref, tm, n_pieces=8)

    @pl.when(l == 0)
    def _():
        xp_ref[0:8, :] = jnp.zeros((8, LRU_WIDTH), F32)
        xp_ref[5:8, :] = hist0_ref[0]
        h_ref[...] = h0_ref[0]

    ya, yb = gather.start()
    x = _moe_residual(xa_ref[0], ya, yb, rprev_ref[0], mprev_ref[0])
    gather.issue_piece()
    mod = mod_ref[0]
    hn = (_rms(x) * ng_ref[...]) * (1.0 + mod[1:2]) + mod[0:1]
    gather.issue_piece()
    proj = _dot(hn.astype(BF16), win_ref[...])
    gather.issue_piece()
    gate_br = _gelu_tanh(proj[:, :LRU_WIDTH])
    xb = proj[:, LRU_WIDTH:]
    xp_ref[8:8 + tm, :] = xb
    cw = cw_ref[...]
    xc = cw[3:4] * xb
    for j in range(CONV_W - 1):
        xc = xc + cw[j:j + 1] * xp_ref[5 + j:5 + j + tm, :]
    xc = xc + cb_ref[...]
    new_hist = xp_ref[tm + 5:tm + 8, :]
    xp_ref[5:8, :] = new_hist
    hist_ref[0] = new_hist
    gather.issue_piece()
    xcb = xc.astype(BF16)
    ra, ia = [], []
    for h in range(LRU_BLOCKS):
        s = slice(h * LRU_BLOCK_W, (h + 1) * LRU_BLOCK_W)
        r2 = _dot(xcb[:, s], wax_ref[h])
        ra.append(r2[:, :LRU_BLOCK_W])
        ia.append(r2[:, LRU_BLOCK_W:])
    r = _sigmoid(jnp.concatenate(ra, axis=1) + ba_ref[...])
    i = _sigmoid(jnp.concatenate(ia, axis=1) + bx_ref[...])
    gather.issue_piece()
    log_a = (-LRU_C * r) * _softplus(-lam_ref[...])
    a = jnp.exp(log_a)
    mult = jnp.sqrt(-_expm1(2.0 * log_a))
    b = mult * (i * xc)
    sub = lax.broadcasted_iota(jnp.int32, (tm, LRU_WIDTH), 0) % SUBLANES
    sft = 1
    while sft < SUBLANES:
        keep = sub >= sft
        a_prev = jnp.where(keep, pltpu.roll(a, sft, 0), 1.0)
        b_prev = jnp.where(keep, pltpu.roll(b, sft, 0), 0.0)
        b = a * b_prev + b
        a = a * a_prev
        sft *= 2
    gather.issue_piece()
    h = h_ref[...]
    groups = []
    for g in range(tm // SUBLANES):
        rows = slice(g * SUBLANES, (g + 1) * SUBLANES)
        hg = b[rows] + a[rows] * h
        groups.append(hg)
        h = hg[SUBLANES - 1:SUBLANES, :]
    hs = jnp.concatenate(groups, axis=0)
    h_last = h
    h_ref[...] = h_last
    hlast_ref[0] = h_last
    gather.issue_piece()
    y = _dot((hs * gate_br).astype(BF16), wout_ref[...])
    gather.issue_piece()
    x1, hn2, route = _post_mixer(x, y, mod, nf_ref[...], wr_ref[...], br_ref[...], cnt_ref)
    x1_ref[0] = x1
    _store_token_major(hn2_ref, (0,), hn2)
    route_ref[0] = route
    gather.finish()


def _lru_layer(dest, xa, ys, rprev, mprev, mod, ng, win, cw, cb, wax, ba, bx, lam, wout, hist0, h0, nf, wr, br, tm):
    b, l, d = xa.shape
    row = lambda shape: pl.BlockSpec(shape, lambda i, j, *_: (i, j, 0))
    per_b = lambda shape: pl.BlockSpec(shape, lambda i, j, *_: (i, 0, 0))
    vec = _const_spec((1, d))
    grid_spec = pltpu.PrefetchScalarGridSpec(
        num_scalar_prefetch=1,
        grid=(b, l // tm),
        in_specs=[
            row((1, tm, d)), pl.BlockSpec(memory_space=pl.ANY), row((1, tm, LANES)), per_b((1, 6, d)),
            per_b((1, 6, d)), vec, _const_spec(win.shape), _const_spec(cw.shape), vec,
            _const_spec(wax.shape), vec, vec, vec, _const_spec(wout.shape),
            per_b((1, CONV_W - 1, LRU_WIDTH)), per_b((1, 1, LRU_WIDTH)), vec,
            _const_spec(wr.shape), _const_spec((1, LANES)),
        ],
        out_specs=[row((1, tm, d)), _tok_spec(tm), row((1, tm, LANES)),
                   per_b((1, CONV_W - 1, LRU_WIDTH)), per_b((1, 1, LRU_WIDTH)), _const_spec((1, LANES))],
        scratch_shapes=[pltpu.VMEM((tm + 8, LRU_WIDTH), F32), pltpu.VMEM((1, LRU_WIDTH), F32),
                        pltpu.VMEM((2, 2 * tm, d // LANES, LANES), F32), pltpu.SemaphoreType.DMA((2,))],
    )
    return pl.pallas_call(
        functools.partial(_lru_kernel, tm=tm),
        grid_spec=grid_spec,
        out_shape=[jax.ShapeDtypeStruct((b, l, d), F32), jax.ShapeDtypeStruct((b, l, d // LANES, LANES), F32),
                   jax.ShapeDtypeStruct((b, l, LANES), F32),
                   jax.ShapeDtypeStruct((b, CONV_W - 1, LRU_WIDTH), F32),
                   jax.ShapeDtypeStruct((b, 1, LRU_WIDTH), F32), jax.ShapeDtypeStruct((1, LANES), F32)],
        compiler_params=_params("arbitrary", "arbitrary"),
        name="lru_layer",
    )(dest, xa, ys, rprev, mprev, mod, ng, win, cw, cb, wax, ba, bx, lam, wout, hist0, h0, nf, wr, br)


def _moe_kernel(be_ref, st_ref, x_hbm, w1_ref, w3_ref, w2_ref, y_ref, w1b_ref, w3b_ref, w2b_ref, xbuf_ref, sem_ref):
    i = pl.program_id(0)
    last = pl.num_programs(0) - 1
    slot = i % 2

    def row_copy(blk, r, buf):
        tok = st_ref[blk * MOE_BLOCK + r]
        return pltpu.make_async_copy(x_hbm.at[tok], xbuf_ref.at[buf, r], sem_ref.at[buf])

    def wait_block(buf):
        pltpu.make_async_copy(x_hbm.at[pl.ds(0, MOE_BLOCK)], xbuf_ref.at[buf], sem_ref.at[buf]).wait()

    @pl.when(i == 0)
    def _():
        for r in range(MOE_BLOCK):
            row_copy(0, r, 0).start(priority=r % N_DMA_QUEUES)

    prev = be_ref[jnp.maximum(i - 1, 0)]

    @pl.when((i == 0) | (be_ref[i] != prev))
    def _():
        w1b_ref[...] = w1_ref[0, 0].astype(BF16)
        w3b_ref[...] = w3_ref[0, 0].astype(BF16)
        w2b_ref[...] = w2_ref[0, 0].astype(BF16)

    nxt = jnp.minimum(i + 1, last)

    def body(buf):
        wait_block(buf)
        n_piece = 6
        per = MOE_BLOCK // n_piece + 1
        issued = [0]

        def issue_some():
            hi = min(issued[0] + per, MOE_BLOCK)
            for r in range(issued[0], hi):
                row_copy(nxt, r, 1 - buf).start(priority=r % N_DMA_QUEUES)
            issued[0] = hi

        xb = _load_token_major(xbuf_ref, (buf,)).astype(BF16)
        half = D_FF_EXPERT // 2
        hid = []
        for c in range(2):
            cols = slice(c * half, (c + 1) * half)
            issue_some()
            h1 = _dot(xb, w1b_ref[:, cols])
            issue_some()
            h3 = _dot(xb, w3b_ref[:, cols])
            hid.append((_silu(h1) * h3).astype(BF16))
        issue_some()
        y = _dot(hid[0], w2b_ref[0:half, :])
        issue_some()
        _store_token_major(y_ref, (), y + _dot(hid[1], w2b_ref[half:, :]))
        assert issued[0] == MOE_BLOCK

        @pl.when(i == last)
        def _():
            wait_block(1 - buf)

    for buf in range(2):
        pl.when(slot == buf)(functools.partial(body, buf))


def _moe_blocks(blk_expert, slot_tok, x, w1, w3, w2, layer):
    d = D_MODEL
    p = slot_tok.shape[0]
    nb = p // MOE_BLOCK
    grid_spec = pltpu.PrefetchScalarGridSpec(
        num_scalar_prefetch=2,
        grid=(nb,),
        in_specs=[
            pl.BlockSpec(memory_space=pl.ANY),
            pl.BlockSpec((1, 1, d, D_FF_EXPERT), lambda i, be, st: (layer, be[i], 0, 0)),
            pl.BlockSpec((1, 1, d, D_FF_EXPERT), lambda i, be, st: (layer, be[i], 0, 0)),
            pl.BlockSpec((1, 1, D_FF_EXPERT, d), lambda i, be, st: (layer, be[i], 0, 0)),
        ],
        out_specs=pl.BlockSpec((MOE_BLOCK, d // LANES, LANES), lambda i, be, st: (i, 0, 0)),
        scratch_shapes=[pltpu.VMEM((d, D_FF_EXPERT), BF16), pltpu.VMEM((d, D_FF_EXPERT), BF16),
                        pltpu.VMEM((D_FF_EXPERT, d), BF16), pltpu.VMEM((2, MOE_BLOCK, d // LANES, LANES), F32),
                        pltpu.SemaphoreType.DMA((2,))],
    )
    return pl.pallas_call(
        _moe_kernel,
        grid_spec=grid_spec,
        out_shape=jax.ShapeDtypeStruct((p, d // LANES, LANES), F32),
        compiler_params=_params("arbitrary"),
        name="moe_blocks",
    )(blk_expert, slot_tok, x, w1, w3, w2)


def _moe_residual(x1, ya, yb, route, mod):
    return x1 + mod[5:6] * (ya * route[:, 2:3] + yb * route[:, 3:4])


def _final_kernel(dest_ref, x1_ref, ys_hbm, route_ref, mod_ref, no_ref, o_ref, ybuf_ref, sem_ref, *, tm):
    gather = _PairGather(dest_ref, ys_hbm, ybuf_ref, sem_ref, tm, n_pieces=1)
    ya, yb = gather.start()
    gather.issue_piece()
    x2 = _moe_residual(x1_ref[0], ya, yb, route_ref[0], mod_ref[0])
    o_ref[0] = _rms(x2) * no_ref[...]
    gather.finish()


def _final(dest, x1, ys, route, mod, norm_out, tm):
    b, l, d = x1.shape
    row = lambda shape: pl.BlockSpec(shape, lambda i, j, *_: (i, j, 0))
    grid_spec = pltpu.PrefetchScalarGridSpec(
        num_scalar_prefetch=1,
        grid=(b, l // tm),
        in_specs=[row((1, tm, d)), pl.BlockSpec(memory_space=pl.ANY), row((1, tm, LANES)),
                  pl.BlockSpec((1, 6, d), lambda i, j, *_: (i, 0, 0)), _const_spec((1, d))],
        out_specs=row((1, tm, d)),
        scratch_shapes=[pltpu.VMEM((2, 2 * tm, d // LANES, LANES), F32), pltpu.SemaphoreType.DMA((2,))],
    )
    return pl.pallas_call(
        functools.partial(_final_kernel, tm=tm),
        grid_spec=grid_spec,
        out_shape=jax.ShapeDtypeStruct((b, l, d), F32),
        compiler_params=_params("arbitrary", "arbitrary"),
        name="final",
    )(dest, x1, ys, route, mod, norm_out)


def _moe_experts(hn2, route, counts, w1, w3, w2, layer):
    b, l = hn2.shape[:2]
    d = D_MODEL
    n = b * l
    a = n * EXPERT_TOPK
    rt = route.reshape(n, LANES)
    e_ab = rt[:, 0:EXPERT_TOPK].astype(jnp.int32)
    r_ab = rt[:, 4:4 + EXPERT_TOPK].astype(jnp.int32)
    counts = counts[0, :N_EXPERTS].astype(jnp.int32)
    padded = (counts + MOE_BLOCK - 1) // MOE_BLOCK * MOE_BLOCK
    pad_end = jnp.cumsum(padded)
    pad_start = pad_end - padded
    dest = pad_start[e_ab] + r_ab
    n_blocks = -(-a // MOE_BLOCK) + N_EXPERTS
    p = n_blocks * MOE_BLOCK
    tok = jnp.broadcast_to(jnp.arange(n, dtype=jnp.int32)[:, None], (n, EXPERT_TOPK))
    slot_tok = (jnp.arange(p, dtype=jnp.int32) % n).at[dest.reshape(-1)].set(tok.reshape(-1))
    blk_first = jnp.arange(n_blocks, dtype=jnp.int32) * MOE_BLOCK
    blk_expert = jnp.minimum(jnp.sum((pad_end[None, :] <= blk_first[:, None]).astype(jnp.int32), axis=1),
                             N_EXPERTS - 1)
    ys = _moe_blocks(blk_expert, slot_tok, hn2.reshape(n, d // LANES, LANES), w1, w3, w2, layer)
    return ys, jnp.concatenate([dest[:, 0], dest[:, 1]])


def _trunk(x, mods, gdn_s, gdn_conv, lru_h, lru_conv, wp):
    b, l, d = x.shape
    tm = min(ROW_TILE, l)
    chunk = min(GDN_CHUNK, l)
    assert l % tm == 0 and tm % chunk == 0 and chunk & (chunk - 1) == 0
    mod = mods[0]
    q, k, v, z, gbc, gbr, gconv_new = _gdn_in(x, mod, wp['norm_mix0'], wp['gdn_wqkv'], wp['gdn_wz'], wp['gdn_wab'],
                                              wp['gdn_conv_w'], gdn_conv, wp['gdn_alog'], wp['gdn_dtb'], tm, chunk)
    o, s_new = _gdn_chunks(q, k, v, gbc, gbr, gdn_s, tm, chunk)
    x1, hn2, route, counts = _gdn_out(o, z, x, mod, wp['gdn_norm'], wp['gdn_wout'], wp['norm_ffn0'],
                                      wp['wr0'], wp['br0'], tm)
    ys, dest = _moe_experts(hn2, route, counts, wp['moe_w1'], wp['moe_w3'], wp['moe_w2'], 0)
    x1, hn2, route, lconv_new, h_new, counts = _lru_layer(
        dest, x1, ys, route, mod, mods[1], wp['norm_mix1'], wp['lru_win'], wp['lru_conv_w'], wp['lru_conv_b'],
        wp['lru_wax'], wp['lru_ba'], wp['lru_bx'], wp['lru_lam'], wp['lru_wout'], lru_conv, lru_h, wp['norm_ffn1'],
        wp['wr1'], wp['br1'], tm)
    ys, dest = _moe_experts(hn2, route, counts, wp['moe_w1'], wp['moe_w3'], wp['moe_w2'], 1)
    y = _final(dest, x1, ys, route, mods[1], wp['norm_out'], tm)
    return y, s_new[None], gconv_new[None], h_new.reshape(1, b, LRU_WIDTH), lconv_new[None]


def _pad_lanes(v, width=LANES):
    v = v.reshape(1, -1)
    return jnp.pad(v, ((0, 0), (0, width - v.shape[1])))


def _router_weights(w_rg, b_rg, w_re, b_re):
    w = jnp.pad(jnp.concatenate([w_rg, w_re], axis=1), ((0, 0), (0, LANES - N_GROUPS - N_EXPERTS)))
    return w.astype(BF16), _pad_lanes(jnp.concatenate([b_rg, b_re]))


def kernel(x_prompt, x_sample, state_gdn_S, state_gdn_conv, state_lru_h, state_lru_conv, c_prompt, c_sample, w_ada, b_ada, norm_mix, norm_ffn, norm_out, gdn_w_in, gdn_conv_w, gdn_a_log, gdn_dt_bias, gdn_norm, gdn_w_out, lru_w_in, lru_conv_w, lru_conv_b, lru_w_a, lru_b_a, lru_w_x, lru_b_x, lru_lambda, lru_w_out, moe_w_rg, moe_b_rg, moe_w_re, moe_b_re, moe_w1, moe_w3, moe_w2):
    d = D_MODEL
    bp = x_prompt.shape[0]
    bs = x_sample.shape[0]
    win = gdn_w_in[0]
    wab = jnp.pad(win[:, GDN_CONV_CH + GDN_V_W:], ((0, 0), (0, LANES - 2 * GDN_HEADS)))
    wp = dict(
        norm_mix0=norm_mix[0].reshape(1, d), norm_mix1=norm_mix[1].reshape(1, d),
        norm_ffn0=norm_ffn[0].reshape(1, d), norm_ffn1=norm_ffn[1].reshape(1, d),
        norm_out=norm_out.reshape(1, d),
        gdn_wqkv=win[:, :GDN_CONV_CH].astype(BF16),
        gdn_wz=win[:, GDN_CONV_CH:GDN_CONV_CH + GDN_V_W].astype(BF16),
        gdn_wab=wab.astype(BF16),
        gdn_conv_w=gdn_conv_w[0],
        gdn_alog=_pad_lanes(gdn_a_log[0]), gdn_dtb=_pad_lanes(gdn_dt_bias[0]),
        gdn_norm=gdn_norm[0].reshape(1, GDN_DV), gdn_wout=gdn_w_out[0].astype(BF16),
        lru_win=lru_w_in[0].astype(BF16), lru_conv_w=lru_conv_w[0], lru_conv_b=lru_conv_b[0].reshape(1, d),
        lru_wax=jnp.concatenate([lru_w_a[0], lru_w_x[0]], axis=-1).astype(BF16),
        lru_ba=lru_b_a[0].reshape(1, d), lru_bx=lru_b_x[0].reshape(1, d), lru_lam=lru_lambda[0].reshape(1, d),
        lru_wout=lru_w_out[0].astype(BF16),
        moe_w1=moe_w1, moe_w3=moe_w3, moe_w2=moe_w2,
    )
    for i in range(DEPTH):
        wp[f'wr{i}'], wp[f'br{i}'] = _router_weights(moe_w_rg[i], moe_b_rg[i], moe_w_re[i], moe_b_re[i])

    mods = _ada_mod(jnp.concatenate([c_prompt, c_sample], axis=0), w_ada, b_ada)
    mods = mods.reshape(DEPTH, bp + bs, 6, d)
    mods_p = [mods[i, :bp] for i in range(DEPTH)]
    mods_s = [mods[i, bp:] for i in range(DEPTH)]

    dt = x_prompt.dtype
    z_s = jnp.zeros((bp, GDN_HEADS, GDN_DK, GDN_DV), dt)
    z_gc = jnp.zeros((bp, CONV_W - 1, GDN_CONV_CH), dt)
    z_h = jnp.zeros((bp, 1, LRU_WIDTH), dt)
    z_lc = jnp.zeros((bp, CONV_W - 1, LRU_WIDTH), dt)
    y_p, gs_p, gc_p, lh_p, lc_p = _trunk(x_prompt, mods_p, z_s, z_gc, z_h, z_lc, wp)
    y_s, gs_s, gc_s, lh_s, lc_s = _trunk(x_sample, mods_s, state_gdn_S[0], state_gdn_conv[0],
                                         state_lru_h[0].reshape(bs, 1, LRU_WIDTH), state_lru_conv[0], wp)
    return (y_p, y_s, gs_p, gc_p, lh_p, lc_p, gs_s, gc_s, lh_s, lc_s)
```

```python
import functools
import math

import jax
import jax.numpy as jnp
from jax import lax
from jax.experimental import pallas as pl
from jax.experimental.pallas import tpu as pltpu

F32 = jnp.float32
BF16 = jnp.bfloat16

D_MODEL = 1024
DEPTH = 2
CONV_W = 4
NORM_EPS = 1e-6
GDN_HEADS = 8
GDN_DK = 128
GDN_DV = 128
GDN_QK_W = GDN_HEADS * GDN_DK
GDN_V_W = GDN_HEADS * GDN_DV
GDN_CONV_CH = 2 * GDN_QK_W + GDN_V_W
GDN_CHUNK = 64
GDN_INV_BLOCK = 16
LRU_WIDTH = D_MODEL
LRU_BLOCKS = 8
LRU_BLOCK_W = LRU_WIDTH // LRU_BLOCKS
LRU_C = 8.0
N_GROUPS = 4
EXPERTS_PER_GROUP = 8
N_EXPERTS = N_GROUPS * EXPERTS_PER_GROUP
EXPERT_TOPK = 2
D_FF_EXPERT = 512
MOE_BLOCK = 256
MOE_BUFFERS = 3
LANES = 128
SUBLANES = 8
ROW_TILE = 256
VMEM_LIMIT = 56 * 1024 * 1024


def _dot(a, b):
    return jnp.dot(a, b, preferred_element_type=F32)


def _dot_nt(a, b):
    return lax.dot_general(a, b, (((1,), (1,)), ((), ())), preferred_element_type=F32)


def _dot_tn(a, b):
    return lax.dot_general(a, b, (((0,), (0,)), ((), ())), preferred_element_type=F32)


def _split3(x):
    a = x.astype(BF16)
    r = x - a.astype(F32)
    b = r.astype(BF16)
    c = (r - b.astype(F32)).astype(BF16)
    return a, b, c


def _rms(x):
    return x * lax.rsqrt(jnp.mean(x * x, axis=-1, keepdims=True) + NORM_EPS)


def _sigmoid(x):
    return 1.0 / (1.0 + jnp.exp(-x))


def _silu(x):
    return x * _sigmoid(x)


def _softplus(x):
    return jnp.maximum(x, 0.0) + jnp.log1p(jnp.exp(-jnp.abs(x)))


def _expm1(x):
    u = jnp.exp(x)
    near = (u > 0.5) & (u < 2.0) & (u != 1.0)
    corrected = (u - 1.0) * x / jnp.where(near, jnp.log(u), 1.0)
    return jnp.where(u == 1.0, x, jnp.where(near, corrected, u - 1.0))


def _params(*sem):
    return pltpu.CompilerParams(dimension_semantics=sem, vmem_limit_bytes=VMEM_LIMIT)


def _const_spec(shape):
    nd = len(shape)
    return pl.BlockSpec(shape, lambda *_: (0,) * nd)


def _ada_kernel(c_ref, w_ref, b_ref, o_ref):
    s = _silu(c_ref[...]).astype(BF16)
    o_ref[0] = _dot(s, w_ref[0].astype(BF16)) + b_ref[0]


def _ada_mod(c_all, w_ada, b_ada):
    bt = c_all.shape[0]
    tn = 1536
    n6 = 6 * D_MODEL
    return pl.pallas_call(
        _ada_kernel,
        grid=(DEPTH, n6 // tn),
        in_specs=[
            pl.BlockSpec((bt, D_MODEL), lambda i, j: (0, 0)),
            pl.BlockSpec((1, D_MODEL, tn), lambda i, j: (i, 0, j)),
            pl.BlockSpec((1, 1, tn), lambda i, j: (i, 0, j)),
        ],
        out_specs=pl.BlockSpec((1, bt, tn), lambda i, j: (i, 0, j)),
        out_shape=jax.ShapeDtypeStruct((DEPTH, bt, n6), F32),
        compiler_params=_params("arbitrary", "arbitrary"),
        name="ada_mod",
    )(c_all, w_ada, b_ada.reshape(DEPTH, 1, n6))


def _route(hn2, wr, br, cnt_ref):
    logits = _dot(hn2.astype(BF16), wr) + br
    lane = lax.broadcasted_iota(jnp.int32, logits.shape, 1)
    neg = jnp.float32(-jnp.inf)
    big = jnp.int32(1 << 20)
    is_g = lane < N_GROUPS
    lg = jnp.where(is_g, logits, neg)
    eg = jnp.exp(lg - jnp.max(lg, axis=-1, keepdims=True))
    pg = eg / jnp.sum(eg, axis=-1, keepdims=True)
    pg = jnp.where(is_g, pg, -1.0)
    gate_g = jnp.max(pg, axis=-1, keepdims=True)
    grp = jnp.min(jnp.where(pg == gate_g, lane, big), axis=-1, keepdims=True)
    lo = N_GROUPS + grp * EXPERTS_PER_GROUP
    is_e = (lane >= lo) & (lane < lo + EXPERTS_PER_GROUP)
    le = jnp.where(is_e, logits, neg)
    ee = jnp.exp(le - jnp.max(le, axis=-1, keepdims=True))
    pe = ee / jnp.sum(ee, axis=-1, keepdims=True)
    pe = jnp.where(is_e, pe, -1.0)
    p1 = jnp.max(pe, axis=-1, keepdims=True)
    i1 = jnp.min(jnp.where(pe == p1, lane, big), axis=-1, keepdims=True)
    pe2 = jnp.where(lane == i1, -1.0, pe)
    p2 = jnp.max(pe2, axis=-1, keepdims=True)
    i2 = jnp.min(jnp.where(pe2 == p2, lane, big), axis=-1, keepdims=True)
    tot = p1 + p2
    w1 = gate_g * (p1 / tot)
    w2 = gate_g * (p2 / tot)
    e1 = i1 - N_GROUPS
    e2 = i2 - N_GROUPS
    tm = logits.shape[0]
    oh1 = lane == e1
    oh2 = lane == e2
    cnt = jnp.where(oh1 | oh2, 1.0, 0.0)
    ri = lax.broadcasted_iota(jnp.int32, (tm, tm), 0)
    ci = lax.broadcasted_iota(jnp.int32, (tm, tm), 1)
    before = jnp.where(ci < ri, 1.0, 0.0).astype(BF16)
    pos = _dot(before, cnt.astype(BF16)) + cnt_ref[...]
    r1 = jnp.sum(jnp.where(oh1, pos, 0.0), axis=-1, keepdims=True)
    r2 = jnp.sum(jnp.where(oh2, pos, 0.0), axis=-1, keepdims=True)
    cnt_ref[...] = cnt_ref[...] + jnp.sum(cnt, axis=0, keepdims=True)
    vals = (e1.astype(F32), e2.astype(F32), w1, w2, r1, r2)
    out = jnp.zeros(logits.shape, F32)
    for j, val in enumerate(vals):
        out = jnp.where(lane == j, val, out)
    return out


def _tok_spec(tm):
    return pl.BlockSpec((1, tm, D_MODEL // LANES, LANES), lambda i, j, *_: (i, j, 0, 0))


def _store_token_major(ref, lead, x):
    for c in range(D_MODEL // LANES):
        ref[lead + (slice(None), c, slice(None))] = x[:, c * LANES:(c + 1) * LANES]


def _load_token_major(ref, lead, rows=slice(None)):
    return jnp.concatenate([ref[lead + (rows, c, slice(None))] for c in range(D_MODEL // LANES)], axis=1)


N_DMA_QUEUES = 2


class _PairGather:
    def __init__(self, dest_ref, ys_hbm, buf_ref, sem_ref, tm, n_pieces):
        self.dest_ref, self.ys_hbm, self.buf_ref, self.sem_ref, self.tm = dest_ref, ys_hbm, buf_ref, sem_ref, tm
        n_steps = pl.num_programs(0) * pl.num_programs(1)
        self.n = n_steps * tm
        self.step = pl.program_id(0) * pl.num_programs(1) + pl.program_id(1)
        self.last = n_steps - 1
        self.slot = self.step % 2
        self.nxt = jnp.minimum(self.step + 1, self.last)
        self.per = -(-2 * tm // n_pieces)
        self.issued = 0

    def _copy(self, tile, j, buf):
        k, r = divmod(j, self.tm)
        row = self.dest_ref[k * self.n + tile * self.tm + r]
        return pltpu.make_async_copy(self.ys_hbm.at[row], self.buf_ref.at[buf, j], self.sem_ref.at[buf])

    def _wait(self, buf):
        pltpu.make_async_copy(self.ys_hbm.at[pl.ds(0, 2 * self.tm)], self.buf_ref.at[buf], self.sem_ref.at[buf]).wait()

    def start(self):
        @pl.when(self.step == 0)
        def _():
            for j in range(2 * self.tm):
                self._copy(0, j, 0).start(priority=j % N_DMA_QUEUES)

        self._wait(self.slot)
        ya = _load_token_major(self.buf_ref, (self.slot,), slice(0, self.tm))
        yb = _load_token_major(self.buf_ref, (self.slot,), slice(self.tm, 2 * self.tm))
        return ya, yb

    def issue_piece(self):
        hi = min(self.issued + self.per, 2 * self.tm)
        for j in range(self.issued, hi):
            self._copy(self.nxt, j, 1 - self.slot).start(priority=j % N_DMA_QUEUES)
        self.issued = hi

    def finish(self):
        assert self.issued == 2 * self.tm

        @pl.when(self.step == self.last)
        def _():
            self._wait(1 - self.slot)


def _init_counts(cnt_ref):
    @pl.when((pl.program_id(0) == 0) & (pl.program_id(1) == 0))
    def _():
        cnt_ref[...] = jnp.zeros(cnt_ref.shape, F32)


def _post_mixer(x, y, mod, nf, wr, br, cnt_ref):
    g1 = mod[2:3]
    sh2 = mod[3:4]
    sc2 = mod[4:5]
    x1 = x + g1 * y
    hn2 = (_rms(x1) * nf) * (1.0 + sc2) + sh2
    return x1, hn2, _route(hn2, wr, br, cnt_ref)


def _gdn_in_kernel(x_ref, mod_ref, ng_ref, wqkv_ref, wz_ref, wab_ref, cw_ref, hist0_ref, alog_ref, dtb_ref,
                   q_ref, k_ref, v_ref, z_ref, gbc_ref, gbr_ref, hist_ref, xp_ref, *, tm, chunk):
    l = pl.program_id(1)

    @pl.when(l == 0)
    def _():
        xp_ref[0:8, :] = jnp.zeros((8, GDN_CONV_CH), F32)
        xp_ref[5:8, :] = hist0_ref[0]

    x = x_ref[0]
    mod = mod_ref[0]
    hn = (_rms(x) * ng_ref[...]) * (1.0 + mod[1:2]) + mod[0:1]
    hb = hn.astype(BF16)
    qkv = _dot(hb, wqkv_ref[...])
    xp_ref[8:8 + tm, :] = qkv
    cw = cw_ref[...]
    y = cw[3:4] * qkv
    for j in range(CONV_W - 1):
        y = y + cw[j:j + 1] * xp_ref[5 + j:5 + j + tm, :]
    new_hist = xp_ref[tm + 5:tm + 8, :]
    xp_ref[5:8, :] = new_hist
    hist_ref[0] = new_hist
    y = _silu(y)
    for h in range(GDN_HEADS):
        s = slice(h * GDN_DK, (h + 1) * GDN_DK)
        qh = y[:, s]
        q_ref[0, :, s] = qh * lax.rsqrt(jnp.sum(qh * qh, axis=-1, keepdims=True) + 1e-6) * (GDN_DK ** -0.5)
        kh = y[:, GDN_QK_W + h * GDN_DK:GDN_QK_W + (h + 1) * GDN_DK]
        k_ref[0, :, s] = kh * lax.rsqrt(jnp.sum(kh * kh, axis=-1, keepdims=True) + 1e-6)
    v_ref[0] = y[:, 2 * GDN_QK_W:]
    z_ref[0] = _dot(hb, wz_ref[...])
    ab = _dot(hb, wab_ref[...])
    g = -jnp.exp(alog_ref[...]) * _softplus(ab + dtb_ref[...])
    beta = _sigmoid(ab)
    ri = lax.broadcasted_iota(jnp.int32, (tm, tm), 0)
    ci = lax.broadcasted_iota(jnp.int32, (tm, tm), 1)
    tri = jnp.where((ri // chunk == ci // chunk) & (ci <= ri), 1.0, 0.0).astype(BF16)
    g1, g2, g3 = _split3(g)
    gcum = (_dot(tri, g1) + _dot(tri, g2)) + _dot(tri, g3)
    lane = lax.broadcasted_iota(jnp.int32, (tm, LANES), 1)
    gb = jnp.where(lane < GDN_HEADS, gcum, beta)
    gbc_ref[0] = gb
    er = lax.broadcasted_iota(jnp.int32, (2 * GDN_HEADS, LANES), 0)
    ec = lax.broadcasted_iota(jnp.int32, (2 * GDN_HEADS, LANES), 1)
    sel = jnp.where(er == ec, 1.0, 0.0).astype(BF16)
    b1, b2, b3 = _split3(gb)
    for n in range(tm // chunk):
        r = slice(n * chunk, (n + 1) * chunk)
        gbr_ref[0, n] = (_dot_nt(sel, b1[r]) + _dot_nt(sel, b2[r])) + _dot_nt(sel, b3[r])


def _gdn_in(x, mod, ng, wqkv, wz, wab, cw, hist0, alog, dtb, tm, chunk):
    b, l, d = x.shape
    grid = (b, l // tm)
    row = lambda shape: pl.BlockSpec(shape, lambda i, j: (i, j, 0))
    outs = pl.pallas_call(
        functools.partial(_gdn_in_kernel, tm=tm, chunk=chunk),
        grid=grid,
        in_specs=[
            row((1, tm, d)),
            pl.BlockSpec((1, 6, d), lambda i, j: (i, 0, 0)),
            _const_spec((1, d)),
            _const_spec(wqkv.shape),
            _const_spec(wz.shape),
            _const_spec(wab.shape),
            _const_spec(cw.shape),
            pl.BlockSpec((1, CONV_W - 1, GDN_CONV_CH), lambda i, j: (i, 0, 0)),
            _const_spec((1, LANES)),
            _const_spec((1, LANES)),
        ],
        out_specs=[
            row((1, tm, GDN_QK_W)), row((1, tm, GDN_QK_W)), row((1, tm, GDN_V_W)), row((1, tm, GDN_V_W)),
            row((1, tm, LANES)),
            pl.BlockSpec((1, tm // chunk, 2 * GDN_HEADS, chunk), lambda i, j: (i, j, 0, 0)),
            pl.BlockSpec((1, CONV_W - 1, GDN_CONV_CH), lambda i, j: (i, 0, 0)),
        ],
        out_shape=[
            jax.ShapeDtypeStruct((b, l, GDN_QK_W), F32), jax.ShapeDtypeStruct((b, l, GDN_QK_W), F32),
            jax.ShapeDtypeStruct((b, l, GDN_V_W), F32), jax.ShapeDtypeStruct((b, l, GDN_V_W), F32),
            jax.ShapeDtypeStruct((b, l, LANES), F32),
            jax.ShapeDtypeStruct((b, l // chunk, 2 * GDN_HEADS, chunk), F32),
            jax.ShapeDtypeStruct((b, CONV_W - 1, GDN_CONV_CH), F32),
        ],
        scratch_shapes=[pltpu.VMEM((tm + 8, GDN_CONV_CH), F32)],
        compiler_params=_params("arbitrary", "arbitrary"),
        name="gdn_in",
    )(x, mod, ng, wqkv, wz, wab, cw, hist0, alog, dtb)
    return outs


def _gdn_chunk_kernel(q_ref, k_ref, v_ref, gbc_ref, gbr_ref, s0_ref, o_ref, sout_ref, s_ref, *, tm, chunk):
    l = pl.program_id(1)

    @pl.when(l == 0)
    def _():
        s_ref[...] = s0_ref[0]

    c2 = 2 * chunk
    n_pairs = GDN_HEADS // 2
    ri = lax.broadcasted_iota(jnp.int32, (chunk, c2), 0)
    cl = lax.broadcasted_iota(jnp.int32, (chunk, c2), 1)
    ci = cl % chunk
    left = cl < chunk
    incl = ri >= ci
    strict = ri > ci
    eye = jnp.where(ri == ci, 1.0, 0.0).astype(F32)
    base = min(GDN_INV_BLOCK, chunk)
    levels = int(math.log2(base))
    diag_blk = (ri // base) == (ci // base)
    merge_masks = []
    blk = base
    while blk < chunk:
        merge_masks.append(((ri // (2 * blk)) == (ci // (2 * blk))) & ((ri // blk) % 2 == 1) & ((ci // blk) % 2 == 0))
        blk *= 2
    heads = range(GDN_HEADS)
    pairs = range(n_pairs)

    def bdiag(x):
        return jnp.concatenate([jnp.where(left, x, 0.0), jnp.where(left, 0.0, x)], axis=0).astype(BF16)

    def bdiag2(x0, x1):
        z = jnp.zeros_like(x0)
        return jnp.concatenate([jnp.concatenate([x0, z], axis=1), jnp.concatenate([z, x1], axis=1)], axis=0).astype(BF16)

    def one_chunk(n, carry):
        r0 = pl.multiple_of(n * chunk, chunk)
        rows = pl.ds(r0, chunk)
        gbc = gbc_ref[0, rows, :]
        gbr = gbr_ref[0, n]
        hs = [slice(h * GDN_DK, (h + 1) * GDN_DK) for h in heads]
        q = [q_ref[0, rows, hs[h]] for h in heads]
        k = [k_ref[0, rows, hs[h]] for h in heads]
        v = [v_ref[0, rows, hs[h]] for h in heads]
        gc = [gbc[:, h:h + 1] for h in heads]
        bc = [gbc[:, GDN_HEADS + h:GDN_HEADS + h + 1] for h in heads]
        kb = [k[h] * bc[h] for h in heads]
        a2 = [_dot_nt(jnp.concatenate([q[h], kb[h]], axis=0).astype(BF16), k[h].astype(BF16)) for h in heads]
        attn, m = [], []
        for hp in pairs:
            h0, h1 = 2 * hp, 2 * hp + 1
            gcp = jnp.where(left, gc[h0], gc[h1])
            grp = jnp.concatenate([gbr[h0:h0 + 1, :], gbr[h1:h1 + 1, :]], axis=1)
            decay = jnp.where(incl, jnp.exp(jnp.where(incl, gcp - grp, 0.0)), 0.0)
            ap = jnp.concatenate([a2[h0], a2[h1]], axis=1)
            attn.append(ap[:chunk] * decay)
            m.append(jnp.where(strict, ap[chunk:] * decay, 0.0))
        nq = [jnp.where(diag_blk, -m[hp], 0.0) for hp in pairs]
        p = [eye + nq[hp] for hp in pairs]
        nq = [_dot(nq[hp].astype(BF16), bdiag(nq[hp])) for hp in pairs]
        for j in range(1, levels):
            if j < levels - 1:
                r = [_dot(nq[hp].astype(BF16), jnp.concatenate([bdiag(nq[hp]), bdiag(p[hp])], axis=1)) for hp in pairs]
                nq = [r[hp][:, :c2] for hp in pairs]
                p = [p[hp] + r[hp][:, c2:] for hp in pairs]
            else:
                p = [p[hp] + _dot(nq[hp].astype(BF16), bdiag(p[hp])) for hp in pairs]
        for low in merge_masks:
            t = [_dot(jnp.where(low, m[hp], 0.0).astype(BF16), bdiag(p[hp])) for hp in pairs]
            p = [p[hp] - _dot(p[hp].astype(BF16), bdiag(t[hp])) for hp in pairs]
        eg = [jnp.exp(gc[h]) for h in heads]
        rhs = [jnp.concatenate([kb[h] * eg[h], v[h] * bc[h]], axis=1) for h in heads]
        wu = [_dot(p[hp].astype(BF16), bdiag2(rhs[2 * hp], rhs[2 * hp + 1])) for hp in pairs]
        wd = GDN_DK + GDN_DV
        w = [wu[h // 2][:, (h % 2) * wd:(h % 2) * wd + GDN_DK] for h in heads]
        u = [wu[h // 2][:, (h % 2) * wd + GDN_DK:(h % 2 + 1) * wd] for h in heads]
        st = [s_ref[h] for h in heads]
        ws = [_dot(jnp.concatenate([w[h], q[h] * eg[h]], axis=0).astype(BF16), st[h].astype(BF16)) for h in heads]
        v_new = [u[h] - ws[h][:chunk] for h in heads]
        av = [_dot(attn[hp].astype(BF16), bdiag2(v_new[2 * hp], v_new[2 * hp + 1])) for hp in pairs]
        for h in heads:
            o_ref[0, rows, hs[h]] = ws[h][chunk:] + av[h // 2][:, (h % 2) * GDN_DV:(h % 2 + 1) * GDN_DV]
        for h in heads:
            g_last = gc[h][chunk - 1:chunk, :]
            kd = k[h] * jnp.exp(g_last - gc[h])
            s_ref[h] = st[h] * jnp.exp(g_last) + _dot_tn(kd.astype(BF16), v_new[h].astype(BF16))
        return carry

    lax.fori_loop(0, tm // chunk, one_chunk, 0)
    sout_ref[0] = s_ref[...]


def _gdn_chunks(q, k, v, gbc, gbr, s0, tm, chunk):
    b, l, _ = q.shape
    row = lambda shape: pl.BlockSpec(shape, lambda i, j: (i, j, 0))
    sspec = pl.BlockSpec((1, GDN_HEADS, GDN_DK, GDN_DV), lambda i, j: (i, 0, 0, 0))
    return pl.pallas_call(
        functools.partial(_gdn_chunk_kernel, tm=tm, chunk=chunk),
        grid=(b, l // tm),
        in_specs=[
            row((1, tm, GDN_QK_W)), row((1, tm, GDN_QK_W)), row((1, tm, GDN_V_W)), row((1, tm, LANES)),
            pl.BlockSpec((1, tm // chunk, 2 * GDN_HEADS, chunk), lambda i, j: (i, j, 0, 0)),
            sspec,
        ],
        out_specs=[row((1, tm, GDN_V_W)), sspec],
        out_shape=[jax.ShapeDtypeStruct((b, l, GDN_V_W), F32),
                   jax.ShapeDtypeStruct((b, GDN_HEADS, GDN_DK, GDN_DV), F32)],
        scratch_shapes=[pltpu.VMEM((GDN_HEADS, GDN_DK, GDN_DV), F32)],
        compiler_params=_params("arbitrary", "arbitrary"),
        name="gdn_chunks",
    )(q, k, v, gbc, gbr, s0)


def _gdn_out_kernel(o_ref, z_ref, x_ref, mod_ref, gn_ref, wout_ref, nf_ref, wr_ref, br_ref,
                    x1_ref, hn2_ref, route_ref, cnt_ref):
    _init_counts(cnt_ref)
    o = o_ref[0]
    z = z_ref[0]
    gn = gn_ref[...]
    parts = []
    for h in range(GDN_HEADS):
        s = slice(h * GDN_DV, (h + 1) * GDN_DV)
        parts.append((_rms(o[:, s]) * gn) * _silu(z[:, s]))
    on = jnp.concatenate(parts, axis=1).astype(BF16)
    y = _dot(on, wout_ref[...])
    x1, hn2, route = _post_mixer(x_ref[0], y, mod_ref[0], nf_ref[...], wr_ref[...], br_ref[...], cnt_ref)
    x1_ref[0] = x1
    _store_token_major(hn2_ref, (0,), hn2)
    route_ref[0] = route


def _gdn_out(o, z, x, mod, gn, wout, nf, wr, br, tm):
    b, l, d = x.shape
    row = lambda shape: pl.BlockSpec(shape, lambda i, j: (i, j, 0))
    return pl.pallas_call(
        _gdn_out_kernel,
        grid=(b, l // tm),
        in_specs=[
            row((1, tm, GDN_V_W)), row((1, tm, GDN_V_W)), row((1, tm, d)),
            pl.BlockSpec((1, 6, d), lambda i, j: (i, 0, 0)),
            _const_spec((1, GDN_DV)), _const_spec(wout.shape), _const_spec((1, d)),
            _const_spec(wr.shape), _const_spec((1, LANES)),
        ],
        out_specs=[row((1, tm, d)), _tok_spec(tm), row((1, tm, LANES)), _const_spec((1, LANES))],
        out_shape=[jax.ShapeDtypeStruct((b, l, d), F32), jax.ShapeDtypeStruct((b, l, d // LANES, LANES), F32),
                   jax.ShapeDtypeStruct((b, l, LANES), F32), jax.ShapeDtypeStruct((1, LANES), F32)],
        compiler_params=_params("arbitrary", "arbitrary"),
        name="gdn_out",
    )(o, z, x, mod, gn, wout, nf, wr, br)


def _gelu_tanh(x):
    return 0.5 * x * (1.0 + jnp.tanh(math.sqrt(2.0 / math.pi) * (x + 0.044715 * (x * x * x))))


def _lru_kernel(dest_ref, xa_ref, ys_hbm, rprev_ref, mprev_ref,
                mod_ref, ng_ref, win_ref, cw_ref, cb_ref, wax_ref, ba_ref, bx_ref, lam_ref, wout_ref,
                hist0_ref, h0_ref, nf_ref, wr_ref, br_ref,
                x1_ref, hn2_ref, route_ref, hist_ref, hlast_ref, cnt_ref, xp_ref, h_ref, ybuf_ref, sem_ref, *, tm):
    l = pl.program_id(1)
    _init_counts(cnt_ref)
    gather = _PairGather(dest_ref, ys_hbm, ybuf_ref, sem_ref, tm, n_pieces=8)

    @pl.when(l == 0)
    def _():
        xp_ref[0:8, :] = jnp.zeros((8, LRU_WIDTH), F32)
        xp_ref[5:8, :] = hist0_ref[0]
        h_ref[...] = h0_ref[0]

    ya, yb = gather.start()
    x = _moe_residual(xa_ref[0], ya, yb, rprev_ref[0], mprev_ref[0])
    gather.issue_piece()
    mod = mod_ref[0]
    hn = (_rms(x) * ng_ref[...]) * (1.0 + mod[1:2]) + mod[0:1]
    gather.issue_piece()
    proj = _dot(hn.astype(BF16), win_ref[...])
    gather.issue_piece()
    gate_br = _gelu_tanh(proj[:, :LRU_WIDTH])
    xb = proj[:, LRU_WIDTH:]
    xp_ref[8:8 + tm, :] = xb
    cw = cw_ref[...]
    xc = cw[3:4] * xb
    for j in range(CONV_W - 1):
        xc = xc + cw[j:j + 1] * xp_ref[5 + j:5 + j + tm, :]
    xc = xc + cb_ref[...]
    new_hist = xp_ref[tm + 5:tm + 8, :]
    xp_ref[5:8, :] = new_hist
    hist_ref[0] = new_hist
    gather.issue_piece()
    xcb = xc.astype(BF16)
    ra, ia = [], []
    for h in range(LRU_BLOCKS):
        s = slice(h * LRU_BLOCK_W, (h + 1) * LRU_BLOCK_W)
        r2 = _dot(xcb[:, s], wax_ref[h])
        ra.append(r2[:, :LRU_BLOCK_W])
        ia.append(r2[:, LRU_BLOCK_W:])
    r = _sigmoid(jnp.concatenate(ra, axis=1) + ba_ref[...])
    i = _sigmoid(jnp.concatenate(ia, axis=1) + bx_ref[...])
    gather.issue_piece()
    log_a = (-LRU_C * r) * _softplus(-lam_ref[...])
    a = jnp.exp(log_a)
    mult = jnp.sqrt(-_expm1(2.0 * log_a))
    b = mult * (i * xc)
    sub = lax.broadcasted_iota(jnp.int32, (tm, LRU_WIDTH), 0) % SUBLANES
    sft = 1
    while sft < SUBLANES:
        keep = sub >= sft
        a_prev = jnp.where(keep, pltpu.roll(a, sft, 0), 1.0)
        b_prev = jnp.where(keep, pltpu.roll(b, sft, 0), 0.0)
        b = a * b_prev + b
        a = a * a_prev
        sft *= 2
    gather.issue_piece()
    h = h_ref[...]
    groups = []
    for g in range(tm // SUBLANES):
        rows = slice(g * SUBLANES, (g + 1) * SUBLANES)
        hg = b[rows] + a[rows] * h
        groups.append(hg)
        h = hg[SUBLANES - 1:SUBLANES, :]
    hs = jnp.concatenate(groups, axis=0)
    h_last = h
    h_ref[...] = h_last
    hlast_ref[0] = h_last
    gather.issue_piece()
    y = _dot((hs * gate_br).astype(BF16), wout_ref[...])
    gather.issue_piece()
    x1, hn2, route = _post_mixer(x, y, mod, nf_ref[...], wr_ref[...], br_ref[...], cnt_ref)
    x1_ref[0] = x1
    _store_token_major(hn2_ref, (0,), hn2)
    route_ref[0] = route
    gather.finish()


def _lru_layer(dest, xa, ys, rprev, mprev, mod, ng, win, cw, cb, wax, ba, bx, lam, wout, hist0, h0, nf, wr, br, tm):
    b, l, d = xa.shape
    row = lambda shape: pl.BlockSpec(shape, lambda i, j, *_: (i, j, 0))
    per_b = lambda shape: pl.BlockSpec(shape, lambda i, j, *_: (i, 0, 0))
    vec = _const_spec((1, d))
    grid_spec = pltpu.PrefetchScalarGridSpec(
        num_scalar_prefetch=1,
        grid=(b, l // tm),
        in_specs=[
            row((1, tm, d)), pl.BlockSpec(memory_space=pl.ANY), row((1, tm, LANES)), per_b((1, 6, d)),
            per_b((1, 6, d)), vec, _const_spec(win.shape), _const_spec(cw.shape), vec,
            _const_spec(wax.shape), vec, vec, vec, _const_spec(wout.shape),
            per_b((1, CONV_W - 1, LRU_WIDTH)), per_b((1, 1, LRU_WIDTH)), vec,
            _const_spec(wr.shape), _const_spec((1, LANES)),
        ],
        out_specs=[row((1, tm, d)), _tok_spec(tm), row((1, tm, LANES)),
                   per_b((1, CONV_W - 1, LRU_WIDTH)), per_b((1, 1, LRU_WIDTH)), _const_spec((1, LANES))],
        scratch_shapes=[pltpu.VMEM((tm + 8, LRU_WIDTH), F32), pltpu.VMEM((1, LRU_WIDTH), F32),
                        pltpu.VMEM((2, 2 * tm, d // LANES, LANES), F32), pltpu.SemaphoreType.DMA((2,))],
    )
    return pl.pallas_call(
        functools.partial(_lru_kernel, tm=tm),
        grid_spec=grid_spec,
        out_shape=[jax.ShapeDtypeStruct((b, l, d), F32), jax.ShapeDtypeStruct((b, l, d // LANES, LANES), F32),
                   jax.ShapeDtypeStruct((b, l, LANES), F32),
                   jax.ShapeDtypeStruct((b, CONV_W - 1, LRU_WIDTH), F32),
                   jax.ShapeDtypeStruct((b, 1, LRU_WIDTH), F32), jax.ShapeDtypeStruct((1, LANES), F32)],
        compiler_params=_params("arbitrary", "arbitrary"),
        name="lru_layer",
    )(dest, xa, ys, rprev, mprev, mod, ng, win, cw, cb, wax, ba, bx, lam, wout, hist0, h0, nf, wr, br)


def _moe_kernel(be_ref, st_ref, x_hbm, w1_ref, w3_ref, w2_ref, y_ref, w1b_ref, w3b_ref, w2b_ref, xbuf_ref, sem_ref):
    i = pl.program_id(0)
    last = pl.num_programs(0) - 1
    slot = i % MOE_BUFFERS

    def row_copy(blk, r, buf):
        tok = st_ref[blk * MOE_BLOCK + r]
        return pltpu.make_async_copy(x_hbm.at[tok], xbuf_ref.at[buf, r], sem_ref.at[buf])

    def wait_block(buf):
        pltpu.make_async_copy(x_hbm.at[pl.ds(0, MOE_BLOCK)], xbuf_ref.at[buf], sem_ref.at[buf]).wait()

    @pl.when(i == 0)
    def _():
        for ahead in range(MOE_BUFFERS - 1):
            for r in range(MOE_BLOCK):
                row_copy(jnp.minimum(ahead, last), r, ahead).start(priority=r % N_DMA_QUEUES)

    prev = be_ref[jnp.maximum(i - 1, 0)]

    @pl.when((i == 0) | (be_ref[i] != prev))
    def _():
        w1b_ref[...] = w1_ref[0, 0].astype(BF16)
        w3b_ref[...] = w3_ref[0, 0].astype(BF16)
        w2b_ref[...] = w2_ref[0, 0].astype(BF16)

    nxt = jnp.minimum(i + MOE_BUFFERS - 1, last)

    def body(buf):
        wait_block(buf)
        n_piece = 6
        per = MOE_BLOCK // n_piece + 1
        issued = [0]

        def issue_some():
            hi = min(issued[0] + per, MOE_BLOCK)
            for r in range(issued[0], hi):
                row_copy(nxt, r, (buf + MOE_BUFFERS - 1) % MOE_BUFFERS).start(priority=r % N_DMA_QUEUES)
            issued[0] = hi

        xb = _load_token_major(xbuf_ref, (buf,)).astype(BF16)
        half = D_FF_EXPERT // 2
        hid = []
        for c in range(2):
            cols = slice(c * half, (c + 1) * half)
            issue_some()
            h1 = _dot(xb, w1b_ref[:, cols])
            issue_some()
            h3 = _dot(xb, w3b_ref[:, cols])
            hid.append((_silu(h1) * h3).astype(BF16))
        issue_some()
        y = _dot(hid[0], w2b_ref[0:half, :])
        issue_some()
        _store_token_major(y_ref, (), y + _dot(hid[1], w2b_ref[half:, :]))
        assert issued[0] == MOE_BLOCK

        @pl.when(i == last)
        def _():
            for other in range(1, MOE_BUFFERS):
                wait_block((buf + other) % MOE_BUFFERS)

    for buf in range(MOE_BUFFERS):
        pl.when(slot == buf)(functools.partial(body, buf))


def _moe_blocks(blk_expert, slot_tok, x, w1, w3, w2, layer):
    d = D_MODEL
    p = slot_tok.shape[0]
    nb = p // MOE_BLOCK
    grid_spec = pltpu.PrefetchScalarGridSpec(
        num_scalar_prefetch=2,
        grid=(nb,),
        in_specs=[
            pl.BlockSpec(memory_space=pl.ANY),
            pl.BlockSpec((1, 1, d, D_FF_EXPERT), lambda i, be, st: (layer, be[i], 0, 0)),
            pl.BlockSpec((1, 1, d, D_FF_EXPERT), lambda i, be, st: (layer, be[i], 0, 0)),
            pl.BlockSpec((1, 1, D_FF_EXPERT, d), lambda i, be, st: (layer, be[i], 0, 0)),
        ],
        out_specs=pl.BlockSpec((MOE_BLOCK, d // LANES, LANES), lambda i, be, st: (i, 0, 0)),
        scratch_shapes=[pltpu.VMEM((d, D_FF_EXPERT), BF16), pltpu.VMEM((d, D_FF_EXPERT), BF16),
                        pltpu.VMEM((D_FF_EXPERT, d), BF16), pltpu.VMEM((MOE_BUFFERS, MOE_BLOCK, d // LANES, LANES), F32),
                        pltpu.SemaphoreType.DMA((MOE_BUFFERS,))],
    )
    return pl.pallas_call(
        _moe_kernel,
        grid_spec=grid_spec,
        out_shape=jax.ShapeDtypeStruct((p, d // LANES, LANES), F32),
        compiler_params=_params("arbitrary"),
        name="moe_blocks",
    )(blk_expert, slot_tok, x, w1, w3, w2)


def _moe_residual(x1, ya, yb, route, mod):
    return x1 + mod[5:6] * (ya * route[:, 2:3] + yb * route[:, 3:4])


def _final_kernel(dest_ref, x1_ref, ys_hbm, route_ref, mod_ref, no_ref, o_ref, ybuf_ref, sem_ref, *, tm):
    gather = _PairGather(dest_ref, ys_hbm, ybuf_ref, sem_ref, tm, n_pieces=1)
    ya, yb = gather.start()
    gather.issue_piece()
    x2 = _moe_residual(x1_ref[0], ya, yb, route_ref[0], mod_ref[0])
    o_ref[0] = _rms(x2) * no_ref[...]
    gather.finish()


def _final(dest, x1, ys, route, mod, norm_out, tm):
    b, l, d = x1.shape
    row = lambda shape: pl.BlockSpec(shape, lambda i, j, *_: (i, j, 0))
    grid_spec = pltpu.PrefetchScalarGridSpec(
        num_scalar_prefetch=1,
        grid=(b, l // tm),
        in_specs=[row((1, tm, d)), pl.BlockSpec(memory_space=pl.ANY), row((1, tm, LANES)),
                  pl.BlockSpec((1, 6, d), lambda i, j, *_: (i, 0, 0)), _const_spec((1, d))],
        out_specs=row((1, tm, d)),
        scratch_shapes=[pltpu.VMEM((2, 2 * tm, d // LANES, LANES), F32), pltpu.SemaphoreType.DMA((2,))],
    )
    return pl.pallas_call(
        functools.partial(_final_kernel, tm=tm),
        grid_spec=grid_spec,
        out_shape=jax.ShapeDtypeStruct((b, l, d), F32),
        compiler_params=_params("arbitrary", "arbitrary"),
        name="final",
    )(dest, x1, ys, route, mod, norm_out)


def _moe_experts(hn2, route, counts, w1, w3, w2, layer):
    b, l = hn2.shape[:2]
    d = D_MODEL
    n = b * l
    a = n * EXPERT_TOPK
    rt = route.reshape(n, LANES)
    e_ab = rt[:, 0:EXPERT_TOPK].astype(jnp.int32)
    r_ab = rt[:, 4:4 + EXPERT_TOPK].astype(jnp.int32)
    counts = counts[0, :N_EXPERTS].astype(jnp.int32)
    padded = (counts + MOE_BLOCK - 1) // MOE_BLOCK * MOE_BLOCK
    pad_end = jnp.cumsum(padded)
    pad_start = pad_end - padded
    dest = pad_start[e_ab] + r_ab
    n_blocks = -(-a // MOE_BLOCK) + N_EXPERTS
    p = n_blocks * MOE_BLOCK
    tok = jnp.broadcast_to(jnp.arange(n, dtype=jnp.int32)[:, None], (n, EXPERT_TOPK))
    slot_tok = (jnp.arange(p, dtype=jnp.int32) % n).at[dest.reshape(-1)].set(tok.reshape(-1))
    blk_first = jnp.arange(n_blocks, dtype=jnp.int32) * MOE_BLOCK
    blk_expert = jnp.minimum(jnp.sum((pad_end[None, :] <= blk_first[:, None]).astype(jnp.int32), axis=1),
                             N_EXPERTS - 1)
    ys = _moe_blocks(blk_expert, slot_tok, hn2.reshape(n, d // LANES, LANES), w1, w3, w2, layer)
    return ys, jnp.concatenate([dest[:, 0], dest[:, 1]])


def _trunk(x, mods, gdn_s, gdn_conv, lru_h, lru_conv, wp):
    b, l, d = x.shape
    tm = min(ROW_TILE, l)
    chunk = min(GDN_CHUNK, l)
    assert l % tm == 0 and tm % chunk == 0 and chunk & (chunk - 1) == 0
    mod = mods[0]
    q, k, v, z, gbc, gbr, gconv_new = _gdn_in(x, mod, wp['norm_mix0'], wp['gdn_wqkv'], wp['gdn_wz'], wp['gdn_wab'],
                                              wp['gdn_conv_w'], gdn_conv, wp['gdn_alog'], wp['gdn_dtb'], tm, chunk)
    o, s_new = _gdn_chunks(q, k, v, gbc, gbr, gdn_s, tm, chunk)
    x1, hn2, route, counts = _gdn_out(o, z, x, mod, wp['gdn_norm'], wp['gdn_wout'], wp['norm_ffn0'],
                                      wp['wr0'], wp['br0'], tm)
    ys, dest = _moe_experts(hn2, route, counts, wp['moe_w1'], wp['moe_w3'], wp['moe_w2'], 0)
    x1, hn2, route, lconv_new, h_new, counts = _lru_layer(
        dest, x1, ys, route, mod, mods[1], wp['norm_mix1'], wp['lru_win'], wp['lru_conv_w'], wp['lru_conv_b'],
        wp['lru_wax'], wp['lru_ba'], wp['lru_bx'], wp['lru_lam'], wp['lru_wout'], lru_conv, lru_h, wp['norm_ffn1'],
        wp['wr1'], wp['br1'], tm)
    ys, dest = _moe_experts(hn2, route, counts, wp['moe_w1'], wp['moe_w3'], wp['moe_w2'], 1)
    y = _final(dest, x1, ys, route, mods[1], wp['norm_out'], tm)
    return y, s_new[None], gconv_new[None], h_new.reshape(1, b, LRU_WIDTH), lconv_new[None]


def _pad_lanes(v, width=LANES):
    v = v.reshape(1, -1)
    return jnp.pad(v, ((0, 0), (0, width - v.shape[1])))


def _router_weights(w_rg, b_rg, w_re, b_re):
    w = jnp.pad(jnp.concatenate([w_rg, w_re], axis=1), ((0, 0), (0, LANES - N_GROUPS - N_EXPERTS)))
    return w.astype(BF16), _pad_lanes(jnp.concatenate([b_rg, b_re]))


def kernel(x_prompt, x_sample, state_gdn_S, state_gdn_conv, state_lru_h, state_lru_conv, c_prompt, c_sample, w_ada, b_ada, norm_mix, norm_ffn, norm_out, gdn_w_in, gdn_conv_w, gdn_a_log, gdn_dt_bias, gdn_norm, gdn_w_out, lru_w_in, lru_conv_w, lru_conv_b, lru_w_a, lru_b_a, lru_w_x, lru_b_x, lru_lambda, lru_w_out, moe_w_rg, moe_b_rg, moe_w_re, moe_b_re, moe_w1, moe_w3, moe_w2):
    d = D_MODEL
    bp = x_prompt.shape[0]
    bs = x_sample.shape[0]
    win = gdn_w_in[0]
    wab = jnp.pad(win[:, GDN_CONV_CH + GDN_V_W:], ((0, 0), (0, LANES - 2 * GDN_HEADS)))
    wp = dict(
        norm_mix0=norm_mix[0].reshape(1, d), norm_mix1=norm_mix[1].reshape(1, d),
        norm_ffn0=norm_ffn[0].reshape(1, d), norm_ffn1=norm_ffn[1].reshape(1, d),
        norm_out=norm_out.reshape(1, d),
        gdn_wqkv=win[:, :GDN_CONV_CH].astype(BF16),
        gdn_wz=win[:, GDN_CONV_CH:GDN_CONV_CH + GDN_V_W].astype(BF16),
        gdn_wab=wab.astype(BF16),
        gdn_conv_w=gdn_conv_w[0],
        gdn_alog=_pad_lanes(gdn_a_log[0]), gdn_dtb=_pad_lanes(gdn_dt_bias[0]),
        gdn_norm=gdn_norm[0].reshape(1, GDN_DV), gdn_wout=gdn_w_out[0].astype(BF16),
        lru_win=lru_w_in[0].astype(BF16), lru_conv_w=lru_conv_w[0], lru_conv_b=lru_conv_b[0].reshape(1, d),
        lru_wax=jnp.concatenate([lru_w_a[0], lru_w_x[0]], axis=-1).astype(BF16),
        lru_ba=lru_b_a[0].reshape(1, d), lru_bx=lru_b_x[0].reshape(1, d), lru_lam=lru_lambda[0].reshape(1, d),
        lru_wout=lru_w_out[0].astype(BF16),
        moe_w1=moe_w1, moe_w3=moe_w3, moe_w2=moe_w2,
    )
    for i in range(DEPTH):
        wp[f'wr{i}'], wp[f'br{i}'] = _router_weights(moe_w_rg[i], moe_b_rg[i], moe_w_re[i], moe_b_re[i])

    mods = _ada_mod(jnp.concatenate([c_prompt, c_sample], axis=0), w_ada, b_ada)
    mods = mods.reshape(DEPTH, bp + bs, 6, d)
    mods_p = [mods[i, :bp] for i in range(DEPTH)]
    mods_s = [mods[i, bp:] for i in range(DEPTH)]

    dt = x_prompt.dtype
    z_s = jnp.zeros((bp, GDN_HEADS, GDN_DK, GDN_DV), dt)
    z_gc = jnp.zeros((bp, CONV_W - 1, GDN_CONV_CH), dt)
    z_h = jnp.zeros((bp, 1, LRU_WIDTH), dt)
    z_lc = jnp.zeros((bp, CONV_W - 1, LRU_WIDTH), dt)
    y_p, gs_p, gc_p, lh_p, lc_p = _trunk(x_prompt, mods_p, z_s, z_gc, z_h, z_lc, wp)
    y_s, gs_s, gc_s, lh_s, lc_s = _trunk(x_sample, mods_s, state_gdn_S[0], state_gdn_conv[0],
                                         state_lru_h[0].reshape(bs, 1, LRU_WIDTH), state_lru_conv[0], wp)
    return (y_p, y_s, gs_p, gc_p, lh_p, lc_p, gs_s, gc_s, lh_s, lc_s)
```

```python
import functools
import math

import jax
import jax.numpy as jnp
from jax import lax
from jax.experimental import pallas as pl
from jax.experimental.pallas import tpu as pltpu

F32 = jnp.float32
BF16 = jnp.bfloat16

D_MODEL = 1024
DEPTH = 2
CONV_W = 4
NORM_EPS = 1e-6
GDN_HEADS = 8
GDN_DK = 128
GDN_DV = 128
GDN_QK_W = GDN_HEADS * GDN_DK
GDN_V_W = GDN_HEADS * GDN_DV
GDN_CONV_CH = 2 * GDN_QK_W + GDN_V_W
GDN_CHUNK = 64
GDN_INV_BLOCK = 16
GDN_CHUNK_UNROLL = 4
LRU_WIDTH = D_MODEL
LRU_BLOCKS = 8
LRU_BLOCK_W = LRU_WIDTH // LRU_BLOCKS
LRU_C = 8.0
N_GROUPS = 4
EXPERTS_PER_GROUP = 8
N_EXPERTS = N_GROUPS * EXPERTS_PER_GROUP
EXPERT_TOPK = 2
D_FF_EXPERT = 512
MOE_BLOCK = 256
MOE_BUFFERS = 3
LANES = 128
SUBLANES = 8
ROW_TILE = 256
VMEM_LIMIT = 56 * 1024 * 1024


def _dot(a, b):
    return jnp.dot(a, b, preferred_element_type=F32)


def _dot_nt(a, b):
    return lax.dot_general(a, b, (((1,), (1,)), ((), ())), preferred_element_type=F32)


def _dot_tn(a, b):
    return lax.dot_general(a, b, (((0,), (0,)), ((), ())), preferred_element_type=F32)


def _split3(x):
    a = x.astype(BF16)
    r = x - a.astype(F32)
    b = r.astype(BF16)
    c = (r - b.astype(F32)).astype(BF16)
    return a, b, c


def _rms(x):
    return x * lax.rsqrt(jnp.mean(x * x, axis=-1, keepdims=True) + NORM_EPS)


def _sigmoid(x):
    return 1.0 / (1.0 + jnp.exp(-x))


def _silu(x):
    return x * _sigmoid(x)


def _softplus(x):
    return jnp.maximum(x, 0.0) + jnp.log1p(jnp.exp(-jnp.abs(x)))


def _expm1(x):
    u = jnp.exp(x)
    near = (u > 0.5) & (u < 2.0) & (u != 1.0)
    corrected = (u - 1.0) * x / jnp.where(near, jnp.log(u), 1.0)
    return jnp.where(u == 1.0, x, jnp.where(near, corrected, u - 1.0))


def _params(*sem):
    return pltpu.CompilerParams(dimension_semantics=sem, vmem_limit_bytes=VMEM_LIMIT)


def _const_spec(shape):
    nd = len(shape)
    return pl.BlockSpec(shape, lambda *_: (0,) * nd)


def _ada_kernel(c_ref, w_ref, b_ref, o_ref):
    s = _silu(c_ref[...]).astype(BF16)
    o_ref[0] = _dot(s, w_ref[0].astype(BF16)) + b_ref[0]


def _ada_mod(c_all, w_ada, b_ada):
    bt = c_all.shape[0]
    tn = 1536
    n6 = 6 * D_MODEL
    return pl.pallas_call(
        _ada_kernel,
        grid=(DEPTH, n6 // tn),
        in_specs=[
            pl.BlockSpec((bt, D_MODEL), lambda i, j: (0, 0)),
            pl.BlockSpec((1, D_MODEL, tn), lambda i, j: (i, 0, j)),
            pl.BlockSpec((1, 1, tn), lambda i, j: (i, 0, j)),
        ],
        out_specs=pl.BlockSpec((1, bt, tn), lambda i, j: (i, 0, j)),
        out_shape=jax.ShapeDtypeStruct((DEPTH, bt, n6), F32),
        compiler_params=_params("arbitrary", "arbitrary"),
        name="ada_mod",
    )(c_all, w_ada, b_ada.reshape(DEPTH, 1, n6))


def _route(hn2, wr, br, cnt_ref):
    logits = _dot(hn2.astype(BF16), wr) + br
    lane = lax.broadcasted_iota(jnp.int32, logits.shape, 1)
    neg = jnp.float32(-jnp.inf)
    big = jnp.int32(1 << 20)
    is_g = lane < N_GROUPS
    lg = jnp.where(is_g, logits, neg)
    eg = jnp.exp(lg - jnp.max(lg, axis=-1, keepdims=True))
    pg = eg / jnp.sum(eg, axis=-1, keepdims=True)
    pg = jnp.where(is_g, pg, -1.0)
    gate_g = jnp.max(pg, axis=-1, keepdims=True)
    grp = jnp.min(jnp.where(pg == gate_g, lane, big), axis=-1, keepdims=True)
    lo = N_GROUPS + grp * EXPERTS_PER_GROUP
    is_e = (lane >= lo) & (lane < lo + EXPERTS_PER_GROUP)
    le = jnp.where(is_e, logits, neg)
    ee = jnp.exp(le - jnp.max(le, axis=-1, keepdims=True))
    pe = ee / jnp.sum(ee, axis=-1, keepdims=True)
    pe = jnp.where(is_e, pe, -1.0)
    p1 = jnp.max(pe, axis=-1, keepdims=True)
    i1 = jnp.min(jnp.where(pe == p1, lane, big), axis=-1, keepdims=True)
    pe2 = jnp.where(lane == i1, -1.0, pe)
    p2 = jnp.max(pe2, axis=-1, keepdims=True)
    i2 = jnp.min(jnp.where(pe2 == p2, lane, big), axis=-1, keepdims=True)
    tot = p1 + p2
    w1 = gate_g * (p1 / tot)
    w2 = gate_g * (p2 / tot)
    e1 = i1 - N_GROUPS
    e2 = i2 - N_GROUPS
    tm = logits.shape[0]
    oh1 = lane == e1
    oh2 = lane == e2
    cnt = jnp.where(oh1 | oh2, 1.0, 0.0)
    ri = lax.broadcasted_iota(jnp.int32, (tm, tm), 0)
    ci = lax.broadcasted_iota(jnp.int32, (tm, tm), 1)
    before = jnp.where(ci < ri, 1.0, 0.0).astype(BF16)
    pos = _dot(before, cnt.astype(BF16)) + cnt_ref[...]
    r1 = jnp.sum(jnp.where(oh1, pos, 0.0), axis=-1, keepdims=True)
    r2 = jnp.sum(jnp.where(oh2, pos, 0.0), axis=-1, keepdims=True)
    cnt_ref[...] = cnt_ref[...] + jnp.sum(cnt, axis=0, keepdims=True)
    vals = (e1.astype(F32), e2.astype(F32), w1, w2, r1, r2)
    out = jnp.zeros(logits.shape, F32)
    for j, val in enumerate(vals):
        out = jnp.where(lane == j, val, out)
    return out


def _tok_spec(tm):
    return pl.BlockSpec((1, tm, D_MODEL // LANES, LANES), lambda i, j, *_: (i, j, 0, 0))


def _store_token_major(ref, lead, x):
    for c in range(D_MODEL // LANES):
        ref[lead + (slice(None), c, slice(None))] = x[:, c * LANES:(c + 1) * LANES]


def _load_token_major(ref, lead, rows=slice(None)):
    return jnp.concatenate([ref[lead + (rows, c, slice(None))] for c in range(D_MODEL // LANES)], axis=1)


N_DMA_QUEUES = 2


class _PairGather:
    def __init__(self, dest_ref, ys_hbm, buf_ref, sem_ref, tm, n_pieces):
        self.dest_ref, self.ys_hbm, self.buf_ref, self.sem_ref, self.tm = dest_ref, ys_hbm, buf_ref, sem_ref, tm
        n_steps = pl.num_programs(0) * pl.num_programs(1)
        self.n = n_steps * tm
        self.step = pl.program_id(0) * pl.num_programs(1) + pl.program_id(1)
        self.last = n_steps - 1
        self.slot = self.step % 2
        self.nxt = jnp.minimum(self.step + 1, self.last)
        self.per = -(-2 * tm // n_pieces)
        self.issued = 0

    def _copy(self, tile, j, buf):
        k, r = divmod(j, self.tm)
        row = self.dest_ref[k * self.n + tile * self.tm + r]
        return pltpu.make_async_copy(self.ys_hbm.at[row], self.buf_ref.at[buf, j], self.sem_ref.at[buf])

    def _wait(self, buf):
        pltpu.make_async_copy(self.ys_hbm.at[pl.ds(0, 2 * self.tm)], self.buf_ref.at[buf], self.sem_ref.at[buf]).wait()

    def start(self):
        @pl.when(self.step == 0)
        def _():
            for j in range(2 * self.tm):
                self._copy(0, j, 0).start(priority=j % N_DMA_QUEUES)

        self._wait(self.slot)
        ya = _load_token_major(self.buf_ref, (self.slot,), slice(0, self.tm))
        yb = _load_token_major(self.buf_ref, (self.slot,), slice(self.tm, 2 * self.tm))
        return ya, yb

    def issue_piece(self):
        hi = min(self.issued + self.per, 2 * self.tm)
        for j in range(self.issued, hi):
            self._copy(self.nxt, j, 1 - self.slot).start(priority=j % N_DMA_QUEUES)
        self.issued = hi

    def finish(self):
        assert self.issued == 2 * self.tm

        @pl.when(self.step == self.last)
        def _():
            self._wait(1 - self.slot)


def _init_counts(cnt_ref):
    @pl.when((pl.program_id(0) == 0) & (pl.program_id(1) == 0))
    def _():
        cnt_ref[...] = jnp.zeros(cnt_ref.shape, F32)


def _post_mixer(x, y, mod, nf, wr, br, cnt_ref):
    g1 = mod[2:3]
    sh2 = mod[3:4]
    sc2 = mod[4:5]
    x1 = x + g1 * y
    hn2 = (_rms(x1) * nf) * (1.0 + sc2) + sh2
    return x1, hn2, _route(hn2, wr, br, cnt_ref)


def _gdn_in_kernel(x_ref, mod_ref, ng_ref, wqkv_ref, wz_ref, wab_ref, cw_ref, hist0_ref, alog_ref, dtb_ref,
                   q_ref, k_ref, v_ref, z_ref, gbc_ref, gbr_ref, hist_ref, xp_ref, *, tm, chunk):
    l = pl.program_id(1)

    @pl.when(l == 0)
    def _():
        xp_ref[0:8, :] = jnp.zeros((8, GDN_CONV_CH), F32)
        xp_ref[5:8, :] = hist0_ref[0]

    x = x_ref[0]
    mod = mod_ref[0]
    hn = (_rms(x) * ng_ref[...]) * (1.0 + mod[1:2]) + mod[0:1]
    hb = hn.astype(BF16)
    qkv = _dot(hb, wqkv_ref[...])
    xp_ref[8:8 + tm, :] = qkv
    cw = cw_ref[...]
    y = cw[3:4] * qkv
    for j in range(CONV_W - 1):
        y = y + cw[j:j + 1] * xp_ref[5 + j:5 + j + tm, :]
    new_hist = xp_ref[tm + 5:tm + 8, :]
    xp_ref[5:8, :] = new_hist
    hist_ref[0] = new_hist
    y = _silu(y)
    for h in range(GDN_HEADS):
        s = slice(h * GDN_DK, (h + 1) * GDN_DK)
        qh = y[:, s]
        q_ref[0, :, s] = qh * lax.rsqrt(jnp.sum(qh * qh, axis=-1, keepdims=True) + 1e-6) * (GDN_DK ** -0.5)
        kh = y[:, GDN_QK_W + h * GDN_DK:GDN_QK_W + (h + 1) * GDN_DK]
        k_ref[0, :, s] = kh * lax.rsqrt(jnp.sum(kh * kh, axis=-1, keepdims=True) + 1e-6)
    v_ref[0] = y[:, 2 * GDN_QK_W:]
    z_ref[0] = _dot(hb, wz_ref[...])
    ab = _dot(hb, wab_ref[...])
    g = -jnp.exp(alog_ref[...]) * _softplus(ab + dtb_ref[...])
    beta = _sigmoid(ab)
    ri = lax.broadcasted_iota(jnp.int32, (tm, tm), 0)
    ci = lax.broadcasted_iota(jnp.int32, (tm, tm), 1)
    tri = jnp.where((ri // chunk == ci // chunk) & (ci <= ri), 1.0, 0.0).astype(BF16)
    g1, g2, g3 = _split3(g)
    gcum = (_dot(tri, g1) + _dot(tri, g2)) + _dot(tri, g3)
    lane = lax.broadcasted_iota(jnp.int32, (tm, LANES), 1)
    gb = jnp.where(lane < GDN_HEADS, gcum, beta)
    gbc_ref[0] = gb
    er = lax.broadcasted_iota(jnp.int32, (2 * GDN_HEADS, LANES), 0)
    ec = lax.broadcasted_iota(jnp.int32, (2 * GDN_HEADS, LANES), 1)
    sel = jnp.where(er == ec, 1.0, 0.0).astype(BF16)
    b1, b2, b3 = _split3(gb)
    for n in range(tm // chunk):
        r = slice(n * chunk, (n + 1) * chunk)
        gbr_ref[0, n] = (_dot_nt(sel, b1[r]) + _dot_nt(sel, b2[r])) + _dot_nt(sel, b3[r])


def _gdn_in(x, mod, ng, wqkv, wz, wab, cw, hist0, alog, dtb, tm, chunk):
    b, l, d = x.shape
    grid = (b, l // tm)
    row = lambda shape: pl.BlockSpec(shape, lambda i, j: (i, j, 0))
    outs = pl.pallas_call(
        functools.partial(_gdn_in_kernel, tm=tm, chunk=chunk),
        grid=grid,
        in_specs=[
            row((1, tm, d)),
            pl.BlockSpec((1, 6, d), lambda i, j: (i, 0, 0)),
            _const_spec((1, d)),
            _const_spec(wqkv.shape),
            _const_spec(wz.shape),
            _const_spec(wab.shape),
            _const_spec(cw.shape),
            pl.BlockSpec((1, CONV_W - 1, GDN_CONV_CH), lambda i, j: (i, 0, 0)),
            _const_spec((1, LANES)),
            _const_spec((1, LANES)),
        ],
        out_specs=[
            row((1, tm, GDN_QK_W)), row((1, tm, GDN_QK_W)), row((1, tm, GDN_V_W)), row((1, tm, GDN_V_W)),
            row((1, tm, LANES)),
            pl.BlockSpec((1, tm // chunk, 2 * GDN_HEADS, chunk), lambda i, j: (i, j, 0, 0)),
            pl.BlockSpec((1, CONV_W - 1, GDN_CONV_CH), lambda i, j: (i, 0, 0)),
        ],
        out_shape=[
            jax.ShapeDtypeStruct((b, l, GDN_QK_W), F32), jax.ShapeDtypeStruct((b, l, GDN_QK_W), F32),
            jax.ShapeDtypeStruct((b, l, GDN_V_W), F32), jax.ShapeDtypeStruct((b, l, GDN_V_W), F32),
            jax.ShapeDtypeStruct((b, l, LANES), F32),
            jax.ShapeDtypeStruct((b, l // chunk, 2 * GDN_HEADS, chunk), F32),
            jax.ShapeDtypeStruct((b, CONV_W - 1, GDN_CONV_CH), F32),
        ],
        scratch_shapes=[pltpu.VMEM((tm + 8, GDN_CONV_CH), F32)],
        compiler_params=_params("arbitrary", "arbitrary"),
        name="gdn_in",
    )(x, mod, ng, wqkv, wz, wab, cw, hist0, alog, dtb)
    return outs


def _gdn_chunk_kernel(q_ref, k_ref, v_ref, gbc_ref, gbr_ref, s0_ref, o_ref, sout_ref, s_ref, *, tm, chunk):
    l = pl.program_id(1)

    @pl.when(l == 0)
    def _():
        s_ref[...] = s0_ref[0]

    c2 = 2 * chunk
    n_pairs = GDN_HEADS // 2
    ri = lax.broadcasted_iota(jnp.int32, (chunk, c2), 0)
    cl = lax.broadcasted_iota(jnp.int32, (chunk, c2), 1)
    ci = cl % chunk
    left = cl < chunk
    incl = ri >= ci
    strict = ri > ci
    eye = jnp.where(ri == ci, 1.0, 0.0).astype(F32)
    base = min(GDN_INV_BLOCK, chunk)
    levels = int(math.log2(base))
    diag_blk = (ri // base) == (ci // base)
    merge_masks = []
    blk = base
    while blk < chunk:
        merge_masks.append(((ri // (2 * blk)) == (ci // (2 * blk))) & ((ri // blk) % 2 == 1) & ((ci // blk) % 2 == 0))
        blk *= 2
    heads = range(GDN_HEADS)
    pairs = range(n_pairs)

    def bdiag(x):
        return jnp.concatenate([jnp.where(left, x, 0.0), jnp.where(left, 0.0, x)], axis=0).astype(BF16)

    def bdiag2(x0, x1):
        z = jnp.zeros_like(x0)
        return jnp.concatenate([jnp.concatenate([x0, z], axis=1), jnp.concatenate([z, x1], axis=1)], axis=0).astype(BF16)

    unroll = min(GDN_CHUNK_UNROLL, tm // chunk)
    assert (tm // chunk) % unroll == 0
    vheads = range(unroll * GDN_HEADS)
    vpairs = range(unroll * n_pairs)
    hs = [slice(h * GDN_DK, (h + 1) * GDN_DK) for h in heads]

    def chunk_group(g, carry):
        rows = [pl.ds(pl.multiple_of((g * unroll + c) * chunk, chunk), chunk) for c in range(unroll)]
        gbc = [gbc_ref[0, rows[c], :] for c in range(unroll)]
        gbr = [gbr_ref[0, g * unroll + c] for c in range(unroll)]
        q = [q_ref[0, rows[vh // GDN_HEADS], hs[vh % GDN_HEADS]] for vh in vheads]
        k = [k_ref[0, rows[vh // GDN_HEADS], hs[vh % GDN_HEADS]] for vh in vheads]
        v = [v_ref[0, rows[vh // GDN_HEADS], hs[vh % GDN_HEADS]] for vh in vheads]
        gc = [gbc[vh // GDN_HEADS][:, vh % GDN_HEADS:vh % GDN_HEADS + 1] for vh in vheads]
        bc = [gbc[vh // GDN_HEADS][:, GDN_HEADS + vh % GDN_HEADS:GDN_HEADS + vh % GDN_HEADS + 1] for vh in vheads]
        kb = [k[vh] * bc[vh] for vh in vheads]
        a2 = [_dot_nt(jnp.concatenate([q[vh], kb[vh]], axis=0).astype(BF16), k[vh].astype(BF16)) for vh in vheads]
        attn, m = [], []
        for vp in vpairs:
            v0, v1 = 2 * vp, 2 * vp + 1
            h0, h1 = v0 % GDN_HEADS, v1 % GDN_HEADS
            gr = gbr[vp // n_pairs]
            gcp = jnp.where(left, gc[v0], gc[v1])
            grp = jnp.concatenate([gr[h0:h0 + 1, :], gr[h1:h1 + 1, :]], axis=1)
            decay = jnp.where(incl, jnp.exp(jnp.where(incl, gcp - grp, 0.0)), 0.0)
            ap = jnp.concatenate([a2[v0], a2[v1]], axis=1)
            attn.append(ap[:chunk] * decay)
            m.append(jnp.where(strict, ap[chunk:] * decay, 0.0))
        nq = [jnp.where(diag_blk, -m[vp], 0.0) for vp in vpairs]
        p = [eye + nq[vp] for vp in vpairs]
        nq = [_dot(nq[vp].astype(BF16), bdiag(nq[vp])) for vp in vpairs]
        for j in range(1, levels):
            if j < levels - 1:
                r = [_dot(nq[vp].astype(BF16), jnp.concatenate([bdiag(nq[vp]), bdiag(p[vp])], axis=1)) for vp in vpairs]
                nq = [r[vp][:, :c2] for vp in vpairs]
                p = [p[vp] + r[vp][:, c2:] for vp in vpairs]
            else:
                p = [p[vp] + _dot(nq[vp].astype(BF16), bdiag(p[vp])) for vp in vpairs]
        for low in merge_masks:
            t = [_dot(jnp.where(low, m[vp], 0.0).astype(BF16), bdiag(p[vp])) for vp in vpairs]
            p = [p[vp] - _dot(p[vp].astype(BF16), bdiag(t[vp])) for vp in vpairs]
        eg = [jnp.exp(gc[vh]) for vh in vheads]
        rhs = [jnp.concatenate([kb[vh] * eg[vh], v[vh] * bc[vh]], axis=1) for vh in vheads]
        wu = [_dot(p[vp].astype(BF16), bdiag2(rhs[2 * vp], rhs[2 * vp + 1])) for vp in vpairs]
        wd = GDN_DK + GDN_DV
        w = [wu[vh // 2][:, (vh % 2) * wd:(vh % 2) * wd + GDN_DK] for vh in vheads]
        u = [wu[vh // 2][:, (vh % 2) * wd + GDN_DK:(vh % 2 + 1) * wd] for vh in vheads]
        lhs = [jnp.concatenate([w[vh], q[vh] * eg[vh]], axis=0).astype(BF16) for vh in vheads]
        for c in range(unroll):
            base = c * GDN_HEADS
            st = [s_ref[h] for h in heads]
            ws = [_dot(lhs[base + h], st[h].astype(BF16)) for h in heads]
            v_new = [u[base + h] - ws[h][:chunk] for h in heads]
            av = [_dot(attn[c * n_pairs + hp].astype(BF16), bdiag2(v_new[2 * hp], v_new[2 * hp + 1])) for hp in pairs]
            for h in heads:
                o_ref[0, rows[c], hs[h]] = ws[h][chunk:] + av[h // 2][:, (h % 2) * GDN_DV:(h % 2 + 1) * GDN_DV]
            for h in heads:
                g_last = gc[base + h][chunk - 1:chunk, :]
                kd = k[base + h] * jnp.exp(g_last - gc[base + h])
                s_ref[h] = st[h] * jnp.exp(g_last) + _dot_tn(kd.astype(BF16), v_new[h].astype(BF16))
        return carry

    lax.fori_loop(0, tm // (chunk * unroll), chunk_group, 0)
    sout_ref[0] = s_ref[...]


def _gdn_chunks(q, k, v, gbc, gbr, s0, tm, chunk):
    b, l, _ = q.shape
    row = lambda shape: pl.BlockSpec(shape, lambda i, j: (i, j, 0))
    sspec = pl.BlockSpec((1, GDN_HEADS, GDN_DK, GDN_DV), lambda i, j: (i, 0, 0, 0))
    return pl.pallas_call(
        functools.partial(_gdn_chunk_kernel, tm=tm, chunk=chunk),
        grid=(b, l // tm),
        in_specs=[
            row((1, tm, GDN_QK_W)), row((1, tm, GDN_QK_W)), row((1, tm, GDN_V_W)), row((1, tm, LANES)),
            pl.BlockSpec((1, tm // chunk, 2 * GDN_HEADS, chunk), lambda i, j: (i, j, 0, 0)),
            sspec,
        ],
        out_specs=[row((1, tm, GDN_V_W)), sspec],
        out_shape=[jax.ShapeDtypeStruct((b, l, GDN_V_W), F32),
                   jax.ShapeDtypeStruct((b, GDN_HEADS, GDN_DK, GDN_DV), F32)],
        scratch_shapes=[pltpu.VMEM((GDN_HEADS, GDN_DK, GDN_DV), F32)],
        compiler_params=_params("arbitrary", "arbitrary"),
        name="gdn_chunks",
    )(q, k, v, gbc, gbr, s0)


def _gdn_out_kernel(o_ref, z_ref, x_ref, mod_ref, gn_ref, wout_ref, nf_ref, wr_ref, br_ref,
                    x1_ref, hn2_ref, route_ref, cnt_ref):
    _init_counts(cnt_ref)
    o = o_ref[0]
    z = z_ref[0]
    gn = gn_ref[...]
    parts = []
    for h in range(GDN_HEADS):
        s = slice(h * GDN_DV, (h + 1) * GDN_DV)
        parts.append((_rms(o[:, s]) * gn) * _silu(z[:, s]))
    on = jnp.concatenate(parts, axis=1).astype(BF16)
    y = _dot(on, wout_ref[...])
    x1, hn2, route = _post_mixer(x_ref[0], y, mod_ref[0], nf_ref[...], wr_ref[...], br_ref[...], cnt_ref)
    x1_ref[0] = x1
    _store_token_major(hn2_ref, (0,), hn2)
    route_ref[0] = route


def _gdn_out(o, z, x, mod, gn, wout, nf, wr, br, tm):
    b, l, d = x.shape
    row = lambda shape: pl.BlockSpec(shape, lambda i, j: (i, j, 0))
    return pl.pallas_call(
        _gdn_out_kernel,
        grid=(b, l // tm),
        in_specs=[
            row((1, tm, GDN_V_W)), row((1, tm, GDN_V_W)), row((1, tm, d)),
            pl.BlockSpec((1, 6, d), lambda i, j: (i, 0, 0)),
            _const_spec((1, GDN_DV)), _const_spec(wout.shape), _const_spec((1, d)),
            _const_spec(wr.shape), _const_spec((1, LANES)),
        ],
        out_specs=[row((1, tm, d)), _tok_spec(tm), row((1, tm, LANES)), _const_spec((1, LANES))],
        out_shape=[jax.ShapeDtypeStruct((b, l, d), F32), jax.ShapeDtypeStruct((b, l, d // LANES, LANES), F32),
                   jax.ShapeDtypeStruct((b, l, LANES), F32), jax.ShapeDtypeStruct((1, LANES), F32)],
        compiler_params=_params("arbitrary", "arbitrary"),
        name="gdn_out",
    )(o, z, x, mod, gn, wout, nf, wr, br)


def _gelu_tanh(x):
    return 0.5 * x * (1.0 + jnp.tanh(math.sqrt(2.0 / math.pi) * (x + 0.044715 * (x * x * x))))


def _lru_kernel(dest_ref, xa_ref, ys_hbm, rprev_ref, mprev_ref,
                mod_ref, ng_ref, win_ref, cw_ref, cb_ref, wax_ref, ba_ref, bx_ref, lam_ref, wout_ref,
                hist0_ref, h0_ref, nf_ref, wr_ref, br_ref,
                x1_ref, hn2_ref, route_ref, hist_ref, hlast_ref, cnt_ref, xp_ref, h_ref, ybuf_ref, sem_ref, *, tm):
    l = pl.program_id(1)
    _init_counts(cnt_ref)
    gather = _PairGather(dest_ref, ys_hbm, ybuf_ref, sem_ref, tm, n_pieces=8)

    @pl.when(l == 0)
    def _():
        xp_ref[0:8, :] = jnp.zeros((8, LRU_WIDTH), F32)
        xp_ref[5:8, :] = hist0_ref[0]
        h_ref[...] = h0_ref[0]

    ya, yb = gather.start()
    x = _moe_residual(xa_ref[0], ya, yb, rprev_ref[0], mprev_ref[0])
    gather.issue_piece()
    mod = mod_ref[0]
    hn = (_rms(x) * ng_ref[...]) * (1.0 + mod[1:2]) + mod[0:1]
    gather.issue_piece()
    proj = _dot(hn.astype(BF16), win_ref[...])
    gather.issue_piece()
    gate_br = _gelu_tanh(proj[:, :LRU_WIDTH])
    xb = proj[:, LRU_WIDTH:]
    xp_ref[8:8 + tm, :] = xb
    cw = cw_ref[...]
    xc = cw[3:4] * xb
    for j in range(CONV_W - 1):
        xc = xc + cw[j:j + 1] * xp_ref[5 + j:5 + j + tm, :]
    xc = xc + cb_ref[...]
    new_hist = xp_ref[tm + 5:tm + 8, :]
    xp_ref[5:8, :] = new_hist
    hist_ref[0] = new_hist
    gather.issue_piece()
    xcb = xc.astype(BF16)
    ra, ia = [], []
    for h in range(LRU_BLOCKS):
        s = slice(h * LRU_BLOCK_W, (h + 1) * LRU_BLOCK_W)
        r2 = _dot(xcb[:, s], wax_ref[h])
        ra.append(r2[:, :LRU_BLOCK_W])
        ia.append(r2[:, LRU_BLOCK_W:])
    r = _sigmoid(jnp.concatenate(ra, axis=1) + ba_ref[...])
    i = _sigmoid(jnp.concatenate(ia, axis=1) + bx_ref[...])
    gather.issue_piece()
    log_a = (-LRU_C * r) * _softplus(-lam_ref[...])
    a = jnp.exp(log_a)
    mult = jnp.sqrt(-_expm1(2.0 * log_a))
    b = mult * (i * xc)
    sub = lax.broadcasted_iota(jnp.int32, (tm, LRU_WIDTH), 0) % SUBLANES
    sft = 1
    while sft < SUBLANES:
        keep = sub >= sft
        a_prev = jnp.where(keep, pltpu.roll(a, sft, 0), 1.0)
        b_prev = jnp.where(keep, pltpu.roll(b, sft, 0), 0.0)
        b = a * b_prev + b
        a = a * a_prev
        sft *= 2
    gather.issue_piece()
    h = h_ref[...]
    groups = []
    for g in range(tm // SUBLANES):
        rows = slice(g * SUBLANES, (g + 1) * SUBLANES)
        hg = b[rows] + a[rows] * h
        groups.append(hg)
        h = hg[SUBLANES - 1:SUBLANES, :]
    hs = jnp.concatenate(groups, axis=0)
    h_last = h
    h_ref[...] = h_last
    hlast_ref[0] = h_last
    gather.issue_piece()
    y = _dot((hs * gate_br).astype(BF16), wout_ref[...])
    gather.issue_piece()
    x1, hn2, route = _post_mixer(x, y, mod, nf_ref[...], wr_ref[...], br_ref[...], cnt_ref)
    x1_ref[0] = x1
    _store_token_major(hn2_ref, (0,), hn2)
    route_ref[0] = route
    gather.finish()


def _lru_layer(dest, xa, ys, rprev, mprev, mod, ng, win, cw, cb, wax, ba, bx, lam, wout, hist0, h0, nf, wr, br, tm):
    b, l, d = xa.shape
    row = lambda shape: pl.BlockSpec(shape, lambda i, j, *_: (i, j, 0))
    per_b = lambda shape: pl.BlockSpec(shape, lambda i, j, *_: (i, 0, 0))
    vec = _const_spec((1, d))
    grid_spec = pltpu.PrefetchScalarGridSpec(
        num_scalar_prefetch=1,
        grid=(b, l // tm),
        in_specs=[
            row((1, tm, d)), pl.BlockSpec(memory_space=pl.ANY), row((1, tm, LANES)), per_b((1, 6, d)),
            per_b((1, 6, d)), vec, _const_spec(win.shape), _const_spec(cw.shape), vec,
            _const_spec(wax.shape), vec, vec, vec, _const_spec(wout.shape),
            per_b((1, CONV_W - 1, LRU_WIDTH)), per_b((1, 1, LRU_WIDTH)), vec,
            _const_spec(wr.shape), _const_spec((1, LANES)),
        ],
        out_specs=[row((1, tm, d)), _tok_spec(tm), row((1, tm, LANES)),
                   per_b((1, CONV_W - 1, LRU_WIDTH)), per_b((1, 1, LRU_WIDTH)), _const_spec((1, LANES))],
        scratch_shapes=[pltpu.VMEM((tm + 8, LRU_WIDTH), F32), pltpu.VMEM((1, LRU_WIDTH), F32),
                        pltpu.VMEM((2, 2 * tm, d // LANES, LANES), F32), pltpu.SemaphoreType.DMA((2,))],
    )
    return pl.pallas_call(
        functools.partial(_lru_kernel, tm=tm),
        grid_spec=grid_spec,
        out_shape=[jax.ShapeDtypeStruct((b, l, d), F32), jax.ShapeDtypeStruct((b, l, d // LANES, LANES), F32),
                   jax.ShapeDtypeStruct((b, l, LANES), F32),
                   jax.ShapeDtypeStruct((b, CONV_W - 1, LRU_WIDTH), F32),
                   jax.ShapeDtypeStruct((b, 1, LRU_WIDTH), F32), jax.ShapeDtypeStruct((1, LANES), F32)],
        compiler_params=_params("arbitrary", "arbitrary"),
        name="lru_layer",
    )(dest, xa, ys, rprev, mprev, mod, ng, win, cw, cb, wax, ba, bx, lam, wout, hist0, h0, nf, wr, br)


def _moe_kernel(be_ref, st_ref, x_hbm, w1_ref, w3_ref, w2_ref, y_ref, w1b_ref, w3b_ref, w2b_ref, xbuf_ref, sem_ref):
    i = pl.program_id(0)
    last = pl.num_programs(0) - 1
    slot = i % MOE_BUFFERS

    def row_copy(blk, r, buf):
        tok = st_ref[blk * MOE_BLOCK + r]
        return pltpu.make_async_copy(x_hbm.at[tok], xbuf_ref.at[buf, r], sem_ref.at[buf])

    def wait_block(buf):
        pltpu.make_async_copy(x_hbm.at[pl.ds(0, MOE_BLOCK)], xbuf_ref.at[buf], sem_ref.at[buf]).wait()

    @pl.when(i == 0)
    def _():
        for ahead in range(MOE_BUFFERS - 1):
            for r in range(MOE_BLOCK):
                row_copy(jnp.minimum(ahead, last), r, ahead).start(priority=r % N_DMA_QUEUES)

    prev = be_ref[jnp.maximum(i - 1, 0)]

    @pl.when((i == 0) | (be_ref[i] != prev))
    def _():
        w1b_ref[...] = w1_ref[0, 0].astype(BF16)
        w3b_ref[...] = w3_ref[0, 0].astype(BF16)
        w2b_ref[...] = w2_ref[0, 0].astype(BF16)

    nxt = jnp.minimum(i + MOE_BUFFERS - 1, last)

    def body(buf):
        wait_block(buf)
        n_piece = 6
        per = MOE_BLOCK // n_piece + 1
        issued = [0]

        def issue_some():
            hi = min(issued[0] + per, MOE_BLOCK)
            for r in range(issued[0], hi):
                row_copy(nxt, r, (buf + MOE_BUFFERS - 1) % MOE_BUFFERS).start(priority=r % N_DMA_QUEUES)
            issued[0] = hi

        xb = _load_token_major(xbuf_ref, (buf,)).astype(BF16)
        half = D_FF_EXPERT // 2
        hid = []
        for c in range(2):
            cols = slice(c * half, (c + 1) * half)
            issue_some()
            h1 = _dot(xb, w1b_ref[:, cols])
            issue_some()
            h3 = _dot(xb, w3b_ref[:, cols])
            hid.append((_silu(h1) * h3).astype(BF16))
        issue_some()
        y = _dot(hid[0], w2b_ref[0:half, :])
        issue_some()
        _store_token_major(y_ref, (), y + _dot(hid[1], w2b_ref[half:, :]))
        assert issued[0] == MOE_BLOCK

        @pl.when(i == last)
        def _():
            for other in range(1, MOE_BUFFERS):
                wait_block((buf + other) % MOE_BUFFERS)

    for buf in range(MOE_BUFFERS):
        pl.when(slot == buf)(functools.partial(body, buf))


def _moe_blocks(blk_expert, slot_tok, x, w1, w3, w2, layer):
    d = D_MODEL
    p = slot_tok.shape[0]
    nb = p // MOE_BLOCK
    grid_spec = pltpu.PrefetchScalarGridSpec(
        num_scalar_prefetch=2,
        grid=(nb,),
        in_specs=[
            pl.BlockSpec(memory_space=pl.ANY),
            pl.BlockSpec((1, 1, d, D_FF_EXPERT), lambda i, be, st: (layer, be[i], 0, 0)),
            pl.BlockSpec((1, 1, d, D_FF_EXPERT), lambda i, be, st: (layer, be[i], 0, 0)),
            pl.BlockSpec((1, 1, D_FF_EXPERT, d), lambda i, be, st: (layer, be[i], 0, 0)),
        ],
        out_specs=pl.BlockSpec((MOE_BLOCK, d // LANES, LANES), lambda i, be, st: (i, 0, 0)),
        scratch_shapes=[pltpu.VMEM((d, D_FF_EXPERT), BF16), pltpu.VMEM((d, D_FF_EXPERT), BF16),
                        pltpu.VMEM((D_FF_EXPERT, d), BF16), pltpu.VMEM((MOE_BUFFERS, MOE_BLOCK, d // LANES, LANES), F32),
                        pltpu.SemaphoreType.DMA((MOE_BUFFERS,))],
    )
    return pl.pallas_call(
        _moe_kernel,
        grid_spec=grid_spec,
        out_shape=jax.ShapeDtypeStruct((p, d // LANES, LANES), F32),
        compiler_params=_params("arbitrary"),
        name="moe_blocks",
    )(blk_expert, slot_tok, x, w1, w3, w2)


def _moe_residual(x1, ya, yb, route, mod):
    return x1 + mod[5:6] * (ya * route[:, 2:3] + yb * route[:, 3:4])


def _final_kernel(dest_ref, x1_ref, ys_hbm, route_ref, mod_ref, no_ref, o_ref, ybuf_ref, sem_ref, *, tm):
    gather = _PairGather(dest_ref, ys_hbm, ybuf_ref, sem_ref, tm, n_pieces=1)
    ya, yb = gather.start()
    gather.issue_piece()
    x2 = _moe_residual(x1_ref[0], ya, yb, route_ref[0], mod_ref[0])
    o_ref[0] = _rms(x2) * no_ref[...]
    gather.finish()


def _final(dest, x1, ys, route, mod, norm_out, tm):
    b, l, d = x1.shape
    row = lambda shape: pl.BlockSpec(shape, lambda i, j, *_: (i, j, 0))
    grid_spec = pltpu.PrefetchScalarGridSpec(
        num_scalar_prefetch=1,
        grid=(b, l // tm),
        in_specs=[row((1, tm, d)), pl.BlockSpec(memory_space=pl.ANY), row((1, tm, LANES)),
                  pl.BlockSpec((1, 6, d), lambda i, j, *_: (i, 0, 0)), _const_spec((1, d))],
        out_specs=row((1, tm, d)),
        scratch_shapes=[pltpu.VMEM((2, 2 * tm, d // LANES, LANES), F32), pltpu.SemaphoreType.DMA((2,))],
    )
    return pl.pallas_call(
        functools.partial(_final_kernel, tm=tm),
        grid_spec=grid_spec,
        out_shape=jax.ShapeDtypeStruct((b, l, d), F32),
        compiler_params=_params("arbitrary", "arbitrary"),
        name="final",
    )(dest, x1, ys, route, mod, norm_out)


def _moe_experts(hn2, route, counts, w1, w3, w2, layer):
    b, l = hn2.shape[:2]
    d = D_MODEL
    n = b * l
    a = n * EXPERT_TOPK
    rt = route.reshape(n, LANES)
    e_ab = rt[:, 0:EXPERT_TOPK].astype(jnp.int32)
    r_ab = rt[:, 4:4 + EXPERT_TOPK].astype(jnp.int32)
    counts = counts[0, :N_EXPERTS].astype(jnp.int32)
    padded = (counts + MOE_BLOCK - 1) // MOE_BLOCK * MOE_BLOCK
    pad_end = jnp.cumsum(padded)
    pad_start = pad_end - padded
    dest = pad_start[e_ab] + r_ab
    n_blocks = -(-a // MOE_BLOCK) + N_EXPERTS
    p = n_blocks * MOE_BLOCK
    tok = jnp.broadcast_to(jnp.arange(n, dtype=jnp.int32)[:, None], (n, EXPERT_TOPK))
    slot_tok = (jnp.arange(p, dtype=jnp.int32) % n).at[dest.reshape(-1)].set(tok.reshape(-1))
    blk_first = jnp.arange(n_blocks, dtype=jnp.int32) * MOE_BLOCK
    blk_expert = jnp.minimum(jnp.sum((pad_end[None, :] <= blk_first[:, None]).astype(jnp.int32), axis=1),
                             N_EXPERTS - 1)
    ys = _moe_blocks(blk_expert, slot_tok, hn2.reshape(n, d // LANES, LANES), w1, w3, w2, layer)
    return ys, jnp.concatenate([dest[:, 0], dest[:, 1]])


def _trunk(x, mods, gdn_s, gdn_conv, lru_h, lru_conv, wp):
    b, l, d = x.shape
    tm = min(ROW_TILE, l)
    chunk = min(GDN_CHUNK, l)
    assert l % tm == 0 and tm % chunk == 0 and chunk & (chunk - 1) == 0
    mod = mods[0]
    q, k, v, z, gbc, gbr, gconv_new = _gdn_in(x, mod, wp['norm_mix0'], wp['gdn_wqkv'], wp['gdn_wz'], wp['gdn_wab'],
                                              wp['gdn_conv_w'], gdn_conv, wp['gdn_alog'], wp['gdn_dtb'], tm, chunk)
    o, s_new = _gdn_chunks(q, k, v, gbc, gbr, gdn_s, tm, chunk)
    x1, hn2, route, counts = _gdn_out(o, z, x, mod, wp['gdn_norm'], wp['gdn_wout'], wp['norm_ffn0'],
                                      wp['wr0'], wp['br0'], tm)
    ys, dest = _moe_experts(hn2, route, counts, wp['moe_w1'], wp['moe_w3'], wp['moe_w2'], 0)
    x1, hn2, route, lconv_new, h_new, counts = _lru_layer(
        dest, x1, ys, route, mod, mods[1], wp['norm_mix1'], wp['lru_win'], wp['lru_conv_w'], wp['lru_conv_b'],
        wp['lru_wax'], wp['lru_ba'], wp['lru_bx'], wp['lru_lam'], wp['lru_wout'], lru_conv, lru_h, wp['norm_ffn1'],
        wp['wr1'], wp['br1'], tm)
    ys, dest = _moe_experts(hn2, route, counts, wp['moe_w1'], wp['moe_w3'], wp['moe_w2'], 1)
    y = _final(dest, x1, ys, route, mods[1], wp['norm_out'], tm)
    return y, s_new[None], gconv_new[None], h_new.reshape(1, b, LRU_WIDTH), lconv_new[None]


def _pad_lanes(v, width=LANES):
    v = v.reshape(1, -1)
    return jnp.pad(v, ((0, 0), (0, width - v.shape[1])))


def _router_weights(w_rg, b_rg, w_re, b_re):
    w = jnp.pad(jnp.concatenate([w_rg, w_re], axis=1), ((0, 0), (0, LANES - N_GROUPS - N_EXPERTS)))
    return w.astype(BF16), _pad_lanes(jnp.concatenate([b_rg, b_re]))


def kernel(x_prompt, x_sample, state_gdn_S, state_gdn_conv, state_lru_h, state_lru_conv, c_prompt, c_sample, w_ada, b_ada, norm_mix, norm_ffn, norm_out, gdn_w_in, gdn_conv_w, gdn_a_log, gdn_dt_bias, gdn_norm, gdn_w_out, lru_w_in, lru_conv_w, lru_conv_b, lru_w_a, lru_b_a, lru_w_x, lru_b_x, lru_lambda, lru_w_out, moe_w_rg, moe_b_rg, moe_w_re, moe_b_re, moe_w1, moe_w3, moe_w2):
    d = D_MODEL
    bp = x_prompt.shape[0]
    bs = x_sample.shape[0]
    win = gdn_w_in[0]
    wab = jnp.pad(win[:, GDN_CONV_CH + GDN_V_W:], ((0, 0), (0, LANES - 2 * GDN_HEADS)))
    wp = dict(
        norm_mix0=norm_mix[0].reshape(1, d), norm_mix1=norm_mix[1].reshape(1, d),
        norm_ffn0=norm_ffn[0].reshape(1, d), norm_ffn1=norm_ffn[1].reshape(1, d),
        norm_out=norm_out.reshape(1, d),
        gdn_wqkv=win[:, :GDN_CONV_CH].astype(BF16),
        gdn_wz=win[:, GDN_CONV_CH:GDN_CONV_CH + GDN_V_W].astype(BF16),
        gdn_wab=wab.astype(BF16),
        gdn_conv_w=gdn_conv_w[0],
        gdn_alog=_pad_lanes(gdn_a_log[0]), gdn_dtb=_pad_lanes(gdn_dt_bias[0]),
        gdn_norm=gdn_norm[0].reshape(1, GDN_DV), gdn_wout=gdn_w_out[0].astype(BF16),
        lru_win=lru_w_in[0].astype(BF16), lru_conv_w=lru_conv_w[0], lru_conv_b=lru_conv_b[0].reshape(1, d),
        lru_wax=jnp.concatenate([lru_w_a[0], lru_w_x[0]], axis=-1).astype(BF16),
        lru_ba=lru_b_a[0].reshape(1, d), lru_bx=lru_b_x[0].reshape(1, d), lru_lam=lru_lambda[0].reshape(1, d),
        lru_wout=lru_w_out[0].astype(BF16),
        moe_w1=moe_w1, moe_w3=moe_w3, moe_w2=moe_w2,
    )
    for i in range(DEPTH):
        wp[f'wr{i}'], wp[f'br{i}'] = _router_weights(moe_w_rg[i], moe_b_rg[i], moe_w_re[i], moe_b_re[i])

    mods = _ada_mod(jnp.concatenate([c_prompt, c_sample], axis=0), w_ada, b_ada)
    mods = mods.reshape(DEPTH, bp + bs, 6, d)
    mods_p = [mods[i, :bp] for i in range(DEPTH)]
    mods_s = [mods[i, bp:] for i in range(DEPTH)]

    dt = x_prompt.dtype
    z_s = jnp.zeros((bp, GDN_HEADS, GDN_DK, GDN_DV), dt)
    z_gc = jnp.zeros((bp, CONV_W - 1, GDN_CONV_CH), dt)
    z_h = jnp.zeros((bp, 1, LRU_WIDTH), dt)
    z_lc = jnp.zeros((bp, CONV_W - 1, LRU_WIDTH), dt)
    y_p, gs_p, gc_p, lh_p, lc_p = _trunk(x_prompt, mods_p, z_s, z_gc, z_h, z_lc, wp)
    y_s, gs_s, gc_s, lh_s, lc_s = _trunk(x_sample, mods_s, state_gdn_S[0], state_gdn_conv[0],
                                         state_lru_h[0].reshape(bs, 1, LRU_WIDTH), state_lru_conv[0], wp)
    return (y_p, y_s, gs_p, gc_p, lh_p, lc_p, gs_s, gc_s, lh_s, lc_s)
```

```python
import functools
import math

import jax
import jax.numpy as jnp
from jax import lax
from jax.experimental import pallas as pl
from jax.experimental.pallas import tpu as pltpu

F32 = jnp.float32
BF16 = jnp.bfloat16

D_MODEL = 1024
DEPTH = 2
CONV_W = 4
NORM_EPS = 1e-6
GDN_HEADS = 8
GDN_DK = 128
GDN_DV = 128
GDN_QK_W = GDN_HEADS * GDN_DK
GDN_V_W = GDN_HEADS * GDN_DV
GDN_CONV_CH = 2 * GDN_QK_W + GDN_V_W
GDN_CHUNK = 64
GDN_INV_BLOCK = 16
GDN_CHUNK_UNROLL = 4
LRU_WIDTH = D_MODEL
LRU_BLOCKS = 8
LRU_BLOCK_W = LRU_WIDTH // LRU_BLOCKS
LRU_C = 8.0
N_GROUPS = 4
EXPERTS_PER_GROUP = 8
N_EXPERTS = N_GROUPS * EXPERTS_PER_GROUP
EXPERT_TOPK = 2
D_FF_EXPERT = 512
MOE_BLOCK = 256
MOE_BUFFERS = 3
LANES = 128
SUBLANES = 8
ROW_TILE = 256
VMEM_LIMIT = 56 * 1024 * 1024


def _dot(a, b):
    return jnp.dot(a, b, preferred_element_type=F32)


def _dot_nt(a, b):
    return lax.dot_general(a, b, (((1,), (1,)), ((), ())), preferred_element_type=F32)


def _dot_tn(a, b):
    return lax.dot_general(a, b, (((0,), (0,)), ((), ())), preferred_element_type=F32)


def _split3(x):
    a = x.astype(BF16)
    r = x - a.astype(F32)
    b = r.astype(BF16)
    c = (r - b.astype(F32)).astype(BF16)
    return a, b, c


def _rms(x):
    return x * lax.rsqrt(jnp.mean(x * x, axis=-1, keepdims=True) + NORM_EPS)


def _sigmoid(x):
    return 1.0 / (1.0 + jnp.exp(-x))


def _silu(x):
    return x * _sigmoid(x)


def _softplus(x):
    return jnp.maximum(x, 0.0) + jnp.log1p(jnp.exp(-jnp.abs(x)))


def _expm1(x):
    u = jnp.exp(x)
    near = (u > 0.5) & (u < 2.0) & (u != 1.0)
    corrected = (u - 1.0) * x / jnp.where(near, jnp.log(u), 1.0)
    return jnp.where(u == 1.0, x, jnp.where(near, corrected, u - 1.0))


def _params(*sem):
    return pltpu.CompilerParams(dimension_semantics=sem, vmem_limit_bytes=VMEM_LIMIT)


def _const_spec(shape):
    nd = len(shape)
    return pl.BlockSpec(shape, lambda *_: (0,) * nd)


def _ada_kernel(c_ref, w_ref, b_ref, o_ref):
    s = _silu(c_ref[...]).astype(BF16)
    o_ref[0] = _dot(s, w_ref[0].astype(BF16)) + b_ref[0]


def _ada_mod(c_all, w_ada, b_ada):
    bt = c_all.shape[0]
    tn = 1536
    n6 = 6 * D_MODEL
    return pl.pallas_call(
        _ada_kernel,
        grid=(DEPTH, n6 // tn),
        in_specs=[
            pl.BlockSpec((bt, D_MODEL), lambda i, j: (0, 0)),
            pl.BlockSpec((1, D_MODEL, tn), lambda i, j: (i, 0, j)),
            pl.BlockSpec((1, 1, tn), lambda i, j: (i, 0, j)),
        ],
        out_specs=pl.BlockSpec((1, bt, tn), lambda i, j: (i, 0, j)),
        out_shape=jax.ShapeDtypeStruct((DEPTH, bt, n6), F32),
        compiler_params=_params("arbitrary", "arbitrary"),
        name="ada_mod",
    )(c_all, w_ada, b_ada.reshape(DEPTH, 1, n6))


def _route(hn2, wr, br, cnt_ref):
    logits = _dot(hn2.astype(BF16), wr) + br
    lane = lax.broadcasted_iota(jnp.int32, logits.shape, 1)
    neg = jnp.float32(-jnp.inf)
    big = jnp.int32(1 << 20)
    is_g = lane < N_GROUPS
    lg = jnp.where(is_g, logits, neg)
    eg = jnp.exp(lg - jnp.max(lg, axis=-1, keepdims=True))
    pg = eg / jnp.sum(eg, axis=-1, keepdims=True)
    pg = jnp.where(is_g, pg, -1.0)
    gate_g = jnp.max(pg, axis=-1, keepdims=True)
    grp = jnp.min(jnp.where(pg == gate_g, lane, big), axis=-1, keepdims=True)
    lo = N_GROUPS + grp * EXPERTS_PER_GROUP
    is_e = (lane >= lo) & (lane < lo + EXPERTS_PER_GROUP)
    le = jnp.where(is_e, logits, neg)
    ee = jnp.exp(le - jnp.max(le, axis=-1, keepdims=True))
    pe = ee / jnp.sum(ee, axis=-1, keepdims=True)
    pe = jnp.where(is_e, pe, -1.0)
    p1 = jnp.max(pe, axis=-1, keepdims=True)
    i1 = jnp.min(jnp.where(pe == p1, lane, big), axis=-1, keepdims=True)
    pe2 = jnp.where(lane == i1, -1.0, pe)
    p2 = jnp.max(pe2, axis=-1, keepdims=True)
    i2 = jnp.min(jnp.where(pe2 == p2, lane, big), axis=-1, keepdims=True)
    tot = p1 + p2
    w1 = gate_g * (p1 / tot)
    w2 = gate_g * (p2 / tot)
    e1 = i1 - N_GROUPS
    e2 = i2 - N_GROUPS
    tm = logits.shape[0]
    oh1 = lane == e1
    oh2 = lane == e2
    cnt = jnp.where(oh1 | oh2, 1.0, 0.0)
    ri = lax.broadcasted_iota(jnp.int32, (tm, tm), 0)
    ci = lax.broadcasted_iota(jnp.int32, (tm, tm), 1)
    before = jnp.where(ci < ri, 1.0, 0.0).astype(BF16)
    pos = _dot(before, cnt.astype(BF16)) + cnt_ref[...]
    r1 = jnp.sum(jnp.where(oh1, pos, 0.0), axis=-1, keepdims=True)
    r2 = jnp.sum(jnp.where(oh2, pos, 0.0), axis=-1, keepdims=True)
    cnt_ref[...] = cnt_ref[...] + jnp.sum(cnt, axis=0, keepdims=True)
    vals = (e1.astype(F32), e2.astype(F32), w1, w2, r1, r2)
    out = jnp.zeros(logits.shape, F32)
    for j, val in enumerate(vals):
        out = jnp.where(lane == j, val, out)
    return out


def _tok_spec(tm):
    return pl.BlockSpec((1, tm, D_MODEL // LANES, LANES), lambda i, j, *_: (i, j, 0, 0))


def _store_token_major(ref, lead, x):
    for c in range(D_MODEL // LANES):
        ref[lead + (slice(None), c, slice(None))] = x[:, c * LANES:(c + 1) * LANES]


def _load_token_major(ref, lead, rows=slice(None)):
    return jnp.concatenate([ref[lead + (rows, c, slice(None))] for c in range(D_MODEL // LANES)], axis=1)


N_DMA_QUEUES = 2


class _PairGather:
    def __init__(self, dest_ref, ys_hbm, buf_ref, sem_ref, tm, n_pieces):
        self.dest_ref, self.ys_hbm, self.buf_ref, self.sem_ref, self.tm = dest_ref, ys_hbm, buf_ref, sem_ref, tm
        n_steps = pl.num_programs(0) * pl.num_programs(1)
        self.n = n_steps * tm
        self.step = pl.program_id(0) * pl.num_programs(1) + pl.program_id(1)
        self.last = n_steps - 1
        self.slot = self.step % 2
        self.nxt = jnp.minimum(self.step + 1, self.last)
        self.per = -(-2 * tm // n_pieces)
        self.issued = 0

    def _copy(self, tile, j, buf):
        k, r = divmod(j, self.tm)
        row = self.dest_ref[k * self.n + tile * self.tm + r]
        return pltpu.make_async_copy(self.ys_hbm.at[row], self.buf_ref.at[buf, j], self.sem_ref.at[buf])

    def _wait(self, buf):
        pltpu.make_async_copy(self.ys_hbm.at[pl.ds(0, 2 * self.tm)], self.buf_ref.at[buf], self.sem_ref.at[buf]).wait()

    def start(self):
        @pl.when(self.step == 0)
        def _():
            for j in range(2 * self.tm):
                self._copy(0, j, 0).start(priority=j % N_DMA_QUEUES)

        self._wait(self.slot)
        ya = _load_token_major(self.buf_ref, (self.slot,), slice(0, self.tm))
        yb = _load_token_major(self.buf_ref, (self.slot,), slice(self.tm, 2 * self.tm))
        return ya, yb

    def issue_piece(self):
        hi = min(self.issued + self.per, 2 * self.tm)
        for j in range(self.issued, hi):
            self._copy(self.nxt, j, 1 - self.slot).start(priority=j % N_DMA_QUEUES)
        self.issued = hi

    def finish(self):
        assert self.issued == 2 * self.tm

        @pl.when(self.step == self.last)
        def _():
            self._wait(1 - self.slot)


def _init_counts(cnt_ref):
    @pl.when((pl.program_id(0) == 0) & (pl.program_id(1) == 0))
    def _():
        cnt_ref[...] = jnp.zeros(cnt_ref.shape, F32)


def _post_mixer(x, y, mod, nf, wr, br, cnt_ref):
    g1 = mod[2:3]
    sh2 = mod[3:4]
    sc2 = mod[4:5]
    x1 = x + g1 * y
    hn2 = (_rms(x1) * nf) * (1.0 + sc2) + sh2
    return x1, hn2, _route(hn2, wr, br, cnt_ref)


def _gdn_in_kernel(x_ref, mod_ref, ng_ref, wqkv_ref, wz_ref, wab_ref, cw_ref, hist0_ref, alog_ref, dtb_ref,
                   q_ref, k_ref, v_ref, z_ref, gbc_ref, gbr_ref, hist_ref, xp_ref, *, tm, chunk):
    l = pl.program_id(1)

    @pl.when(l == 0)
    def _():
        xp_ref[0:8, :] = jnp.zeros((8, GDN_CONV_CH), F32)
        xp_ref[5:8, :] = hist0_ref[0]

    x = x_ref[0]
    mod = mod_ref[0]
    hn = (_rms(x) * ng_ref[...]) * (1.0 + mod[1:2]) + mod[0:1]
    hb = hn.astype(BF16)
    qkv = _dot(hb, wqkv_ref[...])
    xp_ref[8:8 + tm, :] = qkv
    cw = cw_ref[...]
    y = cw[3:4] * qkv
    for j in range(CONV_W - 1):
        y = y + cw[j:j + 1] * xp_ref[5 + j:5 + j + tm, :]
    new_hist = xp_ref[tm + 5:tm + 8, :]
    xp_ref[5:8, :] = new_hist
    hist_ref[0] = new_hist
    y = _silu(y)
    for h in range(GDN_HEADS):
        s = slice(h * GDN_DK, (h + 1) * GDN_DK)
        qh = y[:, s]
        q_ref[0, :, s] = qh * lax.rsqrt(jnp.sum(qh * qh, axis=-1, keepdims=True) + 1e-6) * (GDN_DK ** -0.5)
        kh = y[:, GDN_QK_W + h * GDN_DK:GDN_QK_W + (h + 1) * GDN_DK]
        k_ref[0, :, s] = kh * lax.rsqrt(jnp.sum(kh * kh, axis=-1, keepdims=True) + 1e-6)
    v_ref[0] = y[:, 2 * GDN_QK_W:]
    z_ref[0] = _dot(hb, wz_ref[...])
    ab = _dot(hb, wab_ref[...])
    g = -jnp.exp(alog_ref[...]) * _softplus(ab + dtb_ref[...])
    beta = _sigmoid(ab)
    ri = lax.broadcasted_iota(jnp.int32, (tm, tm), 0)
    ci = lax.broadcasted_iota(jnp.int32, (tm, tm), 1)
    tri = jnp.where((ri // chunk == ci // chunk) & (ci <= ri), 1.0, 0.0).astype(BF16)
    g1, g2, g3 = _split3(g)
    gcum = (_dot(tri, g1) + _dot(tri, g2)) + _dot(tri, g3)
    lane = lax.broadcasted_iota(jnp.int32, (tm, LANES), 1)
    gb = jnp.where(lane < GDN_HEADS, gcum, beta)
    gbc_ref[0] = gb
    er = lax.broadcasted_iota(jnp.int32, (2 * GDN_HEADS, LANES), 0)
    ec = lax.broadcasted_iota(jnp.int32, (2 * GDN_HEADS, LANES), 1)
    sel = jnp.where(er == ec, 1.0, 0.0).astype(BF16)
    b1, b2, b3 = _split3(gb)
    for n in range(tm // chunk):
        r = slice(n * chunk, (n + 1) * chunk)
        gbr_ref[0, n] = (_dot_nt(sel, b1[r]) + _dot_nt(sel, b2[r])) + _dot_nt(sel, b3[r])


def _gdn_in(x, mod, ng, wqkv, wz, wab, cw, hist0, alog, dtb, tm, chunk):
    b, l, d = x.shape
    grid = (b, l // tm)
    row = lambda shape: pl.BlockSpec(shape, lambda i, j: (i, j, 0))
    outs = pl.pallas_call(
        functools.partial(_gdn_in_kernel, tm=tm, chunk=chunk),
        grid=grid,
        in_specs=[
            row((1, tm, d)),
            pl.BlockSpec((1, 6, d), lambda i, j: (i, 0, 0)),
            _const_spec((1, d)),
            _const_spec(wqkv.shape),
            _const_spec(wz.shape),
            _const_spec(wab.shape),
            _const_spec(cw.shape),
            pl.BlockSpec((1, CONV_W - 1, GDN_CONV_CH), lambda i, j: (i, 0, 0)),
            _const_spec((1, LANES)),
            _const_spec((1, LANES)),
        ],
        out_specs=[
            row((1, tm, GDN_QK_W)), row((1, tm, GDN_QK_W)), row((1, tm, GDN_V_W)), row((1, tm, GDN_V_W)),
            row((1, tm, LANES)),
            pl.BlockSpec((1, tm // chunk, 2 * GDN_HEADS, chunk), lambda i, j: (i, j, 0, 0)),
            pl.BlockSpec((1, CONV_W - 1, GDN_CONV_CH), lambda i, j: (i, 0, 0)),
        ],
        out_shape=[
            jax.ShapeDtypeStruct((b, l, GDN_QK_W), F32), jax.ShapeDtypeStruct((b, l, GDN_QK_W), F32),
            jax.ShapeDtypeStruct((b, l, GDN_V_W), F32), jax.ShapeDtypeStruct((b, l, GDN_V_W), F32),
            jax.ShapeDtypeStruct((b, l, LANES), F32),
            jax.ShapeDtypeStruct((b, l // chunk, 2 * GDN_HEADS, chunk), F32),
            jax.ShapeDtypeStruct((b, CONV_W - 1, GDN_CONV_CH), F32),
        ],
        scratch_shapes=[pltpu.VMEM((tm + 8, GDN_CONV_CH), F32)],
        compiler_params=_params("arbitrary", "arbitrary"),
        name="gdn_in",
    )(x, mod, ng, wqkv, wz, wab, cw, hist0, alog, dtb)
    return outs


def _gdn_chunk_kernel(q_ref, k_ref, v_ref, gbc_ref, gbr_ref, s0_ref, o_ref, sout_ref, s_ref, *, tm, chunk):
    l = pl.program_id(1)

    @pl.when(l == 0)
    def _():
        s_ref[...] = s0_ref[0]

    c2 = 2 * chunk
    n_pairs = GDN_HEADS // 2
    ri = lax.broadcasted_iota(jnp.int32, (chunk, c2), 0)
    cl = lax.broadcasted_iota(jnp.int32, (chunk, c2), 1)
    ci = cl % chunk
    left = cl < chunk
    incl = ri >= ci
    strict = ri > ci
    eye = jnp.where(ri == ci, 1.0, 0.0).astype(F32)
    base = min(GDN_INV_BLOCK, chunk)
    levels = int(math.log2(base))
    diag_blk = (ri // base) == (ci // base)
    merge_masks = []
    blk = base
    while blk < chunk:
        merge_masks.append(((ri // (2 * blk)) == (ci // (2 * blk))) & ((ri // blk) % 2 == 1) & ((ci // blk) % 2 == 0))
        blk *= 2
    heads = range(GDN_HEADS)
    pairs = range(n_pairs)

    def bdiag(x):
        return jnp.concatenate([jnp.where(left, x, 0.0), jnp.where(left, 0.0, x)], axis=0).astype(BF16)

    def bdiag2(x0, x1):
        z = jnp.zeros_like(x0)
        return jnp.concatenate([jnp.concatenate([x0, z], axis=1), jnp.concatenate([z, x1], axis=1)], axis=0).astype(BF16)

    unroll = min(GDN_CHUNK_UNROLL, tm // chunk)
    assert (tm // chunk) % unroll == 0
    vheads = range(unroll * GDN_HEADS)
    vpairs = range(unroll * n_pairs)
    hs = [slice(h * GDN_DK, (h + 1) * GDN_DK) for h in heads]

    def chunk_group(g, carry):
        rows = [pl.ds(pl.multiple_of((g * unroll + c) * chunk, chunk), chunk) for c in range(unroll)]
        gbc = [gbc_ref[0, rows[c], :] for c in range(unroll)]
        gbr = [gbr_ref[0, g * unroll + c] for c in range(unroll)]
        q = [q_ref[0, rows[vh // GDN_HEADS], hs[vh % GDN_HEADS]] for vh in vheads]
        k = [k_ref[0, rows[vh // GDN_HEADS], hs[vh % GDN_HEADS]] for vh in vheads]
        v = [v_ref[0, rows[vh // GDN_HEADS], hs[vh % GDN_HEADS]] for vh in vheads]
        gc = [gbc[vh // GDN_HEADS][:, vh % GDN_HEADS:vh % GDN_HEADS + 1] for vh in vheads]
        bc = [gbc[vh // GDN_HEADS][:, GDN_HEADS + vh % GDN_HEADS:GDN_HEADS + vh % GDN_HEADS + 1] for vh in vheads]
        kb = [k[vh] * bc[vh] for vh in vheads]
        a2 = [_dot_nt(jnp.concatenate([q[vh], kb[vh]], axis=0).astype(BF16), k[vh].astype(BF16)) for vh in vheads]
        attn, m = [], []
        for vp in vpairs:
            v0, v1 = 2 * vp, 2 * vp + 1
            h0, h1 = v0 % GDN_HEADS, v1 % GDN_HEADS
            gr = gbr[vp // n_pairs]
            gcp = jnp.where(left, gc[v0], gc[v1])
            grp = jnp.concatenate([gr[h0:h0 + 1, :], gr[h1:h1 + 1, :]], axis=1)
            decay = jnp.where(incl, jnp.exp(jnp.where(incl, gcp - grp, 0.0)), 0.0)
            ap = jnp.concatenate([a2[v0], a2[v1]], axis=1)
            attn.append(ap[:chunk] * decay)
            m.append(jnp.where(strict, ap[chunk:] * decay, 0.0))
        nq = [jnp.where(diag_blk, -m[vp], 0.0) for vp in vpairs]
        p = [eye + nq[vp] for vp in vpairs]
        nq = [_dot(nq[vp].astype(BF16), bdiag(nq[vp])) for vp in vpairs]
        for j in range(1, levels):
            if j < levels - 1:
                r = [_dot(nq[vp].astype(BF16), jnp.concatenate([bdiag(nq[vp]), bdiag(p[vp])], axis=1)) for vp in vpairs]
                nq = [r[vp][:, :c2] for vp in vpairs]
                p = [p[vp] + r[vp][:, c2:] for vp in vpairs]
            else:
                p = [p[vp] + _dot(nq[vp].astype(BF16), bdiag(p[vp])) for vp in vpairs]
        for low in merge_masks:
            t = [_dot(jnp.where(low, m[vp], 0.0).astype(BF16), bdiag(p[vp])) for vp in vpairs]
            p = [p[vp] - _dot(p[vp].astype(BF16), bdiag(t[vp])) for vp in vpairs]
        eg = [jnp.exp(gc[vh]) for vh in vheads]
        rhs = [jnp.concatenate([kb[vh] * eg[vh], v[vh] * bc[vh]], axis=1) for vh in vheads]
        wu = [_dot(p[vp].astype(BF16), bdiag2(rhs[2 * vp], rhs[2 * vp + 1])) for vp in vpairs]
        wd = GDN_DK + GDN_DV
        w = [wu[vh // 2][:, (vh % 2) * wd:(vh % 2) * wd + GDN_DK] for vh in vheads]
        u = [wu[vh // 2][:, (vh % 2) * wd + GDN_DK:(vh % 2 + 1) * wd] for vh in vheads]
        lhs = [jnp.concatenate([w[vh], q[vh] * eg[vh]], axis=0).astype(BF16) for vh in vheads]
        for c in range(unroll):
            base = c * GDN_HEADS
            st = [s_ref[h] for h in heads]
            ws = [_dot(lhs[base + h], st[h].astype(BF16)) for h in heads]
            v_new = [u[base + h] - ws[h][:chunk] for h in heads]
            av = [_dot(attn[c * n_pairs + hp].astype(BF16), bdiag2(v_new[2 * hp], v_new[2 * hp + 1])) for hp in pairs]
            for h in heads:
                o_ref[0, rows[c], hs[h]] = ws[h][chunk:] + av[h // 2][:, (h % 2) * GDN_DV:(h % 2 + 1) * GDN_DV]
            for h in heads:
                g_last = gc[base + h][chunk - 1:chunk, :]
                kd = k[base + h] * jnp.exp(g_last - gc[base + h])
                s_ref[h] = st[h] * jnp.exp(g_last) + _dot_tn(kd.astype(BF16), v_new[h].astype(BF16))
        return carry

    lax.fori_loop(0, tm // (chunk * unroll), chunk_group, 0)
    sout_ref[0] = s_ref[...]


def _gdn_chunks(q, k, v, gbc, gbr, s0, tm, chunk):
    b, l, _ = q.shape
    row = lambda shape: pl.BlockSpec(shape, lambda i, j: (i, j, 0))
    sspec = pl.BlockSpec((1, GDN_HEADS, GDN_DK, GDN_DV), lambda i, j: (i, 0, 0, 0))
    return pl.pallas_call(
        functools.partial(_gdn_chunk_kernel, tm=tm, chunk=chunk),
        grid=(b, l // tm),
        in_specs=[
            row((1, tm, GDN_QK_W)), row((1, tm, GDN_QK_W)), row((1, tm, GDN_V_W)), row((1, tm, LANES)),
            pl.BlockSpec((1, tm // chunk, 2 * GDN_HEADS, chunk), lambda i, j: (i, j, 0, 0)),
            sspec,
        ],
        out_specs=[row((1, tm, GDN_V_W)), sspec],
        out_shape=[jax.ShapeDtypeStruct((b, l, GDN_V_W), F32),
                   jax.ShapeDtypeStruct((b, GDN_HEADS, GDN_DK, GDN_DV), F32)],
        scratch_shapes=[pltpu.VMEM((GDN_HEADS, GDN_DK, GDN_DV), F32)],
        compiler_params=_params("arbitrary", "arbitrary"),
        name="gdn_chunks",
    )(q, k, v, gbc, gbr, s0)


def _gdn_out_kernel(o_ref, z_ref, x_ref, mod_ref, gn_ref, wout_ref, nf_ref, wr_ref, br_ref,
                    x1_ref, hn2_ref, route_ref, cnt_ref):
    _init_counts(cnt_ref)
    o = o_ref[0]
    z = z_ref[0]
    gn = gn_ref[...]
    parts = []
    for h in range(GDN_HEADS):
        s = slice(h * GDN_DV, (h + 1) * GDN_DV)
        parts.append((_rms(o[:, s]) * gn) * _silu(z[:, s]))
    on = jnp.concatenate(parts, axis=1).astype(BF16)
    y = _dot(on, wout_ref[...])
    x1, hn2, route = _post_mixer(x_ref[0], y, mod_ref[0], nf_ref[...], wr_ref[...], br_ref[...], cnt_ref)
    x1_ref[0] = x1
    _store_token_major(hn2_ref, (0,), hn2)
    route_ref[0] = route


def _gdn_out(o, z, x, mod, gn, wout, nf, wr, br, tm):
    b, l, d = x.shape
    row = lambda shape: pl.BlockSpec(shape, lambda i, j: (i, j, 0))
    return pl.pallas_call(
        _gdn_out_kernel,
        grid=(b, l // tm),
        in_specs=[
            row((1, tm, GDN_V_W)), row((1, tm, GDN_V_W)), row((1, tm, d)),
            pl.BlockSpec((1, 6, d), lambda i, j: (i, 0, 0)),
            _const_spec((1, GDN_DV)), _const_spec(wout.shape), _const_spec((1, d)),
            _const_spec(wr.shape), _const_spec((1, LANES)),
        ],
        out_specs=[row((1, tm, d)), _tok_spec(tm), row((1, tm, LANES)), _const_spec((1, LANES))],
        out_shape=[jax.ShapeDtypeStruct((b, l, d), F32), jax.ShapeDtypeStruct((b, l, d // LANES, LANES), F32),
                   jax.ShapeDtypeStruct((b, l, LANES), F32), jax.ShapeDtypeStruct((1, LANES), F32)],
        compiler_params=_params("arbitrary", "arbitrary"),
        name="gdn_out",
    )(o, z, x, mod, gn, wout, nf, wr, br)


def _gelu_tanh(x):
    return 0.5 * x * (1.0 + jnp.tanh(math.sqrt(2.0 / math.pi) * (x + 0.044715 * (x * x * x))))


def _lru_kernel(dest_ref, xa_ref, ys_hbm, rprev_ref, mprev_ref,
                mod_ref, ng_ref, win_ref, cw_ref, cb_ref, wax_ref, ba_ref, bx_ref, lam_ref, wout_ref,
                hist0_ref, h0_ref, nf_ref, wr_ref, br_ref,
                x1_ref, hn2_ref, route_ref, hist_ref, hlast_ref, cnt_ref, xp_ref, h_ref, ybuf_ref, sem_ref, *, tm):
    l = pl.program_id(1)
    _init_counts(cnt_ref)
    gather = _PairGather(dest_ref, ys_hbm, ybuf_ref, sem_ref, tm, n_pieces=8)

    @pl.when(l == 0)
    def _():
        xp_ref[0:8, :] = jnp.zeros((8, LRU_WIDTH), F32)
        xp_ref[5:8, :] = hist0_ref[0]
        h_ref[...] = h0_ref[0]

    ya, yb = gather.start()
    x = _moe_residual(xa_ref[0], ya, yb, rprev_ref[0], mprev_ref[0])
    gather.issue_piece()
    mod = mod_ref[0]
    hn = (_rms(x) * ng_ref[...]) * (1.0 + mod[1:2]) + mod[0:1]
    gather.issue_piece()
    proj = _dot(hn.astype(BF16), win_ref[...])
    gather.issue_piece()
    gate_br = _gelu_tanh(proj[:, :LRU_WIDTH])
    xb = proj[:, LRU_WIDTH:]
    xp_ref[8:8 + tm, :] = xb
    cw = cw_ref[...]
    xc = cw[3:4] * xb
    for j in range(CONV_W - 1):
        xc = xc + cw[j:j + 1] * xp_ref[5 + j:5 + j + tm, :]
    xc = xc + cb_ref[...]
    new_hist = xp_ref[tm + 5:tm + 8, :]
    xp_ref[5:8, :] = new_hist
    hist_ref[0] = new_hist
    gather.issue_piece()
    xcb = xc.astype(BF16)
    ra, ia = [], []
    for h in range(LRU_BLOCKS):
        s = slice(h * LRU_BLOCK_W, (h + 1) * LRU_BLOCK_W)
        r2 = _dot(xcb[:, s], wax_ref[h])
        ra.append(r2[:, :LRU_BLOCK_W])
        ia.append(r2[:, LRU_BLOCK_W:])
    r = _sigmoid(jnp.concatenate(ra, axis=1) + ba_ref[...])
    i = _sigmoid(jnp.concatenate(ia, axis=1) + bx_ref[...])
    gather.issue_piece()
    log_a = (-LRU_C * r) * _softplus(-lam_ref[...])
    a = jnp.exp(log_a)
    mult = jnp.sqrt(-_expm1(2.0 * log_a))
    b = mult * (i * xc)
    sub = lax.broadcasted_iota(jnp.int32, (tm, LRU_WIDTH), 0) % SUBLANES
    sft = 1
    while sft < SUBLANES:
        keep = sub >= sft
        a_prev = jnp.where(keep, pltpu.roll(a, sft, 0), 1.0)
        b_prev = jnp.where(keep, pltpu.roll(b, sft, 0), 0.0)
        b = a * b_prev + b
        a = a * a_prev
        sft *= 2
    gather.issue_piece()
    h = h_ref[...]
    groups = []
    for g in range(tm // SUBLANES):
        rows = slice(g * SUBLANES, (g + 1) * SUBLANES)
        hg = b[rows] + a[rows] * h
        groups.append(hg)
        h = hg[SUBLANES - 1:SUBLANES, :]
    hs = jnp.concatenate(groups, axis=0)
    h_last = h
    h_ref[...] = h_last
    hlast_ref[0] = h_last
    gather.issue_piece()
    y = _dot((hs * gate_br).astype(BF16), wout_ref[...])
    gather.issue_piece()
    x1, hn2, route = _post_mixer(x, y, mod, nf_ref[...], wr_ref[...], br_ref[...], cnt_ref)
    x1_ref[0] = x1
    _store_token_major(hn2_ref, (0,), hn2)
    route_ref[0] = route
    gather.finish()


def _lru_layer(dest, xa, ys, rprev, mprev, mod, ng, win, cw, cb, wax, ba, bx, lam, wout, hist0, h0, nf, wr, br, tm):
    b, l, d = xa.shape
    row = lambda shape: pl.BlockSpec(shape, lambda i, j, *_: (i, j, 0))
    per_b = lambda shape: pl.BlockSpec(shape, lambda i, j, *_: (i, 0, 0))
    vec = _const_spec((1, d))
    grid_spec = pltpu.PrefetchScalarGridSpec(
        num_scalar_prefetch=1,
        grid=(b, l // tm),
        in_specs=[
            row((1, tm, d)), pl.BlockSpec(memory_space=pl.ANY), row((1, tm, LANES)), per_b((1, 6, d)),
            per_b((1, 6, d)), vec, _const_spec(win.shape), _const_spec(cw.shape), vec,
            _const_spec(wax.shape), vec, vec, vec, _const_spec(wout.shape),
            per_b((1, CONV_W - 1, LRU_WIDTH)), per_b((1, 1, LRU_WIDTH)), vec,
            _const_spec(wr.shape), _const_spec((1, LANES)),
        ],
        out_specs=[row((1, tm, d)), _tok_spec(tm), row((1, tm, LANES)),
                   per_b((1, CONV_W - 1, LRU_WIDTH)), per_b((1, 1, LRU_WIDTH)), _const_spec((1, LANES))],
        scratch_shapes=[pltpu.VMEM((tm + 8, LRU_WIDTH), F32), pltpu.VMEM((1, LRU_WIDTH), F32),
                        pltpu.VMEM((2, 2 * tm, d // LANES, LANES), F32), pltpu.SemaphoreType.DMA((2,))],
    )
    return pl.pallas_call(
        functools.partial(_lru_kernel, tm=tm),
        grid_spec=grid_spec,
        out_shape=[jax.ShapeDtypeStruct((b, l, d), F32), jax.ShapeDtypeStruct((b, l, d // LANES, LANES), F32),
                   jax.ShapeDtypeStruct((b, l, LANES), F32),
                   jax.ShapeDtypeStruct((b, CONV_W - 1, LRU_WIDTH), F32),
                   jax.ShapeDtypeStruct((b, 1, LRU_WIDTH), F32), jax.ShapeDtypeStruct((1, LANES), F32)],
        compiler_params=_params("arbitrary", "arbitrary"),
        name="lru_layer",
    )(dest, xa, ys, rprev, mprev, mod, ng, win, cw, cb, wax, ba, bx, lam, wout, hist0, h0, nf, wr, br)


def _moe_kernel(be_ref, st_ref, x_hbm, w1_ref, w3_ref, w2_ref, y_ref, w1b_ref, w3b_ref, w2b_ref, xbuf_ref, sem_ref):
    i = pl.program_id(0)
    last = pl.num_programs(0) - 1
    slot = i % MOE_BUFFERS

    def row_copy(blk, r, buf):
        tok = st_ref[blk * MOE_BLOCK + r]
        return pltpu.make_async_copy(x_hbm.at[tok], xbuf_ref.at[buf, r], sem_ref.at[buf])

    def wait_block(buf):
        pltpu.make_async_copy(x_hbm.at[pl.ds(0, MOE_BLOCK)], xbuf_ref.at[buf], sem_ref.at[buf]).wait()

    @pl.when(i == 0)
    def _():
        for ahead in range(MOE_BUFFERS - 1):
            for r in range(MOE_BLOCK):
                row_copy(jnp.minimum(ahead, last), r, ahead).start(priority=r % N_DMA_QUEUES)

    prev = be_ref[jnp.maximum(i - 1, 0)]

    @pl.when((i == 0) | (be_ref[i] != prev))
    def _():
        w1b_ref[...] = w1_ref[0, 0].astype(BF16)
        w3b_ref[...] = w3_ref[0, 0].astype(BF16)
        w2b_ref[...] = w2_ref[0, 0].astype(BF16)

    nxt = jnp.minimum(i + MOE_BUFFERS - 1, last)

    def body(buf):
        wait_block(buf)
        n_piece = 6
        per = MOE_BLOCK // n_piece + 1
        issued = [0]

        def issue_some():
            hi = min(issued[0] + per, MOE_BLOCK)
            for r in range(issued[0], hi):
                row_copy(nxt, r, (buf + MOE_BUFFERS - 1) % MOE_BUFFERS).start(priority=r % N_DMA_QUEUES)
            issued[0] = hi

        xb = _load_token_major(xbuf_ref, (buf,)).astype(BF16)
        half = D_FF_EXPERT // 2
        hid = []
        for c in range(2):
            cols = slice(c * half, (c + 1) * half)
            issue_some()
            h1 = _dot(xb, w1b_ref[:, cols])
            issue_some()
            h3 = _dot(xb, w3b_ref[:, cols])
            hid.append((_silu(h1) * h3).astype(BF16))
        issue_some()
        y = _dot(hid[0], w2b_ref[0:half, :])
        issue_some()
        _store_token_major(y_ref, (), y + _dot(hid[1], w2b_ref[half:, :]))
        assert issued[0] == MOE_BLOCK

        @pl.when(i == last)
        def _():
            for other in range(1, MOE_BUFFERS):
                wait_block((buf + other) % MOE_BUFFERS)

    for buf in range(MOE_BUFFERS):
        pl.when(slot == buf)(functools.partial(body, buf))


def _moe_blocks(blk_expert, slot_tok, x, w1, w3, w2, layer):
    d = D_MODEL
    p = slot_tok.shape[0]
    nb = p // MOE_BLOCK
    grid_spec = pltpu.PrefetchScalarGridSpec(
        num_scalar_prefetch=2,
        grid=(nb,),
        in_specs=[
            pl.BlockSpec(memory_space=pl.ANY),
            pl.BlockSpec((1, 1, d, D_FF_EXPERT), lambda i, be, st: (layer, be[i], 0, 0)),
            pl.BlockSpec((1, 1, d, D_FF_EXPERT), lambda i, be, st: (layer, be[i], 0, 0)),
            pl.BlockSpec((1, 1, D_FF_EXPERT, d), lambda i, be, st: (layer, be[i], 0, 0)),
        ],
        out_specs=pl.BlockSpec((MOE_BLOCK, d // LANES, LANES), lambda i, be, st: (i, 0, 0)),
        scratch_shapes=[pltpu.VMEM((d, D_FF_EXPERT), BF16), pltpu.VMEM((d, D_FF_EXPERT), BF16),
                        pltpu.VMEM((D_FF_EXPERT, d), BF16), pltpu.VMEM((MOE_BUFFERS, MOE_BLOCK, d // LANES, LANES), F32),
                        pltpu.SemaphoreType.DMA((MOE_BUFFERS,))],
    )
    return pl.pallas_call(
        _moe_kernel,
        grid_spec=grid_spec,
        out_shape=jax.ShapeDtypeStruct((p, d // LANES, LANES), F32),
        compiler_params=_params("arbitrary"),
        name="moe_blocks",
    )(blk_expert, slot_tok, x, w1, w3, w2)


def _moe_residual(x1, ya, yb, route, mod):
    return x1 + mod[5:6] * (ya * route[:, 2:3] + yb * route[:, 3:4])


def _final_kernel(dest_ref, x1_ref, ys_hbm, route_ref, mod_ref, no_ref, o_ref, ybuf_ref, sem_ref, *, tm):
    gather = _PairGather(dest_ref, ys_hbm, ybuf_ref, sem_ref, tm, n_pieces=1)
    ya, yb = gather.start()
    gather.issue_piece()
    x2 = _moe_residual(x1_ref[0], ya, yb, route_ref[0], mod_ref[0])
    o_ref[0] = _rms(x2) * no_ref[...]
    gather.finish()


def _final(dest, x1, ys, route, mod, norm_out, tm):
    b, l, d = x1.shape
    row = lambda shape: pl.BlockSpec(shape, lambda i, j, *_: (i, j, 0))
    grid_spec = pltpu.PrefetchScalarGridSpec(
        num_scalar_prefetch=1,
        grid=(b, l // tm),
        in_specs=[row((1, tm, d)), pl.BlockSpec(memory_space=pl.ANY), row((1, tm, LANES)),
                  pl.BlockSpec((1, 6, d), lambda i, j, *_: (i, 0, 0)), _const_spec((1, d))],
        out_specs=row((1, tm, d)),
        scratch_shapes=[pltpu.VMEM((2, 2 * tm, d // LANES, LANES), F32), pltpu.SemaphoreType.DMA((2,))],
    )
    return pl.pallas_call(
        functools.partial(_final_kernel, tm=tm),
        grid_spec=grid_spec,
        out_shape=jax.ShapeDtypeStruct((b, l, d), F32),
        compiler_params=_params("arbitrary", "arbitrary"),
        name="final",
    )(dest, x1, ys, route, mod, norm_out)


def _moe_experts(hn2, route, counts, w1, w3, w2, layer):
    b, l = hn2.shape[:2]
    d = D_MODEL
    n = b * l
    a = n * EXPERT_TOPK
    rt = route.reshape(n, LANES)
    e_ab = rt[:, 0:EXPERT_TOPK].astype(jnp.int32)
    r_ab = rt[:, 4:4 + EXPERT_TOPK].astype(jnp.int32)
    counts = counts[0, :N_EXPERTS].astype(jnp.int32)
    padded = (counts + MOE_BLOCK - 1) // MOE_BLOCK * MOE_BLOCK
    pad_end = jnp.cumsum(padded)
    pad_start = pad_end - padded
    dest = pad_start[e_ab] + r_ab
    n_blocks = -(-a // MOE_BLOCK) + N_EXPERTS
    p = n_blocks * MOE_BLOCK
    tok = jnp.broadcast_to(jnp.arange(n, dtype=jnp.int32)[:, None], (n, EXPERT_TOPK))
    slot_tok = (jnp.arange(p, dtype=jnp.int32) % n).at[dest.reshape(-1)].set(
        tok.reshape(-1), unique_indices=True, indices_are_sorted=False, mode='promise_in_bounds')
    blk_first = jnp.arange(n_blocks, dtype=jnp.int32) * MOE_BLOCK
    blk_expert = jnp.minimum(jnp.sum((pad_end[None, :] <= blk_first[:, None]).astype(jnp.int32), axis=1),
                             N_EXPERTS - 1)
    ys = _moe_blocks(blk_expert, slot_tok, hn2.reshape(n, d // LANES, LANES), w1, w3, w2, layer)
    return ys, jnp.concatenate([dest[:, 0], dest[:, 1]])


def _trunk(x, mods, gdn_s, gdn_conv, lru_h, lru_conv, wp):
    b, l, d = x.shape
    tm = min(ROW_TILE, l)
    chunk = min(GDN_CHUNK, l)
    assert l % tm == 0 and tm % chunk == 0 and chunk & (chunk - 1) == 0
    mod = mods[0]
    q, k, v, z, gbc, gbr, gconv_new = _gdn_in(x, mod, wp['norm_mix0'], wp['gdn_wqkv'], wp['gdn_wz'], wp['gdn_wab'],
                                              wp['gdn_conv_w'], gdn_conv, wp['gdn_alog'], wp['gdn_dtb'], tm, chunk)
    o, s_new = _gdn_chunks(q, k, v, gbc, gbr, gdn_s, tm, chunk)
    x1, hn2, route, counts = _gdn_out(o, z, x, mod, wp['gdn_norm'], wp['gdn_wout'], wp['norm_ffn0'],
                                      wp['wr0'], wp['br0'], tm)
    ys, dest = _moe_experts(hn2, route, counts, wp['moe_w1'], wp['moe_w3'], wp['moe_w2'], 0)
    x1, hn2, route, lconv_new, h_new, counts = _lru_layer(
        dest, x1, ys, route, mod, mods[1], wp['norm_mix1'], wp['lru_win'], wp['lru_conv_w'], wp['lru_conv_b'],
        wp['lru_wax'], wp['lru_ba'], wp['lru_bx'], wp['lru_lam'], wp['lru_wout'], lru_conv, lru_h, wp['norm_ffn1'],
        wp['wr1'], wp['br1'], tm)
    ys, dest = _moe_experts(hn2, route, counts, wp['moe_w1'], wp['moe_w3'], wp['moe_w2'], 1)
    y = _final(dest, x1, ys, route, mods[1], wp['norm_out'], tm)
    return y, s_new[None], gconv_new[None], h_new.reshape(1, b, LRU_WIDTH), lconv_new[None]


def _pad_lanes(v, width=LANES):
    v = v.reshape(1, -1)
    return jnp.pad(v, ((0, 0), (0, width - v.shape[1])))


def _router_weights(w_rg, b_rg, w_re, b_re):
    w = jnp.pad(jnp.concatenate([w_rg, w_re], axis=1), ((0, 0), (0, LANES - N_GROUPS - N_EXPERTS)))
    return w.astype(BF16), _pad_lanes(jnp.concatenate([b_rg, b_re]))


def kernel(x_prompt, x_sample, state_gdn_S, state_gdn_conv, state_lru_h, state_lru_conv, c_prompt, c_sample, w_ada, b_ada, norm_mix, norm_ffn, norm_out, gdn_w_in, gdn_conv_w, gdn_a_log, gdn_dt_bias, gdn_norm, gdn_w_out, lru_w_in, lru_conv_w, lru_conv_b, lru_w_a, lru_b_a, lru_w_x, lru_b_x, lru_lambda, lru_w_out, moe_w_rg, moe_b_rg, moe_w_re, moe_b_re, moe_w1, moe_w3, moe_w2):
    d = D_MODEL
    bp = x_prompt.shape[0]
    bs = x_sample.shape[0]
    win = gdn_w_in[0]
    wab = jnp.pad(win[:, GDN_CONV_CH + GDN_V_W:], ((0, 0), (0, LANES - 2 * GDN_HEADS)))
    wp = dict(
        norm_mix0=norm_mix[0].reshape(1, d), norm_mix1=norm_mix[1].reshape(1, d),
        norm_ffn0=norm_ffn[0].reshape(1, d), norm_ffn1=norm_ffn[1].reshape(1, d),
        norm_out=norm_out.reshape(1, d),
        gdn_wqkv=win[:, :GDN_CONV_CH].astype(BF16),
        gdn_wz=win[:, GDN_CONV_CH:GDN_CONV_CH + GDN_V_W].astype(BF16),
        gdn_wab=wab.astype(BF16),
        gdn_conv_w=gdn_conv_w[0],
        gdn_alog=_pad_lanes(gdn_a_log[0]), gdn_dtb=_pad_lanes(gdn_dt_bias[0]),
        gdn_norm=gdn_norm[0].reshape(1, GDN_DV), gdn_wout=gdn_w_out[0].astype(BF16),
        lru_win=lru_w_in[0].astype(BF16), lru_conv_w=lru_conv_w[0], lru_conv_b=lru_conv_b[0].reshape(1, d),
        lru_wax=jnp.concatenate([lru_w_a[0], lru_w_x[0]], axis=-1).astype(BF16),
        lru_ba=lru_b_a[0].reshape(1, d), lru_bx=lru_b_x[0].reshape(1, d), lru_lam=lru_lambda[0].reshape(1, d),
        lru_wout=lru_w_out[0].astype(BF16),
        moe_w1=moe_w1, moe_w3=moe_w3, moe_w2=moe_w2,
    )
    for i in range(DEPTH):
        wp[f'wr{i}'], wp[f'br{i}'] = _router_weights(moe_w_rg[i], moe_b_rg[i], moe_w_re[i], moe_b_re[i])

    mods = _ada_mod(jnp.concatenate([c_prompt, c_sample], axis=0), w_ada, b_ada)
    mods = mods.reshape(DEPTH, bp + bs, 6, d)
    mods_p = [mods[i, :bp] for i in range(DEPTH)]
    mods_s = [mods[i, bp:] for i in range(DEPTH)]

    dt = x_prompt.dtype
    z_s = jnp.zeros((bp, GDN_HEADS, GDN_DK, GDN_DV), dt)
    z_gc = jnp.zeros((bp, CONV_W - 1, GDN_CONV_CH), dt)
    z_h = jnp.zeros((bp, 1, LRU_WIDTH), dt)
    z_lc = jnp.zeros((bp, CONV_W - 1, LRU_WIDTH), dt)
    y_p, gs_p, gc_p, lh_p, lc_p = _trunk(x_prompt, mods_p, z_s, z_gc, z_h, z_lc, wp)
    y_s, gs_s, gc_s, lh_s, lc_s = _trunk(x_sample, mods_s, state_gdn_S[0], state_gdn_conv[0],
                                         state_lru_h[0].reshape(bs, 1, LRU_WIDTH), state_lru_conv[0], wp)
    return (y_p, y_s, gs_p, gc_p, lh_p, lc_p, gs_s, gc_s, lh_s, lc_s)
```
